```python
import math
import jax, jax.numpy as jnp
from jax import lax
import numpy as np


D_MODEL = 2048
BATCH = 8
SEQ = 2048
DEPTH = 4

GRID_W = 64
CTX_LEN = 256
MIX_W = D_MODEL
GROUP_W = MIX_W // 4
POOL_GROUPS = 4
POOL_GROUP_DIM = GROUP_W // POOL_GROUPS
POOL_WINDOWS = (2, 4, 8, 16)
GLA_HEADS = 4
GLA_HEAD_DIM = GROUP_W // GLA_HEADS
GLA_GATE_RANK = 16
GLA_GATE_NORM = 16.0
GLA_CHUNK = 64
CONV_W = GROUP_W
CONV_K = 31
DIFF_HEADS = 4
DIFF_QK_DIM = GROUP_W // (2 * DIFF_HEADS)
DIFF_V_DIM = GROUP_W // DIFF_HEADS
ROPE_BASE = 10000.0
ROPE_AXIS_DIM = DIFF_QK_DIM // 2
Q_BLOCK = 128
N_EXPERTS = 16
EXPERT_FF = D_MODEL // 2
EC_CAPACITY = 2
NORM_EPS = 1e-6
IN_SIZES = (GROUP_W, GROUP_W, GROUP_W, GROUP_W, GROUP_W, GLA_GATE_RANK, GLA_GATE_RANK, 2 * CONV_W, GROUP_W, GROUP_W, GROUP_W)
IN_W = 8 * GROUP_W + 2 * GLA_GATE_RANK + 2 * CONV_W

kernel_name = 'hybrid_pool_gla_conformer_diffattn_ec_moe_dit'


def rms_norm(x, g):
    xf = x.astype(jnp.float32)
    y = xf * lax.rsqrt(jnp.mean(xf * xf, axis=-1, keepdims=True) + NORM_EPS)
    return (y * g.astype(jnp.float32)).astype(x.dtype)


def layer_norm(x, g, b):
    xf = x.astype(jnp.float32)
    mu = jnp.mean(xf, axis=-1, keepdims=True)
    var = jnp.mean(jnp.square(xf - mu), axis=-1, keepdims=True)
    y = (xf - mu) * lax.rsqrt(var + NORM_EPS) * g.astype(jnp.float32) + b.astype(jnp.float32)
    return y.astype(x.dtype)


def modulate(h, shift, scale):
    return h * (1.0 + scale) + shift


def split_cols(u):
    parts, start = [], 0
    for size in IN_SIZES:
        parts.append(u[..., start:start + size])
        start += size
    return parts


def multi_scale_pool(u, w_pool, scale):
    B, T, _ = u.shape
    ug = u.astype(jnp.float32).reshape(B, T, POOL_GROUPS, POOL_GROUP_DIM)
    cs = jnp.pad(jnp.cumsum(ug, axis=1), ((0, 0), (1, 0), (0, 0), (0, 0)))
    t = jnp.arange(T)
    outs = []
    for gi, w in enumerate(POOL_WINDOWS):
        lo = jnp.clip(t - w // 2, 0, T)
        hi = jnp.clip(t + w // 2, 0, T)
        mean = (cs[:, hi, gi] - cs[:, lo, gi]) / (hi - lo).astype(jnp.float32)[:, None]
        outs.append(mean - ug[:, :, gi])
    p = jnp.stack(outs, axis=2)
    y = jnp.einsum('btgc,gcd->btgd', p, w_pool.astype(jnp.float32)).reshape(B, T, GROUP_W)
    return (y * scale.astype(jnp.float32)).astype(u.dtype)


def gla_heads(a):
    B, T, _ = a.shape
    return a.reshape(B, T, GLA_HEADS, -1).transpose(0, 2, 1, 3).astype(jnp.float32)


def gla_inputs(q, k, v, low_f, low_b, up_f, bias_f, up_b, bias_b):
    g_f = jax.nn.log_sigmoid((low_f @ up_f + bias_f).astype(jnp.float32)) / GLA_GATE_NORM
    g_b = jax.nn.log_sigmoid((low_b @ up_b + bias_b).astype(jnp.float32)) / GLA_GATE_NORM
    return (gla_heads(q) * GLA_HEAD_DIM ** -0.5, gla_heads(k), gla_heads(v), gla_heads(g_f), gla_heads(g_b))


def gla_chunk_scan(q, k, v, g, s0):
    B, H, T, _ = q.shape
    n = T // GLA_CHUNK

    def chunks(a):
        return jnp.moveaxis(a.reshape(B, H, n, GLA_CHUNK, a.shape[-1]), 2, 0)

    mask = jnp.tril(jnp.ones((GLA_CHUNK, GLA_CHUNK), dtype=bool))[:, :, None]

    def step(s, inp):
        qc, kc, vc, gc = inp
        G = jnp.cumsum(gc, axis=2)
        G_last = G[:, :, -1:, :]
        o_inter = jnp.einsum('bhik,bhkv->bhiv', qc * jnp.exp(G), s)
        decay = jnp.exp(jnp.where(mask, G[:, :, :, None, :] - G[:, :, None, :, :], -jnp.inf))
        att = jnp.einsum('bhik,bhjk,bhijk->bhij', qc, kc, decay)
        o_intra = jnp.einsum('bhij,bhjv->bhiv', att, vc)
        s_new = jnp.exp(G_last[:, :, 0, :, None]) * s + jnp.einsum('bhjk,bhjv->bhkv', kc * jnp.exp(G_last - G), vc)
        return s_new, o_inter + o_intra

    s_fin, o = lax.scan(step, s0, (chunks(q), chunks(k), chunks(v), chunks(g)))
    return jnp.moveaxis(o, 0, 2).reshape(B, H, T, -1), s_fin


def gla_bidir(q, k, v, g_f, g_b, s0_f, s0_b):
    flip = lambda a: jnp.flip(a, axis=2)
    o_f, s_f = gla_chunk_scan(q, k, v, g_f, s0_f)
    o_b, s_b = gla_chunk_scan(flip(q), flip(k), flip(v), flip(g_b), s0_b)
    return o_f + flip(o_b), s_f, s_b


def gla_out(o, gate, norm_g):
    B, H, T, dv = o.shape
    o = rms_norm(o.transpose(0, 2, 1, 3), norm_g).reshape(B, T, H * dv)
    return (o * jax.nn.silu(gate.astype(jnp.float32))).astype(gate.dtype)


def conformer_conv(u, dw, dw_b, ln_g, ln_b, pw, pw_b):
    a, gt = jnp.split(u, 2, axis=-1)
    h = a * jax.nn.sigmoid(gt)
    h = lax.conv_general_dilated(h, dw[:, None, :].astype(h.dtype), window_strides=(1,),
                                 padding=[(CONV_K // 2, CONV_K // 2)],
                                 dimension_numbers=('NWC', 'WIO', 'NWC'),
                                 feature_group_count=CONV_W) + dw_b
    h = layer_norm(h, ln_g, ln_b)
    return jax.nn.silu(h) @ pw + pw_b


def rotate(x, cos, sin):
    x1, x2 = jnp.split(x, 2, axis=-1)
    return jnp.concatenate([x1 * cos - x2 * sin, x2 * cos + x1 * sin], axis=-1)


def rope_2d(x, rope):
    cos_r, sin_r, cos_c, sin_c = rope
    xr, xc = jnp.split(x, 2, axis=-1)
    return jnp.concatenate([rotate(xr, cos_r, sin_r), rotate(xc, cos_c, sin_c)], axis=-1)


def diff_split_qk(a):
    B, T, _ = a.shape
    a = a.reshape(B, T, DIFF_HEADS, 2, DIFF_QK_DIM).transpose(3, 0, 2, 1, 4)
    return a[0], a[1]


def diff_split_v(a):
    B, T, _ = a.shape
    return a.reshape(B, T, DIFF_HEADS, DIFF_V_DIM).transpose(0, 2, 1, 3)


def diff_maps(q1, q2, k1, k2, v, lam):
    scale = DIFF_QK_DIM ** -0.5
    s1 = jnp.einsum('bhqd,bhkd->bhqk', q1, k1).astype(jnp.float32) * scale
    s2 = jnp.einsum('bhqd,bhkd->bhqk', q2, k2).astype(jnp.float32) * scale
    p = jax.nn.softmax(s1, axis=-1) - lam * jax.nn.softmax(s2, axis=-1)
    return jnp.einsum('bhqk,bhkv->bhqv', p.astype(v.dtype), v)


def diff_latent(q1, q2, k1, k2, v, lam):
    B, H, N, _ = q1.shape
    nb = N // Q_BLOCK
    blk = lambda a: jnp.moveaxis(a.reshape(B, H, nb, Q_BLOCK, a.shape[-1]), 2, 0)
    o = lax.map(lambda qs: diff_maps(qs[0], qs[1], k1, k2, v, lam), (blk(q1), blk(q2)))
    return jnp.moveaxis(o, 0, 2).reshape(B, H, N, -1)


def diff_finish(o, subln_g, lam_init):
    B, H, T, dv = o.shape
    o = rms_norm(o, subln_g) * (1.0 - lam_init)
    return o.transpose(0, 2, 1, 3).reshape(B, T, H * dv)


def expert_choice_ffn(h, w_router, w_gate, w_up, w_down):
    B, n, _ = h.shape
    cap = EC_CAPACITY * n // N_EXPERTS
    aff = jax.nn.softmax((h @ w_router).astype(jnp.float32), axis=-1)
    gate, idx = lax.top_k(jnp.swapaxes(aff, 1, 2), cap)
    bidx = jnp.arange(B)[:, None, None]
    xe = h[bidx, idx]
    a = jnp.einsum('becd,edf->becf', xe, w_gate)
    u = jnp.einsum('becd,edf->becf', xe, w_up)
    y = jnp.einsum('becf,efd->becd', jax.nn.silu(a) * u, w_down) * gate[..., None].astype(h.dtype)
    return jnp.zeros_like(h).at[bidx, idx].add(y)


def setup_inputs(seed: int = 0) -> dict:
    key = jax.random.key(seed)
    ks = iter(jax.random.split(key, 40))
    nrm = lambda shape, s: jax.random.normal(next(ks), shape, jnp.float32) * s
    gain = lambda shape: 1.0 + nrm(shape, 0.02)
    L, D, F, E = DEPTH, D_MODEL, EXPERT_FF, N_EXPERTS
    return {
        'x': nrm((BATCH, SEQ, D), 1.0),
        'c': nrm((BATCH, D), 1.0),
        'ctx': nrm((BATCH, CTX_LEN, D), 1.0),
        'c_ctx': nrm((D,), 1.0),
        'w_ada': nrm((L, D, 6 * D), 0.5 * D ** -0.5),
        'b_ada': nrm((L, 6 * D), 0.02),
        'norm1_g': gain((L, D)),
        'norm2_g': gain((L, D)),
        'w_in': nrm((L, D, IN_W), D ** -0.5),
        'pool_w': nrm((L, POOL_GROUPS, POOL_GROUP_DIM, POOL_GROUP_DIM), POOL_GROUP_DIM ** -0.5),
        'pool_scale': gain((L, GROUP_W)),
        'gla_gk_up_f': nrm((L, GLA_GATE_RANK, GROUP_W), GLA_GATE_RANK ** -0.5),
        'gla_gk_bias_f': nrm((L, GROUP_W), 0.02),
        'gla_gk_up_b': nrm((L, GLA_GATE_RANK, GROUP_W), GLA_GATE_RANK ** -0.5),
        'gla_gk_bias_b': nrm((L, GROUP_W), 0.02),
        'gla_norm_g': gain((L, GLA_HEAD_DIM)),
        'conv_dw': nrm((L, CONV_K, CONV_W), CONV_K ** -0.5),
        'conv_dw_b': nrm((L, CONV_W), 0.02),
        'conv_ln_g': gain((L, CONV_W)),
        'conv_ln_b': nrm((L, CONV_W), 0.02),
        'conv_pw': nrm((L, CONV_W, CONV_W), CONV_W ** -0.5),
        'conv_pw_b': nrm((L, CONV_W), 0.02),
        'diff_lq1': nrm((L, DIFF_QK_DIM), 0.1),
        'diff_lk1': nrm((L, DIFF_QK_DIM), 0.1),
        'diff_lq2': nrm((L, DIFF_QK_DIM), 0.1),
        'diff_lk2': nrm((L, DIFF_QK_DIM), 0.1),
        'diff_subln_g': gain((L, DIFF_V_DIM)),
        'w_out': nrm((L, MIX_W, D), MIX_W ** -0.5),
        'w_router': nrm((L, D, E), D ** -0.5),
        'w_exp_gate': nrm((L, E, D, F), D ** -0.5),
        'w_exp_up': nrm((L, E, D, F), D ** -0.5),
        'w_exp_down': nrm((L, E, F, D), F ** -0.5),
        'final_norm_g': gain((D,)),
    }


def reference(x, c, ctx, c_ctx, w_ada, b_ada, norm1_g, norm2_g, w_in, pool_w, pool_scale,
              gla_gk_up_f, gla_gk_bias_f, gla_gk_up_b, gla_gk_bias_b, gla_norm_g,
              conv_dw, conv_dw_b, conv_ln_g, conv_ln_b, conv_pw, conv_pw_b,
              diff_lq1, diff_lk1, diff_lq2, diff_lk2, diff_subln_g,
              w_out, w_router, w_exp_gate, w_exp_up, w_exp_down, final_norm_g):
    B, N, D = x.shape
    ROWS = N // GRID_W
    row = jnp.repeat(jnp.arange(ROWS), GRID_W).astype(jnp.float32)
    col = jnp.tile(jnp.arange(GRID_W), ROWS).astype(jnp.float32)
    inv_freq = ROPE_BASE ** (-jnp.arange(0, ROPE_AXIS_DIM, 2, dtype=jnp.float32) / ROPE_AXIS_DIM)
    ang_r = row[:, None] * inv_freq
    ang_c = col[:, None] * inv_freq
    rope = (jnp.cos(ang_r).astype(x.dtype), jnp.sin(ang_r).astype(x.dtype),
            jnp.cos(ang_c).astype(x.dtype), jnp.sin(ang_c).astype(x.dtype))

    sc = jax.nn.silu(c)
    sc_ctx = jax.nn.silu(c_ctx)
    for l in range(DEPTH):
        last = l == DEPTH - 1
        mod_l = jnp.split((sc @ w_ada[l] + b_ada[l])[:, None, :], 6, axis=-1)
        mod_c = jnp.split(sc_ctx @ w_ada[l] + b_ada[l], 6, axis=-1)
        h_l = modulate(rms_norm(x, norm1_g[l]), mod_l[0], mod_l[1])
        h_c = modulate(rms_norm(ctx, norm1_g[l]), mod_c[0], mod_c[1])
        p_l = split_cols(h_l @ w_in[l])
        p_c = split_cols(h_c @ w_in[l])

        pool_l = multi_scale_pool(p_l[0], pool_w[l], pool_scale[l])

        gp = (gla_gk_up_f[l], gla_gk_bias_f[l], gla_gk_up_b[l], gla_gk_bias_b[l])
        gq_c, gk_c, gv_c, gf_c, gb_c = gla_inputs(p_c[1], p_c[2], p_c[3], p_c[5], p_c[6], *gp)
        gq_l, gk_l, gv_l, gf_l, gb_l = gla_inputs(p_l[1], p_l[2], p_l[3], p_l[5], p_l[6], *gp)
        s0 = jnp.zeros((B, GLA_HEADS, GLA_HEAD_DIM, GLA_HEAD_DIM), jnp.float32)
        go_c, s_f, s_b = gla_bidir(gq_c, gk_c, gv_c, gf_c, gb_c, s0, s0)
        go_l, _, _ = gla_bidir(gq_l, gk_l, gv_l, gf_l, gb_l, s_f, s_b)
        gla_l = gla_out(go_l, p_l[4], gla_norm_g[l])

        cp = (conv_dw[l], conv_dw_b[l], conv_ln_g[l], conv_ln_b[l], conv_pw[l], conv_pw_b[l])
        conv_l = conformer_conv(p_l[7], *cp)

        lam_init = 0.8 - 0.6 * math.exp(-0.3 * l)
        lam = (jnp.exp(jnp.sum(diff_lq1[l] * diff_lk1[l]).astype(jnp.float32))
               - jnp.exp(jnp.sum(diff_lq2[l] * diff_lk2[l]).astype(jnp.float32)) + lam_init)
        q1_c, q2_c = diff_split_qk(p_c[8])
        k1_c, k2_c = diff_split_qk(p_c[9])
        dv_c = diff_split_v(p_c[10])
        q1_l, q2_l = diff_split_qk(p_l[8])
        k1_l, k2_l = diff_split_qk(p_l[9])
        dv_l = diff_split_v(p_l[10])
        k1_all = jnp.concatenate([rope_2d(k1_l, rope), k1_c], axis=2)
        k2_all = jnp.concatenate([rope_2d(k2_l, rope), k2_c], axis=2)
        v_all = jnp.concatenate([dv_l, dv_c], axis=2)
        d_o = diff_latent(rope_2d(q1_l, rope), rope_2d(q2_l, rope), k1_all, k2_all, v_all, lam)
        diff_l = diff_finish(d_o, diff_subln_g[l], lam_init)

        x = x + mod_l[2] * (jnp.concatenate([pool_l, gla_l, conv_l, diff_l], axis=-1) @ w_out[l])
        h2_l = modulate(rms_norm(x, norm2_g[l]), mod_l[3], mod_l[4])
        x = x + mod_l[5] * expert_choice_ffn(h2_l, w_router[l], w_exp_gate[l], w_exp_up[l], w_exp_down[l])

        if not last:
            pool_c = multi_scale_pool(p_c[0], pool_w[l], pool_scale[l])
            gla_c = gla_out(go_c, p_c[4], gla_norm_g[l])
            conv_c = conformer_conv(p_c[7], *cp)
            diff_c = diff_finish(diff_maps(q1_c, q2_c, k1_c, k2_c, dv_c, lam), diff_subln_g[l], lam_init)
            ctx = ctx + mod_c[2] * (jnp.concatenate([pool_c, gla_c, conv_c, diff_c], axis=-1) @ w_out[l])
            h2_c = modulate(rms_norm(ctx, norm2_g[l]), mod_c[3], mod_c[4])
            ctx = ctx + mod_c[5] * expert_choice_ffn(h2_c, w_router[l], w_exp_gate[l], w_exp_up[l], w_exp_down[l])

    return rms_norm(x, final_norm_g)
```

```python
import functools
import math

import jax
import jax.numpy as jnp
from jax import lax
from jax.experimental import pallas as pl
from jax.experimental.pallas import tpu as pltpu

F32 = jnp.float32
BF16 = jnp.bfloat16

D_MODEL = 2048
GRID_W = 64
GROUP_W = D_MODEL // 4
POOL_WINDOWS = (2, 4, 8, 16)
HEAD_DIM = 128
N_HEADS = GROUP_W // HEAD_DIM
GLA_GATE_RANK = 16
GLA_GATE_NORM = 16.0
GLA_CHUNK = 64
CONV_K = 31
DIFF_QK_DIM = 64
ROPE_BASE = 10000.0
ROPE_AXIS_DIM = DIFF_QK_DIM // 2
N_EXPERTS = 16
EXPERT_FF = D_MODEL // 2
EC_CAPACITY = 2
NORM_EPS = 1e-6

LANE = 128
POOL_HALO = 8
CONV_HALO = 16
VMEM_LIMIT = 56 * 1024 * 1024

COL_CONV = 0
COL_POOL = 8
COL_GQ = 12
COL_GK = 16
COL_GV = 20
COL_GGATE = 24
COL_DQ = 28
COL_DK = 32
COL_DV = 36
COL_LOW = 40
IN_COLS = 42 * LANE
IN_TN = 7 * LANE

NT_DIMS = (((1,), (1,)), ((), ()))
TN_DIMS = (((0,), (0,)), ((), ()))


def _cparams(sem):
    return pltpu.CompilerParams(dimension_semantics=sem, vmem_limit_bytes=VMEM_LIMIT)


def _silu(x):
    return x * jax.nn.sigmoid(x)


def _rms(x, g):
    return x * lax.rsqrt(jnp.mean(x * x, axis=-1, keepdims=True) + NORM_EPS) * g


def _ada_kernel(c_ref, w_ref, b_ref, o_ref):
    sc = _silu(c_ref[...]).astype(BF16)
    o_ref[0] = jnp.dot(sc, w_ref[0].astype(BF16), preferred_element_type=F32) + b_ref[0]


def _ada_call(cc, w_ada, b_ada):
    L, D, W = w_ada.shape
    R = cc.shape[0]
    tn = 1536
    return pl.pallas_call(
        _ada_kernel,
        out_shape=jax.ShapeDtypeStruct((L, R, W), F32),
        grid=(L, W // tn),
        in_specs=[
            pl.BlockSpec((R, D), lambda l, j: (0, 0)),
            pl.BlockSpec((1, D, tn), lambda l, j: (l, 0, j)),
            pl.BlockSpec((1, 1, tn), lambda l, j: (l, 0, j)),
        ],
        out_specs=pl.BlockSpec((1, R, tn), lambda l, j: (l, 0, j)),
        compiler_params=_cparams(("arbitrary", "arbitrary")),
        name="ada_mod",
    )(cc, w_ada, b_ada.reshape(L, 1, W))


def _inproj_kernel(x_ref, sh_ref, sc_ref, g_ref, w_ref, o_ref, h_ref):
    @pl.when(pl.program_id(2) == 0)
    def _():
        h = _rms(x_ref[0], g_ref[...]) * (1.0 + sc_ref[0]) + sh_ref[0]
        h_ref[...] = h.astype(BF16)

    o_ref[0] = jnp.dot(h_ref[...], w_ref[...], preferred_element_type=F32)


def _inproj_call(x, shift, scale, g, w):
    Bx, T, D = x.shape
    tm = min(T, 1024)
    return pl.pallas_call(
        _inproj_kernel,
        out_shape=jax.ShapeDtypeStruct((Bx, T, IN_COLS), F32),
        grid=(Bx, T // tm, IN_COLS // IN_TN),
        in_specs=[
            pl.BlockSpec((1, tm, D), lambda b, i, j: (b, i, 0)),
            pl.BlockSpec((1, 1, D), lambda b, i, j: (b, 0, 0)),
            pl.BlockSpec((1, 1, D), lambda b, i, j: (b, 0, 0)),
            pl.BlockSpec((1, D), lambda b, i, j: (0, 0)),
            pl.BlockSpec((D, IN_TN), lambda b, i, j: (0, j)),
        ],
        out_specs=pl.BlockSpec((1, tm, IN_TN), lambda b, i, j: (b, i, j)),
        scratch_shapes=[pltpu.VMEM((tm, D), BF16)],
        compiler_params=_cparams(("arbitrary", "arbitrary", "arbitrary")),
        name="in_proj",
    )(x, shift, scale, g, w)


def _pool_kernel(prev_ref, cur_ref, next_ref, w_ref, s_ref, o_ref, pad_ref, *, T, tt):
    c = pl.program_id(1)
    H = POOL_HALO
    pad_ref[0:H, :] = jnp.where(c > 0, prev_ref[0], 0.0)
    pad_ref[H:H + tt, :] = cur_ref[0]
    pad_ref[H + tt:H + tt + H, :] = jnp.where(c < pl.num_programs(1) - 1, next_ref[0], 0.0)
    t = c * tt + lax.broadcasted_iota(jnp.int32, (tt, HEAD_DIM), 0)
    for gi, w in enumerate(POOL_WINDOWS):
        cols = slice(gi * HEAD_DIM, (gi + 1) * HEAD_DIM)
        acc = pad_ref[H - w // 2:H - w // 2 + tt, cols]
        for d in range(-w // 2 + 1, w // 2):
            acc = acc + pad_ref[H + d:H + d + tt, cols]
        cnt = jnp.minimum(t + w // 2, T) - jnp.maximum(t - w // 2, 0)
        p = acc / cnt.astype(F32) - pad_ref[H:H + tt, cols]
        y = jnp.dot(p.astype(BF16), w_ref[gi], preferred_element_type=F32)
        o_ref[0, :, cols] = (y * s_ref[:, cols]).astype(BF16)


def _pool_call(P, pool_w, pool_scale):
    B, T, _ = P.shape
    tt = min(T, 256)
    H = POOL_HALO
    cb = COL_POOL * LANE // GROUP_W
    nh = T // H
    return pl.pallas_call(
        functools.partial(_pool_kernel, T=T, tt=tt),
        out_shape=jax.ShapeDtypeStruct((B, T, GROUP_W), BF16),
        grid=(B, T // tt),
        in_specs=[
            pl.BlockSpec((1, H, GROUP_W), lambda b, c: (b, jnp.maximum(c * (tt // H) - 1, 0), cb)),
            pl.BlockSpec((1, tt, GROUP_W), lambda b, c: (b, c, cb)),
            pl.BlockSpec((1, H, GROUP_W), lambda b, c: (b, jnp.minimum((c + 1) * (tt // H), nh - 1), cb)),
            pl.BlockSpec((len(POOL_WINDOWS), HEAD_DIM, HEAD_DIM), lambda b, c: (0, 0, 0)),
            pl.BlockSpec((1, GROUP_W), lambda b, c: (0, 0)),
        ],
        out_specs=pl.BlockSpec((1, tt, GROUP_W), lambda b, c: (b, c, 0)),
        scratch_shapes=[pltpu.VMEM((tt + 2 * H, GROUP_W), F32)],
        compiler_params=_cparams(("arbitrary", "arbitrary")),
        name="pool_mixer",
    )(P, P, P, pool_w, pool_scale)


def _log_sigmoid(z):
    return jnp.minimum(z, 0.0) - jnp.log1p(jnp.exp(-jnp.abs(z)))


def _gla_chunk(q, k, v, g, St, tri, keep, tot_row):
    g_hi = g.astype(BF16)
    g_lo = (g - g_hi.astype(F32)).astype(BF16)
    G = jnp.dot(tri, g_hi, preferred_element_type=F32) + jnp.dot(tri, g_lo, preferred_element_type=F32)
    Gtot = G[tot_row:tot_row + 1, :]
    r = 0.5 * Gtot
    qs = q * (HEAD_DIM ** -0.5)
    qg = (qs * jnp.exp(G - r)).astype(BF16)
    kg = (k * jnp.exp(r - G)).astype(BF16)
    att = lax.dot_general(qg, kg, NT_DIMS, preferred_element_type=F32)
    att = jnp.where(keep, att, 0.0).astype(BF16)
    vb = v.astype(BF16)
    o = jnp.dot(att, vb, preferred_element_type=F32)
    qG = (qs * jnp.exp(G)).astype(BF16)
    o = o + lax.dot_general(qG, St.astype(BF16), NT_DIMS, preferred_element_type=F32)
    kd = (k * jnp.exp(Gtot - G)).astype(BF16)
    St_new = St * jnp.exp(Gtot) + lax.dot_general(vb, kd, TN_DIMS, preferred_element_type=F32)
    return o, St_new


def _gla_kernel(ql, kl, vl, gtl, lowl, qc, kc, vc, gtc, lowc, upf, bf, upb, bb, ng,
                ol_ref, oc_ref, gfl, gbl, gfc, gbc, ofl, obl, ofc, obc):
    C = GLA_CHUNK
    T = ql.shape[1]
    Tc = qc.shape[1]

    def gates(low_ref, gf_ref, gb_ref):
        low = low_ref[0].astype(BF16)
        zf = jnp.dot(low, upf[...], preferred_element_type=F32) + bf[...]
        zb = jnp.dot(low, upb[...], preferred_element_type=F32) + bb[...]
        gf_ref[...] = _log_sigmoid(zf) * (1.0 / GLA_GATE_NORM)
        gb_ref[...] = _log_sigmoid(zb) * (1.0 / GLA_GATE_NORM)

    gates(lowc, gfc, gbc)
    gates(lowl, gfl, gbl)

    ri = lax.broadcasted_iota(jnp.int32, (C, C), 0)
    ci = lax.broadcasted_iota(jnp.int32, (C, C), 1)
    keep_f = ci <= ri
    keep_b = ci >= ri
    tri_f = jnp.where(keep_f, 1.0, 0.0).astype(BF16)
    tri_b = jnp.where(keep_b, 1.0, 0.0).astype(BF16)

    def scan(q_ref, k_ref, v_ref, gf_ref, gb_ref, of_ref, ob_ref, n, Sf, Sb):
        def body(i, carry):
            Sf, Sb = carry
            rf = pl.ds(pl.multiple_of(i * C, C), C)
            o, Sf = _gla_chunk(q_ref[0, rf, :], k_ref[0, rf, :], v_ref[0, rf, :], gf_ref[rf, :],
                               Sf, tri_f, keep_f, C - 1)
            of_ref[rf, :] = o
            rb = pl.ds(pl.multiple_of((n - 1 - i) * C, C), C)
            o, Sb = _gla_chunk(q_ref[0, rb, :], k_ref[0, rb, :], v_ref[0, rb, :], gb_ref[rb, :],
                               Sb, tri_b, keep_b, 0)
            ob_ref[rb, :] = o
            return Sf, Sb

        return lax.fori_loop(0, n, body, (Sf, Sb))

    S0 = jnp.zeros((HEAD_DIM, HEAD_DIM), F32)
    Sf, Sb = scan(qc, kc, vc, gfc, gbc, ofc, obc, Tc // C, S0, S0)
    scan(ql, kl, vl, gfl, gbl, ofl, obl, T // C, Sf, Sb)

    def finish(of_ref, ob_ref, gt_ref, o_ref):
        o = _rms(of_ref[...] + ob_ref[...], ng[...])
        o_ref[0] = (o * _silu(gt_ref[0])).astype(BF16)

    finish(ofl, obl, gtl, ol_ref)
    finish(ofc, obc, gtc, oc_ref)


def _gla_call(Pl, Pc, upf, bias_f, upb, bias_b, norm_g):
    B, T, _ = Pl.shape
    Tc = Pc.shape[1]

    def colspec(Tx, col):
        return pl.BlockSpec((1, Tx, LANE), lambda b, h: (b, 0, col + h))

    def lowspec(Tx):
        return pl.BlockSpec((1, Tx, LANE), lambda b, h: (b, 0, COL_LOW))

    headw = pl.BlockSpec((LANE, LANE), lambda b, h: (0, h))
    headv = pl.BlockSpec((1, LANE), lambda b, h: (0, h))
    outspec = lambda Tx: pl.BlockSpec((1, Tx, LANE), lambda b, h: (b, 0, h))
    return pl.pallas_call(
        _gla_kernel,
        out_shape=(jax.ShapeDtypeStruct((B, T, GROUP_W), BF16), jax.ShapeDtypeStruct((B, Tc, GROUP_W), BF16)),
        grid=(B, N_HEADS),
        in_specs=[colspec(T, COL_GQ), colspec(T, COL_GK), colspec(T, COL_GV), colspec(T, COL_GGATE), lowspec(T),
                  colspec(Tc, COL_GQ), colspec(Tc, COL_GK), colspec(Tc, COL_GV), colspec(Tc, COL_GGATE), lowspec(Tc),
                  headw, headv, headw, headv,
                  pl.BlockSpec((1, LANE), lambda b, h: (0, 0))],
        out_specs=(outspec(T), outspec(Tc)),
        scratch_shapes=[pltpu.VMEM((T, LANE), F32), pltpu.VMEM((T, LANE), F32),
                        pltpu.VMEM((Tc, LANE), F32), pltpu.VMEM((Tc, LANE), F32),
                        pltpu.VMEM((T, LANE), F32), pltpu.VMEM((T, LANE), F32),
                        pltpu.VMEM((Tc, LANE), F32), pltpu.VMEM((Tc, LANE), F32)],
        compiler_params=_cparams(("arbitrary", "arbitrary")),
        name="gla_mixer",
    )(Pl, Pl, Pl, Pl, Pl, Pc, Pc, Pc, Pc, Pc, upf, bias_f, upb, bias_b, norm_g)


def _conv_kernel(prev_ref, cur_ref, next_ref, dw_ref, dwb_ref, lng_ref, lnb_ref, pw_ref, pwb_ref,
                 o_ref, pad_ref, acc_ref, *, tt):
    c = pl.program_id(1)
    H = CONV_HALO
    W = GROUP_W

    def glu(u):
        return u[:, :W] * jax.nn.sigmoid(u[:, W:])

    pad_ref[0:H, :] = jnp.where(c > 0, glu(prev_ref[0]), 0.0)
    pad_ref[H:H + tt, :] = glu(cur_ref[0])
    pad_ref[H + tt:H + tt + H, :] = jnp.where(c < pl.num_programs(1) - 1, glu(next_ref[0]), 0.0)

    rs = min(tt, 128)
    off = H - CONV_K // 2
    for r0 in range(0, tt, rs):
        for lb in range(W // LANE):
            cols = slice(lb * LANE, (lb + 1) * LANE)
            acc = jnp.zeros((rs, LANE), F32) + dwb_ref[:, cols]
            for k in range(CONV_K):
                acc = acc + pad_ref[r0 + off + k:r0 + off + k + rs, cols] * dw_ref[k:k + 1, cols]
            acc_ref[r0:r0 + rs, cols] = acc

    h = acc_ref[...]
    mu = jnp.mean(h, axis=-1, keepdims=True)
    hc = h - mu
    var = jnp.mean(hc * hc, axis=-1, keepdims=True)
    y = hc * lax.rsqrt(var + NORM_EPS) * lng_ref[...] + lnb_ref[...]
    y = jnp.dot(_silu(y).astype(BF16), pw_ref[...], preferred_element_type=F32) + pwb_ref[...]
    o_ref[0] = y.astype(BF16)


def _conv_call(P, dw, dw_b, ln_g, ln_b, pw, pw_b):
    B, T, _ = P.shape
    tt = min(T, 256)
    H = CONV_HALO
    W2 = 2 * GROUP_W
    nh = T // H
    vec = pl.BlockSpec((1, GROUP_W), lambda b, c: (0, 0))
    return pl.pallas_call(
        functools.partial(_conv_kernel, tt=tt),
        out_shape=jax.ShapeDtypeStruct((B, T, GROUP_W), BF16),
        grid=(B, T // tt),
        in_specs=[
            pl.BlockSpec((1, H, W2), lambda b, c: (b, jnp.maximum(c * (tt // H) - 1, 0), 0)),
            pl.BlockSpec((1, tt, W2), lambda b, c: (b, c, 0)),
            pl.BlockSpec((1, H, W2), lambda b, c: (b, jnp.minimum((c + 1) * (tt // H), nh - 1), 0)),
            pl.BlockSpec((CONV_K, GROUP_W), lambda b, c: (0, 0)),
            vec, vec, vec,
            pl.BlockSpec((GROUP_W, GROUP_W), lambda b, c: (0, 0)),
            vec,
        ],
        out_specs=pl.BlockSpec((1, tt, GROUP_W), lambda b, c: (b, c, 0)),
        scratch_shapes=[pltpu.VMEM((tt + 2 * H, GROUP_W), F32), pltpu.VMEM((tt, GROUP_W), F32)],
        compiler_params=_cparams(("arbitrary", "arbitrary")),
        name="conv_mixer",
    )(P, P, P, dw, dw_b, ln_g, ln_b, pw, pw_b)


def _rope(x, cos, sin):
    hw = ROPE_AXIS_DIM // 2
    lane = lax.broadcasted_iota(jnp.int32, x.shape, 1)
    first_half = (lane % (2 * hw)) < hw
    swapped = jnp.where(first_half, pltpu.roll(x, LANE - hw, 1), pltpu.roll(x, hw, 1))
    return x * cos + swapped * sin


def _diff_kernel(*refs, rope, lam_init, n_kv):
    it = iter(refs)
    q_ref = next(it)
    kv = [(next(it), next(it)) for _ in range(n_kv)]
    if rope:
        cq, sq, ck, sk = next(it), next(it), next(it), next(it)
    lq1, lk1, lq2, lk2, sg = next(it), next(it), next(it), next(it), next(it)
    o_ref, kbuf, vbuf = next(it), next(it), next(it)

    @pl.when(pl.program_id(2) == 0)
    def _():
        r0 = 0
        for i, (k_ref, v_ref) in enumerate(kv):
            n = k_ref.shape[1]
            k = k_ref[0]
            if rope and i == 0:
                k = _rope(k, ck[...], sk[...])
            kbuf[r0:r0 + n, :] = k.astype(BF16)
            vbuf[r0:r0 + n, :] = v_ref[0].astype(BF16)
            r0 += n

    q = q_ref[0]
    if rope:
        q = _rope(q, cq[...], sq[...])
    q = q * (DIFF_QK_DIM ** -0.5)
    lane = lax.broadcasted_iota(jnp.int32, q.shape, 1)
    q1 = jnp.where(lane < DIFF_QK_DIM, q, 0.0).astype(BF16)
    q2 = jnp.where(lane >= DIFF_QK_DIM, q, 0.0).astype(BF16)
    k = kbuf[...]
    s1 = lax.dot_general(q1, k, NT_DIMS, preferred_element_type=F32)
    s2 = lax.dot_general(q2, k, NT_DIMS, preferred_element_type=F32)
    e1 = jnp.exp(s1 - jnp.max(s1, axis=-1, keepdims=True))
    e2 = jnp.exp(s2 - jnp.max(s2, axis=-1, keepdims=True))
    lam = (jnp.exp(jnp.sum(lq1[...] * lk1[...], axis=-1, keepdims=True))
           - jnp.exp(jnp.sum(lq2[...] * lk2[...], axis=-1, keepdims=True)) + lam_init)
    p = e1 * (1.0 / jnp.sum(e1, axis=-1, keepdims=True)) - e2 * (lam / jnp.sum(e2, axis=-1, keepdims=True))
    o = jnp.dot(p.astype(BF16), vbuf[...], preferred_element_type=F32)
    o_ref[0] = (_rms(o, sg[...]) * (1.0 - lam_init)).astype(BF16)


def _diff_call(Pq, kv_sources, rope_tabs, lq1, lk1, lq2, lk2, subln_g, lam_init):
    B, T, _ = Pq.shape
    tq = min(T, 256)
    rope = rope_tabs is not None
    in_specs = [pl.BlockSpec((1, tq, LANE), lambda b, h, i: (b, i, COL_DQ + h))]
    args = [Pq]
    Tk = 0
    for Ps in kv_sources:
        n = Ps.shape[1]
        in_specs.append(pl.BlockSpec((1, n, LANE), lambda b, h, i: (b, 0, COL_DK + h)))
        in_specs.append(pl.BlockSpec((1, n, LANE), lambda b, h, i: (b, 0, COL_DV + h)))
        args += [Ps, Ps]
        Tk += n
    if rope:
        cos, sin = rope_tabs
        in_specs += [pl.BlockSpec((tq, LANE), lambda b, h, i: (i, 0)), pl.BlockSpec((tq, LANE), lambda b, h, i: (i, 0)),
                     pl.BlockSpec((T, LANE), lambda b, h, i: (0, 0)), pl.BlockSpec((T, LANE), lambda b, h, i: (0, 0))]
        args += [cos, sin, cos, sin]
    small = pl.BlockSpec((1, DIFF_QK_DIM), lambda b, h, i: (0, 0))
    in_specs += [small, small, small, small, pl.BlockSpec((1, LANE), lambda b, h, i: (0, 0))]
    args += [lq1, lk1, lq2, lk2, subln_g]
    return pl.pallas_call(
        functools.partial(_diff_kernel, rope=rope, lam_init=lam_init, n_kv=len(kv_sources)),
        out_shape=jax.ShapeDtypeStruct((B, T, GROUP_W), BF16),
        grid=(B, N_HEADS, T // tq),
        in_specs=in_specs,
        out_specs=pl.BlockSpec((1, tq, LANE), lambda b, h, i: (b, i, h)),
        scratch_shapes=[pltpu.VMEM((Tk, LANE), BF16), pltpu.VMEM((Tk, LANE), BF16)],
        compiler_params=_cparams(("arbitrary", "arbitrary", "arbitrary")),
        name="diff_attn",
    )(*args)


def _outproj_kernel(a_ref, b_ref, c_ref, d_ref, w_ref, x_ref, gate_ref, sh_ref, sc_ref, ng_ref, wrh_ref, wrl_ref,
                    x1_ref, h2_ref, aff_ref):
    W = GROUP_W
    y = jnp.dot(a_ref[0], w_ref[0:W, :], preferred_element_type=F32)
    y = y + jnp.dot(b_ref[0], w_ref[W:2 * W, :], preferred_element_type=F32)
    y = y + jnp.dot(c_ref[0], w_ref[2 * W:3 * W, :], preferred_element_type=F32)
    y = y + jnp.dot(d_ref[0], w_ref[3 * W:4 * W, :], preferred_element_type=F32)
    x1 = x_ref[0] + gate_ref[0] * y
    x1_ref[0] = x1
    h = _rms(x1, ng_ref[...]) * (1.0 + sc_ref[0]) + sh_ref[0]
    hh = h.astype(BF16)
    hl = (h - hh.astype(F32)).astype(BF16)
    h2_ref[0] = hh
    lg = (jnp.dot(hh, wrh_ref[...], preferred_element_type=F32)
          + jnp.dot(hl, wrh_ref[...], preferred_element_type=F32)
          + jnp.dot(hh, wrl_ref[...], preferred_element_type=F32))
    lane = lax.broadcasted_iota(jnp.int32, lg.shape, 1)
    lg = jnp.where(lane < N_EXPERTS, lg, -jnp.inf)
    e = jnp.exp(lg - jnp.max(lg, axis=-1, keepdims=True))
    aff_ref[0] = e / jnp.sum(e, axis=-1, keepdims=True)


def _outproj_call(mix, w_out, x, gate, shift, scale, ng, wr_hi, wr_lo):
    Bx, T, D = x.shape
    tm = min(T, 256)
    mixspec = pl.BlockSpec((1, tm, GROUP_W), lambda b, i: (b, i, 0))
    modspec = pl.BlockSpec((1, 1, D), lambda b, i: (b, 0, 0))
    rowspec = pl.BlockSpec((1, tm, D), lambda b, i: (b, i, 0))
    return pl.pallas_call(
        _outproj_kernel,
        out_shape=(jax.ShapeDtypeStruct((Bx, T, D), F32), jax.ShapeDtypeStruct((Bx, T, D), BF16),
                   jax.ShapeDtypeStruct((Bx, T, LANE), F32)),
        grid=(Bx, T // tm),
        in_specs=[mixspec, mixspec, mixspec, mixspec,
                  pl.BlockSpec((D, D), lambda b, i: (0, 0)),
                  rowspec, modspec, modspec, modspec,
                  pl.BlockSpec((1, D), lambda b, i: (0, 0)),
                  pl.BlockSpec((D, LANE), lambda b, i: (0, 0)),
                  pl.BlockSpec((D, LANE), lambda b, i: (0, 0))],
        out_specs=(rowspec, rowspec, pl.BlockSpec((1, tm, LANE), lambda b, i: (b, i, 0))),
        compiler_params=_cparams(("arbitrary", "arbitrary")),
        name="out_proj",
    )(*mix, w_out, x, gate, shift, scale, ng, wr_hi, wr_lo)


def _rank_kernel(aff_ref, rank_row_ref, rank_col_ref, aff_row_ref, at_ref, rk_ref):
    T = aff_ref.shape[1]
    R = LANE
    nr = T // R
    at_ref[...] = aff_ref[0].T
    rk_ref[...] = jnp.zeros_like(rk_ref)
    ri = lax.broadcasted_iota(jnp.int32, (R, R), 0)
    ci = lax.broadcasted_iota(jnp.int32, (R, R), 1)
    later = jnp.where(ri > ci, 1, 0)

    def per_expert(e, carry):
        row = at_ref[pl.ds(e, 1), :]
        krow = lax.bitcast_convert_type(row, jnp.int32)
        acc = jnp.zeros((8, T), jnp.int32)
        for r in range(nr):
            seg = row[:, r * R:(r + 1) * R]
            kcol = lax.bitcast_convert_type(jnp.broadcast_to(seg, (R, R)).T, jnp.int32)
            pieces = []
            if r > 0:
                pieces.append(jnp.where(kcol[:, :1] - 1 >= krow[:, :r * R], 1, 0))
            pieces.append(jnp.where(kcol - later >= krow[:, r * R:(r + 1) * R], 1, 0))
            if r < nr - 1:
                pieces.append(jnp.where(kcol[:, :1] >= krow[:, (r + 1) * R:], 1, 0))
            beats = jnp.concatenate(pieces, axis=1) if len(pieces) > 1 else pieces[0]
            acc = acc + jnp.sum(beats.reshape(R // 8, 8, T), axis=0)
        rank = jnp.sum(acc, axis=0, keepdims=True) - 1
        rk_ref[pl.ds(e, 1), :] = rank.astype(F32)
        rank_row_ref[0, e] = rank
        aff_row_ref[0, e] = row
        return carry

    lax.fori_loop(0, N_EXPERTS, per_expert, 0)
    rank_col_ref[0] = rk_ref[...].T.astype(jnp.int32)


def _rank_call(aff):
    B, T, _ = aff.shape
    return pl.pallas_call(
        _rank_kernel,
        out_shape=(jax.ShapeDtypeStruct((B, N_EXPERTS, 1, T), jnp.int32),
                   jax.ShapeDtypeStruct((B, T, LANE), jnp.int32),
                   jax.ShapeDtypeStruct((B, N_EXPERTS, 1, T), F32)),
        grid=(B,),
        in_specs=[pl.BlockSpec((1, T, LANE), lambda b: (b, 0, 0))],
        out_specs=(pl.BlockSpec((1, N_EXPERTS, 1, T), lambda b: (b, 0, 0, 0)),
                   pl.BlockSpec((1, T, LANE), lambda b: (b, 0, 0)),
                   pl.BlockSpec((1, N_EXPERTS, 1, T), lambda b: (b, 0, 0, 0))),
        scratch_shapes=[pltpu.VMEM((LANE, T), F32), pltpu.VMEM((LANE, T), F32)],
        compiler_params=_cparams(("arbitrary",)),
        name="router_rank",
    )(aff)


def _dispatch_kernel(rank_ref, affr_ref, h_ref, xe_ref, gs_ref):
    cap = xe_ref.shape[1]
    T = h_ref.shape[1]
    slot = lax.broadcasted_iota(jnp.int32, (cap, T), 0)
    hit = rank_ref[0, 0] == slot
    xe_ref[0] = jnp.dot(jnp.where(hit, 1.0, 0.0).astype(BF16), h_ref[0],
                        preferred_element_type=F32).astype(BF16)
    g = jnp.sum(jnp.where(hit, affr_ref[0, 0], 0.0), axis=-1, keepdims=True)
    gs_ref[0] = jnp.broadcast_to(g, (cap, LANE))


def _dispatch_call(rank_row, aff_row, h2, cap):
    B, T, D = h2.shape
    E = N_EXPERTS
    return pl.pallas_call(
        _dispatch_kernel,
        out_shape=(jax.ShapeDtypeStruct((E, B * cap, D), BF16), jax.ShapeDtypeStruct((E, B * cap, LANE), F32)),
        grid=(B, E),
        in_specs=[pl.BlockSpec((1, 1, 1, T), lambda b, e: (b, e, 0, 0)),
                  pl.BlockSpec((1, 1, 1, T), lambda b, e: (b, e, 0, 0)),
                  pl.BlockSpec((1, T, D), lambda b, e: (b, 0, 0))],
        out_specs=(pl.BlockSpec((1, cap, D), lambda b, e: (e, b, 0)),
                   pl.BlockSpec((1, cap, LANE), lambda b, e: (e, b, 0))),
        compiler_params=_cparams(("arbitrary", "arbitrary")),
        name="moe_dispatch",
    )(rank_row, aff_row, h2)


def _expert_kernel(xe_ref, wg_ref, wu_ref, wd_ref, gs_ref, y_ref, acc_ref):
    f = pl.program_id(2)
    x = xe_ref[0]
    a = jnp.dot(x, wg_ref[0].astype(BF16), preferred_element_type=F32)
    u = jnp.dot(x, wu_ref[0].astype(BF16), preferred_element_type=F32)
    part = jnp.dot((_silu(a) * u).astype(BF16), wd_ref[0].astype(BF16), preferred_element_type=F32)

    @pl.when(f == 0)
    def _():
        acc_ref[...] = part

    @pl.when(f > 0)
    def _():
        acc_ref[...] += part

    @pl.when(f == pl.num_programs(2) - 1)
    def _():
        y_ref[0] = (acc_ref[...] * gs_ref[0][:, 0:1]).astype(BF16)


def _expert_call(xe, gs, w_gate, w_up, w_down):
    E, M, D = xe.shape
    F = w_gate.shape[-1]
    tm = min(M, 1024)
    tf = 256
    return pl.pallas_call(
        _expert_kernel,
        out_shape=jax.ShapeDtypeStruct((E, M, D), BF16),
        grid=(E, M // tm, F // tf),
        in_specs=[pl.BlockSpec((1, tm, D), lambda e, m, f: (e, m, 0)),
                  pl.BlockSpec((1, D, tf), lambda e, m, f: (e, 0, f)),
                  pl.BlockSpec((1, D, tf), lambda e, m, f: (e, 0, f)),
                  pl.BlockSpec((1, tf, D), lambda e, m, f: (e, f, 0)),
                  pl.BlockSpec((1, tm, LANE), lambda e, m, f: (e, m, 0))],
        out_specs=pl.BlockSpec((1, tm, D), lambda e, m, f: (e, m, 0)),
        scratch_shapes=[pltpu.VMEM((tm, D), F32)],
        compiler_params=_cparams(("arbitrary", "arbitrary", "arbitrary")),
        name="moe_experts",
    )(xe, w_gate, w_up, w_down, gs)


def _combine_kernel(rank_ref, y_ref, x_ref, gate_ref, fg_ref, o_ref, *, final_norm):
    cap = y_ref.shape[1]
    tm = x_ref.shape[1]
    slot = lax.broadcasted_iota(jnp.int32, (tm, cap), 1)
    rank = rank_ref[0]
    acc = jnp.zeros(x_ref.shape[1:], F32)
    for e in range(N_EXPERTS):
        hit = jnp.where(rank[:, e:e + 1] == slot, 1.0, 0.0).astype(BF16)
        acc = acc + jnp.dot(hit, y_ref[e], preferred_element_type=F32)
    x2 = x_ref[0] + gate_ref[0] * acc
    if final_norm:
        x2 = _rms(x2, fg_ref[...])
    o_ref[0] = x2


def _combine_call(rank_col, y, x1, gate, final_g, cap, final_norm):
    B, T, D = x1.shape
    tm = min(T, 512)
    return pl.pallas_call(
        functools.partial(_combine_kernel, final_norm=final_norm),
        out_shape=jax.ShapeDtypeStruct((B, T, D), F32),
        grid=(B, T // tm),
        in_specs=[pl.BlockSpec((1, tm, LANE), lambda b, i: (b, i, 0)),
                  pl.BlockSpec((N_EXPERTS, cap, D), lambda b, i: (0, b, 0)),
                  pl.BlockSpec((1, tm, D), lambda b, i: (b, i, 0)),
                  pl.BlockSpec((1, 1, D), lambda b, i: (b, 0, 0)),
                  pl.BlockSpec((1, D), lambda b, i: (0, 0))],
        out_specs=pl.BlockSpec((1, tm, D), lambda b, i: (b, i, 0)),
        compiler_params=_cparams(("arbitrary", "arbitrary")),
        name="moe_combine",
    )(rank_col, y, x1, gate, final_g)


def _moe(h2, aff, x1, gate, final_g, w_gate, w_up, w_down, final_norm):
    B, T, _ = h2.shape
    cap = EC_CAPACITY * T // N_EXPERTS
    rank_row, rank_col, aff_row = _rank_call(aff)
    xe, gs = _dispatch_call(rank_row, aff_row, h2, cap)
    y = _expert_call(xe, gs, w_gate, w_up, w_down)
    return _combine_call(rank_col, y, x1, gate, final_g, cap, final_norm)


def _rope_tables(n):
    rows = n // GRID_W
    row = jnp.repeat(jnp.arange(rows), GRID_W).astype(F32)
    col = jnp.tile(jnp.arange(GRID_W), rows).astype(F32)
    inv_freq = ROPE_BASE ** (-jnp.arange(0, ROPE_AXIS_DIM, 2, dtype=F32) / ROPE_AXIS_DIM)
    ar = row[:, None] * inv_freq
    ac = col[:, None] * inv_freq
    cos = jnp.concatenate([jnp.cos(ar), jnp.cos(ar), jnp.cos(ac), jnp.cos(ac)], axis=-1)
    sin = jnp.concatenate([-jnp.sin(ar), jnp.sin(ar), -jnp.sin(ac), jnp.sin(ac)], axis=-1)
    return jnp.tile(cos, (1, 2)), jnp.tile(sin, (1, 2))


def _reorder_w_in(w_in):
    G = GROUP_W
    o = 0
    parts = {}
    for name, size in (("pool", G), ("gq", G), ("gk", G), ("gv", G), ("gg", G), ("low", 2 * GLA_GATE_RANK),
                       ("conv", 2 * G), ("dq", G), ("dk", G), ("dv", G)):
        parts[name] = w_in[..., o:o + size]
        o += size
    pad = jnp.zeros(w_in.shape[:-1] + (IN_COLS - COL_LOW * LANE - 2 * GLA_GATE_RANK,), w_in.dtype)
    order = ["conv", "pool", "gq", "gk", "gv", "gg", "dq", "dk", "dv", "low"]
    return jnp.concatenate([parts[n] for n in order] + [pad], axis=-1).astype(BF16)


def kernel(x, c, ctx, c_ctx, w_ada, b_ada, norm1_g, norm2_g, w_in, pool_w, pool_scale, gla_gk_up_f, gla_gk_bias_f, gla_gk_up_b, gla_gk_bias_b, gla_norm_g, conv_dw, conv_dw_b, conv_ln_g, conv_ln_b, conv_pw, conv_pw_b, diff_lq1, diff_lk1, diff_lq2, diff_lk2, diff_subln_g, w_out, w_router, w_exp_gate, w_exp_up, w_exp_down, final_norm_g):
    B, N, D = x.shape
    Tc = ctx.shape[1]
    L = w_ada.shape[0]
    assert D == D_MODEL and N % 256 == 0 and Tc % 128 == 0

    rows = 16
    cc = jnp.concatenate([c, c_ctx[None, :], jnp.zeros((rows - B - 1, D), F32)], axis=0)
    mod = _ada_call(cc, w_ada, b_ada).reshape(L, rows, 6, 1, D)

    w_in_r = _reorder_w_in(w_in)
    w_out_b = w_out.astype(BF16)
    pool_w_b = pool_w.astype(BF16)
    conv_pw_b16 = conv_pw.astype(BF16)
    R = GLA_GATE_RANK
    zpad = lambda a, lo: jnp.pad(a, ((0, 0), (lo, LANE - R - lo), (0, 0))).astype(BF16)
    upf = zpad(gla_gk_up_f, 0)
    upb = zpad(gla_gk_up_b, R)
    wr = jnp.pad(w_router, ((0, 0), (0, 0), (0, LANE - N_EXPERTS)))
    wr_hi = wr.astype(BF16)
    wr_lo = (wr - wr_hi.astype(F32)).astype(BF16)
    rope_tabs = _rope_tables(N)
    row2 = lambda a: a.reshape(1, -1)
    fg = row2(final_norm_g)

    for l in range(L):
        last = l == L - 1
        ml = [mod[l, :B, k] for k in range(6)]
        mc = [mod[l, B:B + 1, k] for k in range(6)]
        lam_init = 0.8 - 0.6 * math.exp(-0.3 * l)

        Pl = _inproj_call(x, ml[0], ml[1], row2(norm1_g[l]), w_in_r[l])
        Pc = _inproj_call(ctx.reshape(1, B * Tc, D), mc[0], mc[1], row2(norm1_g[l]), w_in_r[l]).reshape(B, Tc, IN_COLS)

        gla_l, gla_c = _gla_call(Pl, Pc, upf[l], row2(gla_gk_bias_f[l]), upb[l], row2(gla_gk_bias_b[l]),
                                 row2(gla_norm_g[l]))
        conv_args = (conv_dw[l], row2(conv_dw_b[l]), row2(conv_ln_g[l]), row2(conv_ln_b[l]), conv_pw_b16[l],
                     row2(conv_pw_b[l]))
        diff_args = (row2(diff_lq1[l]), row2(diff_lk1[l]), row2(diff_lq2[l]), row2(diff_lk2[l]),
                     row2(diff_subln_g[l]), lam_init)
        pool_l = _pool_call(Pl, pool_w_b[l], row2(pool_scale[l]))
        conv_l = _conv_call(Pl, *conv_args)
        diff_l = _diff_call(Pl, [Pl, Pc], rope_tabs, *diff_args)

        x1, h2, aff = _outproj_call((pool_l, gla_l, conv_l, diff_l), w_out_b[l], x, ml[2], ml[3], ml[4],
                                    row2(norm2_g[l]), wr_hi[l], wr_lo[l])
        x = _moe(h2, aff, x1, ml[5], fg, w_exp_gate[l], w_exp_up[l], w_exp_down[l], final_norm=last)

        if not last:
            pool_c = _pool_call(Pc, pool_w_b[l], row2(pool_scale[l]))
            conv_c = _conv_call(Pc, *conv_args)
            diff_c = _diff_call(Pc, [Pc], None, *diff_args)
            flat = lambda a: a.reshape(1, B * Tc, a.shape[-1])
            c1, h2c, affc = _outproj_call((flat(pool_c), flat(gla_c), flat(conv_c), flat(diff_c)), w_out_b[l],
                                          flat(ctx), mc[2], mc[3], mc[4], row2(norm2_g[l]), wr_hi[l], wr_lo[l])
            unflat = lambda a: a.reshape(B, Tc, a.shape[-1])
            gate_c = jnp.broadcast_to(mc[5], (B, 1, D))
            ctx = _moe(unflat(h2c), unflat(affc), unflat(c1), gate_c, fg, w_exp_gate[l], w_exp_up[l],
                       w_exp_down[l], final_norm=False)

    return x
```

```python
import functools
import math

import jax
import jax.numpy as jnp
from jax import lax
from jax.experimental import pallas as pl
from jax.experimental.pallas import tpu as pltpu

F32 = jnp.float32
BF16 = jnp.bfloat16

D_MODEL = 2048
GRID_W = 64
GROUP_W = D_MODEL // 4
POOL_WINDOWS = (2, 4, 8, 16)
HEAD_DIM = 128
N_HEADS = GROUP_W // HEAD_DIM
GLA_GATE_RANK = 16
GLA_GATE_NORM = 16.0
GLA_CHUNK = 64
CONV_K = 31
DIFF_QK_DIM = 64
ROPE_BASE = 10000.0
ROPE_AXIS_DIM = DIFF_QK_DIM // 2
N_EXPERTS = 16
EXPERT_FF = D_MODEL // 2
EC_CAPACITY = 2
NORM_EPS = 1e-6

LANE = 128
HALO = 16
VMEM_LIMIT = 56 * 1024 * 1024
MOD_ROWS = 16

COL_CONV = 0
COL_POOL = 8
COL_GQ = 12
COL_GK = 16
COL_GV = 20
COL_GGATE = 24
COL_DQ = 28
COL_DK = 32
COL_DV = 36
COL_LOW = 40
IN_COLS = 42 * LANE
IN_TN = 7 * LANE

NT_DIMS = (((1,), (1,)), ((), ()))
TN_DIMS = (((0,), (0,)), ((), ()))


def _cparams(n_axes):
    return pltpu.CompilerParams(dimension_semantics=("arbitrary",) * n_axes, vmem_limit_bytes=VMEM_LIMIT)


def _silu(x):
    return x * jax.nn.sigmoid(x)


def _rms(x, g):
    return x * lax.rsqrt(jnp.mean(x * x, axis=-1, keepdims=True) + NORM_EPS) * g


def _layer_spec(l, *tail):
    return pl.BlockSpec((None,) + tail, lambda *_: (l,) + (0,) * len(tail))


def _mod_spec(l, row_of, k):
    return pl.BlockSpec((None, None, None, 1, D_MODEL), lambda *g: (l, row_of(*g), k, 0, 0))


def _ada_kernel(c_ref, w_ref, b_ref, o_ref):
    sc = _silu(c_ref[...]).astype(BF16)
    o_ref[0] = jnp.dot(sc, w_ref[0].astype(BF16), preferred_element_type=F32) + b_ref[0]


def _ada_call(cc, w_ada, b_ada):
    L, D, W = w_ada.shape
    R = cc.shape[0]
    tn = 1536
    return pl.pallas_call(
        _ada_kernel,
        out_shape=jax.ShapeDtypeStruct((L, R, W), F32),
        grid=(L, W // tn),
        in_specs=[
            pl.BlockSpec((R, D), lambda l, j: (0, 0)),
            pl.BlockSpec((1, D, tn), lambda l, j: (l, 0, j)),
            pl.BlockSpec((1, 1, tn), lambda l, j: (l, 0, j)),
        ],
        out_specs=pl.BlockSpec((1, R, tn), lambda l, j: (l, 0, j)),
        compiler_params=_cparams(2),
        name="ada_mod",
    )(cc, w_ada, b_ada.reshape(L, 1, W))


def _inproj_kernel(x_ref, sh_ref, sc_ref, g_ref, w_ref, o_ref, h_ref):
    @pl.when(pl.program_id(2) == 0)
    def _():
        h = _rms(x_ref[0], g_ref[...]) * (1.0 + sc_ref[...]) + sh_ref[...]
        h_ref[...] = h.astype(BF16)

    o_ref[0] = jnp.dot(h_ref[...], w_ref[...], preferred_element_type=F32).astype(BF16)


def _inproj_call(x, mod, row_of, g, w, l):
    Bx, T, D = x.shape
    tm = min(T, 1024)
    return pl.pallas_call(
        _inproj_kernel,
        out_shape=jax.ShapeDtypeStruct((Bx, T, IN_COLS), BF16),
        grid=(Bx, T // tm, IN_COLS // IN_TN),
        in_specs=[
            pl.BlockSpec((1, tm, D), lambda b, i, j: (b, i, 0)),
            _mod_spec(l, row_of, 0),
            _mod_spec(l, row_of, 1),
            _layer_spec(l, 1, D),
            pl.BlockSpec((None, D, IN_TN), lambda b, i, j: (l, 0, j)),
        ],
        out_specs=pl.BlockSpec((1, tm, IN_TN), lambda b, i, j: (b, i, j)),
        scratch_shapes=[pltpu.VMEM((tm, D), BF16)],
        compiler_params=_cparams(3),
        name="in_proj",
    )(x, mod, mod, g, w)


def _halo_specs(T, tt, width, col_block):
    r = tt // HALO
    nh = T // HALO
    return [
        pl.BlockSpec((1, HALO, width), lambda b, c: (b, jnp.maximum(c * r - 1, 0), col_block)),
        pl.BlockSpec((1, tt, width), lambda b, c: (b, c, col_block)),
        pl.BlockSpec((1, HALO, width), lambda b, c: (b, jnp.minimum((c + 1) * r, nh - 1), col_block)),
    ]


def _pool_kernel(prev_ref, cur_ref, next_ref, w_ref, s_ref, o_ref, pad_ref, *, T, tt):
    c = pl.program_id(1)
    H = HALO
    pad_ref[0:H, :] = jnp.where(c > 0, prev_ref[0].astype(F32), 0.0)
    pad_ref[H:H + tt, :] = cur_ref[0].astype(F32)
    pad_ref[H + tt:H + tt + H, :] = jnp.where(c < pl.num_programs(1) - 1, next_ref[0].astype(F32), 0.0)
    t = c * tt + lax.broadcasted_iota(jnp.int32, (tt, HEAD_DIM), 0)
    for gi, w in enumerate(POOL_WINDOWS):
        cols = slice(gi * HEAD_DIM, (gi + 1) * HEAD_DIM)
        acc = pad_ref[H - w // 2:H - w // 2 + tt, cols]
        for d in range(-w // 2 + 1, w // 2):
            acc = acc + pad_ref[H + d:H + d + tt, cols]
        cnt = jnp.minimum(t + w // 2, T) - jnp.maximum(t - w // 2, 0)
        p = acc / cnt.astype(F32) - pad_ref[H:H + tt, cols]
        y = jnp.dot(p.astype(BF16), w_ref[gi], preferred_element_type=F32)
        o_ref[0, :, cols] = (y * s_ref[:, cols]).astype(BF16)


def _pool_call(P, pool_w, pool_scale, l):
    B, T, _ = P.shape
    tt = min(T, 256)
    return pl.pallas_call(
        functools.partial(_pool_kernel, T=T, tt=tt),
        out_shape=jax.ShapeDtypeStruct((B, T, GROUP_W), BF16),
        grid=(B, T // tt),
        in_specs=_halo_specs(T, tt, GROUP_W, COL_POOL * LANE // GROUP_W) + [
            _layer_spec(l, len(POOL_WINDOWS), HEAD_DIM, HEAD_DIM),
            _layer_spec(l, 1, GROUP_W),
        ],
        out_specs=pl.BlockSpec((1, tt, GROUP_W), lambda b, c: (b, c, 0)),
        scratch_shapes=[pltpu.VMEM((tt + 2 * HALO, GROUP_W), F32)],
        compiler_params=_cparams(2),
        name="pool_mixer",
    )(P, P, P, pool_w, pool_scale)


def _log_sigmoid(z):
    return jnp.minimum(z, 0.0) - jnp.log(1.0 + jnp.exp(-jnp.abs(z)))


def _gla_block_consts(rb):
    ri = lax.broadcasted_iota(jnp.int32, (rb, rb), 0)
    ci = lax.broadcasted_iota(jnp.int32, (rb, rb), 1)
    shift = GLA_CHUNK.bit_length() - 1
    same = jnp.right_shift(ri, shift) == jnp.right_shift(ci, shift)
    out = []
    for causal in (ci <= ri, ci >= ri):
        tri = jnp.where(causal, jnp.where(same, 1.0, 0.0), 0.0)
        out.append((tri.astype(BF16), tri > 0.0))
    return out


def _gla_prepare(items, consts, q_ref, k_ref, v_ref, g_ref, qg_ref, u_ref, dec_ref, o_ref):
    C = GLA_CHUNK
    rb = consts[0][1].shape[0]
    nchunk = rb // C
    Gs = []
    for d, rows, _ in items:
        tri = consts[d][0]
        g = g_ref[d, rows, :]
        g_hi = g.astype(BF16)
        g_lo = (g - g_hi.astype(F32)).astype(BF16)
        Gs.append(jnp.dot(tri, g_hi, preferred_element_type=F32) + jnp.dot(tri, g_lo, preferred_element_type=F32))
    staged = []
    for (d, rows, c0), G in zip(items, Gs):
        tot = C - 1 if d == 0 else 0
        tots = [G[ci * C + tot:ci * C + tot + 1, :] for ci in range(nchunk)]
        Gt = jnp.concatenate([jnp.broadcast_to(t, (C, LANE)) for t in tots], axis=0)
        r = 0.5 * Gt
        q = q_ref[0, rows, :].astype(F32) * (HEAD_DIM ** -0.5)
        k = k_ref[0, rows, :].astype(F32)
        qg = (q * jnp.exp(G - r)).astype(BF16)
        kg = (k * jnp.exp(r - G)).astype(BF16)
        qg_ref[d, rows, :] = (q * jnp.exp(G)).astype(BF16)
        kd = (k * jnp.exp(Gt - G)).astype(BF16)
        for ci in range(nchunk):
            dec_ref[d, c0 + ci] = jnp.exp(jnp.broadcast_to(tots[ci], (8, LANE)))
        staged.append((qg, kg, kd))
    atts = [lax.dot_general(qg, kg, NT_DIMS, preferred_element_type=F32) for qg, kg, _ in staged]
    for (d, rows, c0), att, (_, _, kd) in zip(items, atts, staged):
        v = v_ref[0, rows, :]
        att = jnp.where(consts[d][1], att, 0.0).astype(BF16)
        o_ref[d, rows, :] = jnp.dot(att, v, preferred_element_type=F32)
        for ci in range(nchunk):
            cr = slice(ci * C, (ci + 1) * C)
            u_ref[d, c0 + ci] = lax.dot_general(v[cr], kd[cr], TN_DIMS, preferred_element_type=F32)


def _gla_kernel(ql, kl, vl, gtl, lowl, qc, kc, vc, gtc, lowc, upf, bf, upb, bb, ng,
                ol_ref, oc_ref, gl, qgl, ul, decl, sbl, osl, gc, qgc, uc, decc, sbc, osc):
    C = GLA_CHUNK
    T = ql.shape[1]
    Tc = qc.shape[1]

    def gates(low_ref, g_ref):
        low = low_ref[0]
        zf = jnp.dot(low, upf[...], preferred_element_type=F32) + bf[...]
        zb = jnp.dot(low, upb[...], preferred_element_type=F32) + bb[...]
        g_ref[0] = _log_sigmoid(zf) * (1.0 / GLA_GATE_NORM)
        g_ref[1] = _log_sigmoid(zb) * (1.0 / GLA_GATE_NORM)

    def prepare(q_ref, k_ref, v_ref, g_ref, qg_ref, u_ref, dec_ref, o_ref, Tx):
        rb = min(Tx, 256)
        nb = Tx // rb
        per = 2 if nb % 2 == 0 else 1
        consts = _gla_block_consts(rb)

        def body(j, carry):
            items = []
            for p in range(per):
                blk = j * per + p
                rows = pl.ds(pl.multiple_of(blk * rb, rb), rb)
                items += [(d, rows, blk * (rb // C)) for d in (0, 1)]
            _gla_prepare(items, consts, q_ref, k_ref, v_ref, g_ref, qg_ref, u_ref, dec_ref, o_ref)
            return carry

        lax.fori_loop(0, nb // per, body, 0)

    def recur(u_ref, dec_ref, sb_ref, n, Sf, Sb):
        def step(d, c, S):
            sb_ref[d, c] = S.astype(BF16)
            return S * dec_ref[d, c, 0:1, :] + u_ref[d, c]

        def body(i, carry):
            Sf, Sb = carry
            return step(0, i, Sf), step(1, n - 1 - i, Sb)

        return lax.fori_loop(0, n, body, (Sf, Sb), unroll=min(n, 4))

    def inter(qg_ref, sb_ref, o_ref, n):
        grp = min(n, 4)
        per = 2 if (n // grp) % 2 == 0 else 1

        def body(j, carry):
            work = []
            for p in range(per):
                blk = j * per + p
                rows = pl.ds(pl.multiple_of(blk * grp * C, grp * C), grp * C)
                for d in (0, 1):
                    qg = qg_ref[d, rows, :]
                    parts = [lax.dot_general(qg[ci * C:(ci + 1) * C], sb_ref[d, blk * grp + ci], NT_DIMS,
                                             preferred_element_type=F32) for ci in range(grp)]
                    work.append((d, rows, parts))
            for d, rows, parts in work:
                o_ref[d, rows, :] += jnp.concatenate(parts, axis=0)
            return carry

        lax.fori_loop(0, n // (grp * per), body, 0)

    def finish(o_s, gt_ref, o_ref):
        o = _rms(o_s[0] + o_s[1], ng[...])
        o_ref[0] = (o * _silu(gt_ref[0].astype(F32))).astype(BF16)

    gates(lowc, gc)
    gates(lowl, gl)
    prepare(qc, kc, vc, gc, qgc, uc, decc, osc, Tc)
    prepare(ql, kl, vl, gl, qgl, ul, decl, osl, T)
    S0 = jnp.zeros((HEAD_DIM, HEAD_DIM), F32)
    Sf, Sb = recur(uc, decc, sbc, Tc // C, S0, S0)
    recur(ul, decl, sbl, T // C, Sf, Sb)
    inter(qgc, sbc, osc, Tc // C)
    inter(qgl, sbl, osl, T // C)
    finish(osl, gtl, ol_ref)
    finish(osc, gtc, oc_ref)


def _gla_call(Pl, Pc, upf, bias_f, upb, bias_b, norm_g, l):
    B, T, _ = Pl.shape
    Tc = Pc.shape[1]

    def colspec(Tx, col):
        return pl.BlockSpec((1, Tx, LANE), lambda b, h: (b, 0, col + h))

    def lowspec(Tx):
        return pl.BlockSpec((1, Tx, LANE), lambda b, h: (b, 0, COL_LOW))

    headw = pl.BlockSpec((None, LANE, LANE), lambda b, h: (l, 0, h))
    headv = pl.BlockSpec((None, 1, LANE), lambda b, h: (l, 0, h))
    outspec = lambda Tx: pl.BlockSpec((1, Tx, LANE), lambda b, h: (b, 0, h))

    def scratch(Tx):
        n = Tx // GLA_CHUNK
        return [pltpu.VMEM((2, Tx, LANE), F32),
                pltpu.VMEM((2, Tx, LANE), BF16),
                pltpu.VMEM((2, n, HEAD_DIM, HEAD_DIM), F32),
                pltpu.VMEM((2, n, 8, LANE), F32),
                pltpu.VMEM((2, n, HEAD_DIM, HEAD_DIM), BF16),
                pltpu.VMEM((2, Tx, LANE), F32)]

    return pl.pallas_call(
        _gla_kernel,
        out_shape=(jax.ShapeDtypeStruct((B, T, GROUP_W), BF16), jax.ShapeDtypeStruct((B, Tc, GROUP_W), BF16)),
        grid=(B, N_HEADS),
        in_specs=[colspec(T, COL_GQ), colspec(T, COL_GK), colspec(T, COL_GV), colspec(T, COL_GGATE), lowspec(T),
                  colspec(Tc, COL_GQ), colspec(Tc, COL_GK), colspec(Tc, COL_GV), colspec(Tc, COL_GGATE), lowspec(Tc),
                  headw, headv, headw, headv, _layer_spec(l, 1, LANE)],
        out_specs=(outspec(T), outspec(Tc)),
        scratch_shapes=scratch(T) + scratch(Tc),
        compiler_params=_cparams(2),
        name="gla_mixer",
    )(Pl, Pl, Pl, Pl, Pl, Pc, Pc, Pc, Pc, Pc, upf, bias_f, upb, bias_b, norm_g)


def _conv_kernel(prev_ref, cur_ref, next_ref, dw_ref, dwb_ref, lng_ref, lnb_ref, pw_ref, pwb_ref,
                 o_ref, pad_ref, acc_ref, *, tt):
    c = pl.program_id(1)
    H = HALO
    W = GROUP_W

    def glu(u):
        u = u.astype(F32)
        return u[:, :W] * jax.nn.sigmoid(u[:, W:])

    pad_ref[0:H, :] = jnp.where(c > 0, glu(prev_ref[0]), 0.0)
    pad_ref[H:H + tt, :] = glu(cur_ref[0])
    pad_ref[H + tt:H + tt + H, :] = jnp.where(c < pl.num_programs(1) - 1, glu(next_ref[0]), 0.0)

    rs = min(tt, 128)
    off = H - CONV_K // 2
    for r0 in range(0, tt, rs):
        for lb in range(W // LANE):
            cols = slice(lb * LANE, (lb + 1) * LANE)
            acc = jnp.zeros((rs, LANE), F32) + dwb_ref[:, cols]
            for k in range(CONV_K):
                acc = acc + pad_ref[r0 + off + k:r0 + off + k + rs, cols] * dw_ref[k:k + 1, cols]
            acc_ref[r0:r0 + rs, cols] = acc

    h = acc_ref[...]
    mu = jnp.mean(h, axis=-1, keepdims=True)
    hc = h - mu
    var = jnp.mean(hc * hc, axis=-1, keepdims=True)
    y = hc * lax.rsqrt(var + NORM_EPS) * lng_ref[...] + lnb_ref[...]
    y = jnp.dot(_silu(y).astype(BF16), pw_ref[...], preferred_element_type=F32) + pwb_ref[...]
    o_ref[0] = y.astype(BF16)


def _conv_call(P, dw, dw_b, ln_g, ln_b, pw, pw_b, l):
    B, T, _ = P.shape
    tt = min(T, 256)
    vec = _layer_spec(l, 1, GROUP_W)
    return pl.pallas_call(
        functools.partial(_conv_kernel, tt=tt),
        out_shape=jax.ShapeDtypeStruct((B, T, GROUP_W), BF16),
        grid=(B, T // tt),
        in_specs=_halo_specs(T, tt, 2 * GROUP_W, 0) + [
            _layer_spec(l, CONV_K, GROUP_W), vec, vec, vec, _layer_spec(l, GROUP_W, GROUP_W), vec],
        out_specs=pl.BlockSpec((1, tt, GROUP_W), lambda b, c: (b, c, 0)),
        scratch_shapes=[pltpu.VMEM((tt + 2 * HALO, GROUP_W), F32), pltpu.VMEM((tt, GROUP_W), F32)],
        compiler_params=_cparams(2),
        name="conv_mixer",
    )(P, P, P, dw, dw_b, ln_g, ln_b, pw, pw_b)


def _rope(x, cos, sin):
    hw = ROPE_AXIS_DIM // 2
    lane = lax.broadcasted_iota(jnp.int32, x.shape, 1)
    first_half = (lane % (2 * hw)) < hw
    swapped = jnp.where(first_half, pltpu.roll(x, LANE - hw, 1), pltpu.roll(x, hw, 1))
    return x * cos + swapped * sin


def _diff_kernel(*refs, rope, lam_init, n_kv):
    it = iter(refs)
    q_ref = next(it)
    kv = [(next(it), next(it)) for _ in range(n_kv)]
    if rope:
        cq, sq, ck, sk = next(it), next(it), next(it), next(it)
    lq1, lk1, lq2, lk2, sg = next(it), next(it), next(it), next(it), next(it)
    o_ref, kbuf, vbuf = next(it), next(it), next(it)

    @pl.when(pl.program_id(2) == 0)
    def _():
        r0 = 0
        for i, (k_ref, v_ref) in enumerate(kv):
            n = k_ref.shape[1]
            k = k_ref[0]
            if rope and i == 0:
                k = _rope(k.astype(F32), ck[...], sk[...]).astype(BF16)
            kbuf[r0:r0 + n, :] = k
            vbuf[r0:r0 + n, 0:LANE] = v_ref[0]
            r0 += n
        vbuf[:, LANE:2 * LANE] = jnp.ones((vbuf.shape[0], LANE), BF16)

    q = q_ref[0].astype(F32)
    if rope:
        q = _rope(q, cq[...], sq[...])
    q = q * (DIFF_QK_DIM ** -0.5 * math.log2(math.e))
    lane = lax.broadcasted_iota(jnp.int32, q.shape, 1)
    q1 = jnp.where(lane < DIFF_QK_DIM, q, 0.0).astype(BF16)
    q2 = jnp.where(lane >= DIFF_QK_DIM, q, 0.0).astype(BF16)
    k = kbuf[...]
    v1 = vbuf[...]
    s1 = lax.dot_general(q1, k, NT_DIMS, preferred_element_type=F32)
    s2 = lax.dot_general(q2, k, NT_DIMS, preferred_element_type=F32)
    e1 = jnp.exp2(s1 - jnp.max(s1, axis=-1, keepdims=True)).astype(BF16)
    e2 = jnp.exp2(s2 - jnp.max(s2, axis=-1, keepdims=True)).astype(BF16)
    r1 = jnp.dot(e1, v1, preferred_element_type=F32)
    r2 = jnp.dot(e2, v1, preferred_element_type=F32)
    lam = (jnp.exp(jnp.sum(lq1[...] * lk1[...], axis=-1, keepdims=True))
           - jnp.exp(jnp.sum(lq2[...] * lk2[...], axis=-1, keepdims=True)) + lam_init)
    o = r1[:, :LANE] * (1.0 / r1[:, LANE:LANE + 1]) - r2[:, :LANE] * (lam / r2[:, LANE:LANE + 1])
    o_ref[0] = (_rms(o, sg[...]) * (1.0 - lam_init)).astype(BF16)


def _diff_call(Pq, kv_sources, rope_tabs, lq1, lk1, lq2, lk2, subln_g, lam_init, l):
    B, T, _ = Pq.shape
    tq = min(T, 256)
    rope = rope_tabs is not None
    in_specs = [pl.BlockSpec((1, tq, LANE), lambda b, h, i: (b, i, COL_DQ + h))]
    args = [Pq]
    Tk = 0
    for Ps in kv_sources:
        n = Ps.shape[1]
        in_specs.append(pl.BlockSpec((1, n, LANE), lambda b, h, i: (b, 0, COL_DK + h)))
        in_specs.append(pl.BlockSpec((1, n, LANE), lambda b, h, i: (b, 0, COL_DV + h)))
        args += [Ps, Ps]
        Tk += n
    if rope:
        cos, sin = rope_tabs
        in_specs += [pl.BlockSpec((tq, LANE), lambda b, h, i: (i, 0)), pl.BlockSpec((tq, LANE), lambda b, h, i: (i, 0)),
                     pl.BlockSpec((T, LANE), lambda b, h, i: (0, 0)), pl.BlockSpec((T, LANE), lambda b, h, i: (0, 0))]
        args += [cos, sin, cos, sin]
    small = _layer_spec(l, 1, DIFF_QK_DIM)
    in_specs += [small, small, small, small, _layer_spec(l, 1, LANE)]
    args += [lq1, lk1, lq2, lk2, subln_g]
    return pl.pallas_call(
        functools.partial(_diff_kernel, rope=rope, lam_init=lam_init, n_kv=len(kv_sources)),
        out_shape=jax.ShapeDtypeStruct((B, T, GROUP_W), BF16),
        grid=(B, N_HEADS, T // tq),
        in_specs=in_specs,
        out_specs=pl.BlockSpec((1, tq, LANE), lambda b, h, i: (b, i, h)),
        scratch_shapes=[pltpu.VMEM((Tk, LANE), BF16), pltpu.VMEM((Tk, 2 * LANE), BF16)],
        compiler_params=_cparams(3),
        name="diff_attn",
    )(*args)


def _outproj_kernel(a_ref, b_ref, c_ref, d_ref, w_ref, x_ref, gate_ref, sh_ref, sc_ref, ng_ref, wrh_ref, wrl_ref,
                    x1_ref, h2_ref, aff_ref):
    W = GROUP_W
    y = jnp.dot(a_ref[0], w_ref[0:W, :], preferred_element_type=F32)
    y = y + jnp.dot(b_ref[0], w_ref[W:2 * W, :], preferred_element_type=F32)
    y = y + jnp.dot(c_ref[0], w_ref[2 * W:3 * W, :], preferred_element_type=F32)
    y = y + jnp.dot(d_ref[0], w_ref[3 * W:4 * W, :], preferred_element_type=F32)
    x1 = x_ref[0] + gate_ref[...] * y
    x1_ref[0] = x1
    h = _rms(x1, ng_ref[...]) * (1.0 + sc_ref[...]) + sh_ref[...]
    hh = h.astype(BF16)
    hl = (h - hh.astype(F32)).astype(BF16)
    h2_ref[0] = hh
    lg = (jnp.dot(hh, wrh_ref[...], preferred_element_type=F32)
          + jnp.dot(hl, wrh_ref[...], preferred_element_type=F32)
          + jnp.dot(hh, wrl_ref[...], preferred_element_type=F32))
    lane = lax.broadcasted_iota(jnp.int32, lg.shape, 1)
    lg = jnp.where(lane < N_EXPERTS, lg, -jnp.inf)
    e = jnp.exp(lg - jnp.max(lg, axis=-1, keepdims=True))
    aff_ref[0] = e / jnp.sum(e, axis=-1, keepdims=True)


def _outproj_call(mix, w_out, x, mod, row_of, ng, wr_hi, wr_lo, l):
    Bx, T, D = x.shape
    tm = min(T, 256)
    mixspec = pl.BlockSpec((1, tm, GROUP_W), lambda b, i: (b, i, 0))
    rowspec = pl.BlockSpec((1, tm, D), lambda b, i: (b, i, 0))
    return pl.pallas_call(
        _outproj_kernel,
        out_shape=(jax.ShapeDtypeStruct((Bx, T, D), F32), jax.ShapeDtypeStruct((Bx, T, D), BF16),
                   jax.ShapeDtypeStruct((Bx, T, LANE), F32)),
        grid=(Bx, T // tm),
        in_specs=[mixspec, mixspec, mixspec, mixspec,
                  _layer_spec(l, D, D),
                  rowspec, _mod_spec(l, row_of, 2), _mod_spec(l, row_of, 3), _mod_spec(l, row_of, 4),
                  _layer_spec(l, 1, D), _layer_spec(l, D, LANE), _layer_spec(l, D, LANE)],
        out_specs=(rowspec, rowspec, pl.BlockSpec((1, tm, LANE), lambda b, i: (b, i, 0))),
        compiler_params=_cparams(2),
        name="out_proj",
    )(*mix, w_out, x, mod, mod, mod, ng, wr_hi, wr_lo)


def _excl_prefix(x):
    rows, T = x.shape
    ri = lax.broadcasted_iota(jnp.int32, (LANE, LANE), 0)
    ci = lax.broadcasted_iota(jnp.int32, (LANE, LANE), 1)
    upper = jnp.where(ri <= ci, 1.0, 0.0).astype(BF16)
    carry = jnp.zeros((rows, 1), F32)
    out = []
    for b in range(T // LANE):
        xb = x[:, b * LANE:(b + 1) * LANE]
        inc = jnp.dot(xb.astype(BF16), upper, preferred_element_type=F32)
        out.append(inc - xb + carry)
        carry = carry + jnp.sum(xb, axis=1, keepdims=True)
    return jnp.concatenate(out, axis=1)


def _route_kernel(aff_ref, slot_row_ref, slot_col_ref, aff_row_ref, st_ref, *, cap):
    T = aff_ref.shape[1]
    E = N_EXPERTS
    arow = aff_ref[0].T[0:E, :]
    keys = lax.bitcast_convert_type(arow, jnp.int32)
    v = jnp.zeros((E, 1), jnp.int32)
    for bit in range(30, -1, -1):
        cand = v | (1 << bit)
        cnt = jnp.sum(jnp.where(keys >= cand, 1.0, 0.0), axis=1, keepdims=True)
        v = jnp.where(cnt >= cap, cand, v)
    above = keys > v
    tied = jnp.where(keys == v, 1.0, 0.0)
    room = cap - jnp.sum(jnp.where(above, 1.0, 0.0), axis=1, keepdims=True)
    kept = jnp.where(above, 1.0, jnp.where(_excl_prefix(tied) < room, tied, 0.0))
    slot = jnp.where(kept > 0.0, _excl_prefix(kept), float(T))
    for e in range(E):
        slot_row_ref[0, e] = slot[e:e + 1, :].astype(jnp.int32)
        aff_row_ref[0, e] = arow[e:e + 1, :]
    st_ref[...] = jnp.full(st_ref.shape, float(T), F32)
    st_ref[0:E, :] = slot
    slot_col_ref[0] = st_ref[...].T.astype(jnp.int32)


def _route_call(aff, cap):
    B, T, _ = aff.shape
    E = N_EXPERTS
    return pl.pallas_call(
        functools.partial(_route_kernel, cap=cap),
        out_shape=(jax.ShapeDtypeStruct((B, E, 1, T), jnp.int32),
                   jax.ShapeDtypeStruct((B, T, LANE), jnp.int32),
                   jax.ShapeDtypeStruct((B, E, 1, T), F32)),
        grid=(B,),
        in_specs=[pl.BlockSpec((1, T, LANE), lambda b: (b, 0, 0))],
        out_specs=(pl.BlockSpec((1, E, 1, T), lambda b: (b, 0, 0, 0)),
                   pl.BlockSpec((1, T, LANE), lambda b: (b, 0, 0)),
                   pl.BlockSpec((1, E, 1, T), lambda b: (b, 0, 0, 0))),
        scratch_shapes=[pltpu.VMEM((LANE, T), F32)],
        compiler_params=_cparams(1),
        name="router_route",
    )(aff)


def _dispatch_kernel(slot_ref, affr_ref, h_ref, xe_ref, gs_ref):
    cap = xe_ref.shape[1]
    T = h_ref.shape[1]
    slot = lax.broadcasted_iota(jnp.int32, (cap, T), 0)
    hit = slot_ref[0, 0] == slot
    xe_ref[0] = jnp.dot(jnp.where(hit, 1.0, 0.0).astype(BF16), h_ref[0],
                        preferred_element_type=F32).astype(BF16)
    g = jnp.sum(jnp.where(hit, affr_ref[0, 0], 0.0), axis=-1, keepdims=True)
    gs_ref[0] = jnp.broadcast_to(g, (cap, LANE))


def _dispatch_call(slot_row, aff_row, h2, cap):
    B, T, D = h2.shape
    E = N_EXPERTS
    return pl.pallas_call(
        _dispatch_kernel,
        out_shape=(jax.ShapeDtypeStruct((E, B * cap, D), BF16), jax.ShapeDtypeStruct((E, B * cap, LANE), F32)),
        grid=(B, E),
        in_specs=[pl.BlockSpec((1, 1, 1, T), lambda b, e: (b, e, 0, 0)),
                  pl.BlockSpec((1, 1, 1, T), lambda b, e: (b, e, 0, 0)),
                  pl.BlockSpec((1, T, D), lambda b, e: (b, 0, 0))],
        out_specs=(pl.BlockSpec((1, cap, D), lambda b, e: (e, b, 0)),
                   pl.BlockSpec((1, cap, LANE), lambda b, e: (e, b, 0))),
        compiler_params=_cparams(2),
        name="moe_dispatch",
    )(slot_row, aff_row, h2)


def _expert_kernel(*refs, n_groups, nf):
    xs = refs[:n_groups]
    wg_ref, wu_ref, wd_ref = refs[n_groups:n_groups + 3]
    gss = refs[n_groups + 3:2 * n_groups + 3]
    ys = refs[2 * n_groups + 3:3 * n_groups + 3]
    hms = refs[3 * n_groups + 3:]
    s = pl.program_id(1)
    tf = wg_ref.shape[-1]

    @pl.when(s < nf)
    def _():
        wg = wg_ref[...].astype(BF16)
        wu = wu_ref[...].astype(BF16)
        for x_ref, hm_ref in zip(xs, hms):
            x = x_ref[...]
            a = jnp.dot(x, wg, preferred_element_type=F32)
            u = jnp.dot(x, wu, preferred_element_type=F32)
            hm_ref[s] = (_silu(a) * u).astype(BF16)

    @pl.when(s >= nf)
    def _():
        for hm_ref, gs_ref, y_ref in zip(hms, gss, ys):
            acc = jnp.dot(hm_ref[0], wd_ref[0:tf, :].astype(BF16), preferred_element_type=F32)
            for f in range(1, nf):
                acc = acc + jnp.dot(hm_ref[f], wd_ref[f * tf:(f + 1) * tf, :].astype(BF16),
                                    preferred_element_type=F32)
            y_ref[...] = (acc * gs_ref[...][:, 0:1]).astype(BF16)


def _expert_call(groups, w_gate, w_up, w_down, l):
    E, _, D = groups[0][0].shape
    F = w_gate.shape[-1]
    tf = 256
    tn = 512
    nf = F // tf
    nn = D // tn
    n = len(groups)
    Ms = [g[0].shape[1] for g in groups]
    fidx = lambda s: jnp.minimum(s, nf - 1)
    nidx = lambda s: jnp.maximum(s - nf, 0)
    in_specs = ([pl.BlockSpec((None, M, D), lambda e, s: (e, 0, 0)) for M in Ms]
                + [pl.BlockSpec((None, None, D, tf), lambda e, s: (l, e, 0, fidx(s))),
                   pl.BlockSpec((None, None, D, tf), lambda e, s: (l, e, 0, fidx(s))),
                   pl.BlockSpec((None, None, F, tn), lambda e, s: (l, e, 0, nidx(s)))]
                + [pl.BlockSpec((None, M, LANE), lambda e, s: (e, 0, 0)) for M in Ms])
    return pl.pallas_call(
        functools.partial(_expert_kernel, n_groups=n, nf=nf),
        out_shape=[jax.ShapeDtypeStruct((E, M, D), BF16) for M in Ms],
        grid=(E, nf + nn),
        in_specs=in_specs,
        out_specs=[pl.BlockSpec((None, M, tn), lambda e, s: (e, 0, nidx(s))) for M in Ms],
        scratch_shapes=[pltpu.VMEM((nf, M, tf), BF16) for M in Ms],
        compiler_params=_cparams(2),
        name="moe_experts",
    )(*[g[0] for g in groups], w_gate, w_up, w_down, *[g[1] for g in groups])


def _combine_kernel(slot_ref, y_ref, x_ref, gate_ref, fg_ref, o_ref, *, final_norm):
    cap = y_ref.shape[1]
    tm = x_ref.shape[1]
    lane_slot = lax.broadcasted_iota(jnp.int32, (tm, cap), 1)
    slot = slot_ref[0]
    acc = jnp.zeros(x_ref.shape[1:], F32)
    for e in range(N_EXPERTS):
        hit = jnp.where(slot[:, e:e + 1] == lane_slot, 1.0, 0.0).astype(BF16)
        acc = acc + jnp.dot(hit, y_ref[e], preferred_element_type=F32)
    x2 = x_ref[0] + gate_ref[...] * acc
    if final_norm:
        x2 = _rms(x2, fg_ref[...])
    o_ref[0] = x2


def _combine_call(slot_col, y, x1, mod, row_of, final_g, cap, final_norm, l):
    B, T, D = x1.shape
    tm = min(T, 512)
    return pl.pallas_call(
        functools.partial(_combine_kernel, final_norm=final_norm),
        out_shape=jax.ShapeDtypeStruct((B, T, D), F32),
        grid=(B, T // tm),
        in_specs=[pl.BlockSpec((1, tm, LANE), lambda b, i: (b, i, 0)),
                  pl.BlockSpec((N_EXPERTS, cap, D), lambda b, i: (0, b, 0)),
                  pl.BlockSpec((1, tm, D), lambda b, i: (b, i, 0)),
                  _mod_spec(l, row_of, 5),
                  pl.BlockSpec((1, D), lambda b, i: (0, 0))],
        out_specs=pl.BlockSpec((1, tm, D), lambda b, i: (b, i, 0)),
        compiler_params=_cparams(2),
        name="moe_combine",
    )(slot_col, y, x1, mod, final_g)


def _rope_tables(n):
    rows = n // GRID_W
    row = jnp.repeat(jnp.arange(rows), GRID_W).astype(F32)
    col = jnp.tile(jnp.arange(GRID_W), rows).astype(F32)
    inv_freq = ROPE_BASE ** (-jnp.arange(0, ROPE_AXIS_DIM, 2, dtype=F32) / ROPE_AXIS_DIM)
    ar = row[:, None] * inv_freq
    ac = col[:, None] * inv_freq
    cos = jnp.concatenate([jnp.cos(ar), jnp.cos(ar), jnp.cos(ac), jnp.cos(ac)], axis=-1)
    sin = jnp.concatenate([-jnp.sin(ar), jnp.sin(ar), -jnp.sin(ac), jnp.sin(ac)], axis=-1)
    return jnp.tile(cos, (1, 2)), jnp.tile(sin, (1, 2))


def _reorder_w_in(w_in):
    G = GROUP_W
    o = 0
    parts = {}
    for name, size in (("pool", G), ("gq", G), ("gk", G), ("gv", G), ("gg", G), ("low", 2 * GLA_GATE_RANK),
                       ("conv", 2 * G), ("dq", G), ("dk", G), ("dv", G)):
        parts[name] = w_in[..., o:o + size]
        o += size
    pad = jnp.zeros(w_in.shape[:-1] + (IN_COLS - COL_LOW * LANE - 2 * GLA_GATE_RANK,), w_in.dtype)
    order = ["conv", "pool", "gq", "gk", "gv", "gg", "dq", "dk", "dv", "low"]
    return jnp.concatenate([parts[n] for n in order] + [pad], axis=-1).astype(BF16)


def kernel(x, c, ctx, c_ctx, w_ada, b_ada, norm1_g, norm2_g, w_in, pool_w, pool_scale, gla_gk_up_f, gla_gk_bias_f, gla_gk_up_b, gla_gk_bias_b, gla_norm_g, conv_dw, conv_dw_b, conv_ln_g, conv_ln_b, conv_pw, conv_pw_b, diff_lq1, diff_lk1, diff_lq2, diff_lk2, diff_subln_g, w_out, w_router, w_exp_gate, w_exp_up, w_exp_down, final_norm_g):
    B, N, D = x.shape
    Tc = ctx.shape[1]
    L = w_ada.shape[0]
    assert D == D_MODEL and N % 256 == 0 and Tc % 128 == 0 and B < MOD_ROWS

    cc = jnp.concatenate([c, c_ctx[None, :], jnp.zeros((MOD_ROWS - B - 1, D), F32)], axis=0)
    mod = _ada_call(cc, w_ada, b_ada).reshape(L, MOD_ROWS, 6, 1, D)
    lat_row = lambda b, *_: b
    ctx_row = lambda *_: B

    rows3 = lambda a: a.reshape(L, 1, -1)
    w_in_r = _reorder_w_in(w_in)
    w_out_b = w_out.astype(BF16)
    pool_w_b = pool_w.astype(BF16)
    conv_pw_b16 = conv_pw.astype(BF16)
    R = GLA_GATE_RANK
    zpad = lambda a, lo: jnp.pad(a, ((0, 0), (lo, LANE - R - lo), (0, 0))).astype(BF16)
    upf = zpad(gla_gk_up_f, 0)
    upb = zpad(gla_gk_up_b, R)
    wr = jnp.pad(w_router, ((0, 0), (0, 0), (0, LANE - N_EXPERTS)))
    wr_hi = wr.astype(BF16)
    wr_lo = (wr - wr_hi.astype(F32)).astype(BF16)
    rope_tabs = _rope_tables(N)
    fg = final_norm_g.reshape(1, D)
    n1, n2 = rows3(norm1_g), rows3(norm2_g)
    gla_args = (upf, rows3(gla_gk_bias_f), upb, rows3(gla_gk_bias_b), rows3(gla_norm_g))
    conv_args = (conv_dw, rows3(conv_dw_b), rows3(conv_ln_g), rows3(conv_ln_b), conv_pw_b16, rows3(conv_pw_b))
    diff_vecs = (rows3(diff_lq1), rows3(diff_lk1), rows3(diff_lq2), rows3(diff_lk2), rows3(diff_subln_g))
    pool_args = (pool_w_b, rows3(pool_scale))
    cap_l = EC_CAPACITY * N // N_EXPERTS
    cap_c = EC_CAPACITY * Tc // N_EXPERTS
    flat = lambda a: a.reshape(1, B * Tc, a.shape[-1])
    unflat = lambda a: a.reshape(B, Tc, a.shape[-1])

    for l in range(L):
        last = l == L - 1
        lam_init = 0.8 - 0.6 * math.exp(-0.3 * l)

        Pl = _inproj_call(x, mod, lat_row, n1, w_in_r, l)
        Pc = unflat(_inproj_call(flat(ctx), mod, ctx_row, n1, w_in_r, l))

        gla_l, gla_c = _gla_call(Pl, Pc, *gla_args, l)
        pool_l = _pool_call(Pl, *pool_args, l)
        conv_l = _conv_call(Pl, *conv_args, l)
        diff_l = _diff_call(Pl, [Pl, Pc], rope_tabs, *diff_vecs, lam_init, l)
        x1, h2, aff = _outproj_call((pool_l, gla_l, conv_l, diff_l), w_out_b, x, mod, lat_row, n2, wr_hi, wr_lo, l)
        slot_row, slot_col, aff_row = _route_call(aff, cap_l)
        groups = [_dispatch_call(slot_row, aff_row, h2, cap_l)]

        if not last:
            pool_c = _pool_call(Pc, *pool_args, l)
            conv_c = _conv_call(Pc, *conv_args, l)
            diff_c = _diff_call(Pc, [Pc], None, *diff_vecs, lam_init, l)
            c1, h2c, affc = _outproj_call((flat(pool_c), flat(gla_c), flat(conv_c), flat(diff_c)), w_out_b,
                                          flat(ctx), mod, ctx_row, n2, wr_hi, wr_lo, l)
            slot_row_c, slot_col_c, aff_row_c = _route_call(unflat(affc), cap_c)
            groups.append(_dispatch_call(slot_row_c, aff_row_c, unflat(h2c), cap_c))

        ys = _expert_call(groups, w_exp_gate, w_exp_up, w_exp_down, l)
        x = _combine_call(slot_col, ys[0], x1, mod, lat_row, fg, cap_l, last, l)
        if not last:
            ctx = _combine_call(slot_col_c, ys[1], unflat(c1), mod, ctx_row, fg, cap_c, False, l)

    return x
```

```python
import functools
import math

import jax
import jax.numpy as jnp
from jax import lax
from jax.experimental import pallas as pl
from jax.experimental.pallas import tpu as pltpu

F32 = jnp.float32
BF16 = jnp.bfloat16

D_MODEL = 2048
GRID_W = 64
GROUP_W = D_MODEL // 4
POOL_WINDOWS = (2, 4, 8, 16)
HEAD_DIM = 128
N_HEADS = GROUP_W // HEAD_DIM
GLA_GATE_RANK = 16
GLA_GATE_NORM = 16.0
GLA_CHUNK = 64
CONV_K = 31
DIFF_QK_DIM = 64
ROPE_BASE = 10000.0
ROPE_AXIS_DIM = DIFF_QK_DIM // 2
N_EXPERTS = 16
EXPERT_FF = D_MODEL // 2
EC_CAPACITY = 2
NORM_EPS = 1e-6

LANE = 128
SUBLANE = 8
HALO = 16
VMEM_LIMIT = 56 * 1024 * 1024
MOD_ROWS = 16

COL_CONV = 0
COL_POOL = 8
COL_GQ = 12
COL_GK = 16
COL_GV = 20
COL_GGATE = 24
COL_DQ = 28
COL_DK = 32
COL_DV = 36
COL_LOW = 40
MXU_N = 256
IN_COLS = 42 * LANE
IN_TN = 7 * MXU_N

NT_DIMS = (((1,), (1,)), ((), ()))
TN_DIMS = (((0,), (0,)), ((), ()))


def _cparams(n_axes):
    return pltpu.CompilerParams(dimension_semantics=("arbitrary",) * n_axes, vmem_limit_bytes=VMEM_LIMIT)


def _silu(x):
    return x * jax.nn.sigmoid(x)


def _rms(x, g):
    return x * lax.rsqrt(jnp.mean(x * x, axis=-1, keepdims=True) + NORM_EPS) * g


def _norm_mod(x, g, scale, shift):
    inv = lax.rsqrt(jnp.mean(x * x, axis=-1, keepdims=True) + NORM_EPS)
    return x * inv * (g * (1.0 + scale)) + shift


def _layer_spec(l, *tail):
    return pl.BlockSpec((None,) + tail, lambda *_: (l,) + (0,) * len(tail))


def _mod_spec(l, row_of, k):
    return pl.BlockSpec((None, None, None, 1, D_MODEL), lambda *g: (l, row_of(*g), k, 0, 0))


def _ada_kernel(c_ref, w_ref, b_ref, o_ref):
    sc = _silu(c_ref[...]).astype(BF16)
    o_ref[0] = jnp.dot(sc, w_ref[0].astype(BF16), preferred_element_type=F32) + b_ref[0]


def _ada_call(cc, w_ada, b_ada):
    L, D, W = w_ada.shape
    R = cc.shape[0]
    tn = 1536
    return pl.pallas_call(
        _ada_kernel,
        out_shape=jax.ShapeDtypeStruct((L, R, W), F32),
        grid=(L, W // tn),
        in_specs=[
            pl.BlockSpec((R, D), lambda l, j: (0, 0)),
            pl.BlockSpec((1, D, tn), lambda l, j: (l, 0, j)),
            pl.BlockSpec((1, 1, tn), lambda l, j: (l, 0, j)),
        ],
        out_specs=pl.BlockSpec((1, R, tn), lambda l, j: (l, 0, j)),
        compiler_params=_cparams(2),
        name="ada_mod",
    )(cc, w_ada, b_ada.reshape(L, 1, W))


def _inproj_kernel(x_ref, sh_ref, sc_ref, g_ref, w_ref, o_ref, h_ref):
    @pl.when(pl.program_id(2) == 0)
    def _():
        h_ref[...] = _norm_mod(x_ref[0], g_ref[...], sc_ref[...], sh_ref[...]).astype(BF16)

    o_ref[0] = jnp.dot(h_ref[...], w_ref[...], preferred_element_type=F32).astype(BF16)


def _inproj_call(x, mod, row_of, g, w, l):
    Bx, T, D = x.shape
    tm = min(T, 1024)
    return pl.pallas_call(
        _inproj_kernel,
        out_shape=jax.ShapeDtypeStruct((Bx, T, IN_COLS), BF16),
        grid=(Bx, T // tm, IN_COLS // IN_TN),
        in_specs=[
            pl.BlockSpec((1, tm, D), lambda b, i, j: (b, i, 0)),
            _mod_spec(l, row_of, 0),
            _mod_spec(l, row_of, 1),
            _layer_spec(l, 1, D),
            pl.BlockSpec((None, D, IN_TN), lambda b, i, j: (l, 0, j)),
        ],
        out_specs=pl.BlockSpec((1, tm, IN_TN), lambda b, i, j: (b, i, j)),
        scratch_shapes=[pltpu.VMEM((tm, D), BF16)],
        compiler_params=_cparams(3),
        name="in_proj",
    )(x, mod, mod, g, w)


def _halo_specs(T, tt, width, col_block):
    r = tt // HALO
    nh = T // HALO
    return [
        pl.BlockSpec((1, HALO, width), lambda b, c: (b, jnp.maximum(c * r - 1, 0), col_block)),
        pl.BlockSpec((1, tt, width), lambda b, c: (b, c, col_block)),
        pl.BlockSpec((1, HALO, width), lambda b, c: (b, jnp.minimum((c + 1) * r, nh - 1), col_block)),
    ]


def _pool_kernel(prev_ref, cur_ref, next_ref, w_ref, s_ref, o_ref, pad_ref, *, T, tt):
    c = pl.program_id(1)
    H = HALO
    pad_ref[0:H, :] = jnp.where(c > 0, prev_ref[0].astype(F32), 0.0)
    pad_ref[H:H + tt, :] = cur_ref[0].astype(F32)
    pad_ref[H + tt:H + tt + H, :] = jnp.where(c < pl.num_programs(1) - 1, next_ref[0].astype(F32), 0.0)
    t = c * tt + lax.broadcasted_iota(jnp.int32, (tt, HEAD_DIM), 0)
    for gi, w in enumerate(POOL_WINDOWS):
        cols = slice(gi * HEAD_DIM, (gi + 1) * HEAD_DIM)
        acc = pad_ref[H - w // 2:H - w // 2 + tt, cols]
        for d in range(-w // 2 + 1, w // 2):
            acc = acc + pad_ref[H + d:H + d + tt, cols]
        cnt = jnp.minimum(t + w // 2, T) - jnp.maximum(t - w // 2, 0)
        p = acc / cnt.astype(F32) - pad_ref[H:H + tt, cols]
        y = jnp.dot(p.astype(BF16), w_ref[gi], preferred_element_type=F32)
        o_ref[0, :, cols] = (y * s_ref[:, cols]).astype(BF16)


def _pool_call(P, pool_w, pool_scale, l):
    B, T, _ = P.shape
    tt = min(T, 256)
    return pl.pallas_call(
        functools.partial(_pool_kernel, T=T, tt=tt),
        out_shape=jax.ShapeDtypeStruct((B, T, GROUP_W), BF16),
        grid=(B, T // tt),
        in_specs=_halo_specs(T, tt, GROUP_W, COL_POOL * LANE // GROUP_W) + [
            _layer_spec(l, len(POOL_WINDOWS), HEAD_DIM, HEAD_DIM),
            _layer_spec(l, 1, GROUP_W),
        ],
        out_specs=pl.BlockSpec((1, tt, GROUP_W), lambda b, c: (b, c, 0)),
        scratch_shapes=[pltpu.VMEM((tt + 2 * HALO, GROUP_W), F32)],
        compiler_params=_cparams(2),
        name="pool_mixer",
    )(P, P, P, pool_w, pool_scale)


def _log_sigmoid(z):
    return jnp.minimum(z, 0.0) - jnp.log(1.0 + jnp.exp(-jnp.abs(z)))


def _gla_block_consts(rb):
    ri = lax.broadcasted_iota(jnp.int32, (rb, rb), 0)
    ci = lax.broadcasted_iota(jnp.int32, (rb, rb), 1)
    shift = GLA_CHUNK.bit_length() - 1
    same = jnp.right_shift(ri, shift) == jnp.right_shift(ci, shift)
    out = []
    for causal in (ci <= ri, ci >= ri):
        tri = jnp.where(causal, jnp.where(same, 1.0, 0.0), 0.0)
        out.append((tri.astype(BF16), tri > 0.0))
    return out


def _gla_prepare(items, consts, q_ref, k_ref, v_ref, g_ref, qg_ref, u_ref, dec_ref, o_ref):
    C = GLA_CHUNK
    rb = consts[0][1].shape[0]
    nchunk = rb // C
    Gs = []
    for d, rows, _ in items:
        tri = consts[d][0]
        g = g_ref[d, rows, :]
        g_hi = g.astype(BF16)
        g_lo = (g - g_hi.astype(F32)).astype(BF16)
        Gs.append(jnp.dot(tri, g_hi, preferred_element_type=F32) + jnp.dot(tri, g_lo, preferred_element_type=F32))
    staged = []
    for (d, rows, c0), G in zip(items, Gs):
        tot = C - 1 if d == 0 else 0
        tots = [G[ci * C + tot:ci * C + tot + 1, :] for ci in range(nchunk)]
        Gt = jnp.concatenate([jnp.broadcast_to(t, (C, LANE)) for t in tots], axis=0)
        r = 0.5 * Gt
        q = q_ref[0, rows, :].astype(F32) * (HEAD_DIM ** -0.5)
        k = k_ref[0, rows, :].astype(F32)
        qg = (q * jnp.exp(G - r)).astype(BF16)
        kg = (k * jnp.exp(r - G)).astype(BF16)
        qg_ref[d, rows, :] = (q * jnp.exp(G)).astype(BF16)
        kd = (k * jnp.exp(Gt - G)).astype(BF16)
        for ci in range(nchunk):
            dec_ref[d, c0 + ci] = jnp.exp(jnp.broadcast_to(tots[ci], (8, LANE)))
        staged.append((qg, kg, kd))
    atts = [lax.dot_general(qg, kg, NT_DIMS, preferred_element_type=F32) for qg, kg, _ in staged]
    for (d, rows, c0), att, (_, _, kd) in zip(items, atts, staged):
        v = v_ref[0, rows, :]
        att = jnp.where(consts[d][1], att, 0.0).astype(BF16)
        o_ref[d, rows, :] = jnp.dot(att, v, preferred_element_type=F32)
        for ci in range(nchunk):
            cr = slice(ci * C, (ci + 1) * C)
            u_ref[d, c0 + ci] = lax.dot_general(v[cr], kd[cr], TN_DIMS, preferred_element_type=F32)


def _gla_kernel(ql, kl, vl, gtl, lowl, qc, kc, vc, gtc, lowc, upf, bf, upb, bb, ng,
                ol_ref, oc_ref, gl, qgl, ul, decl, sbl, osl, gc, qgc, uc, decc, sbc, osc):
    C = GLA_CHUNK
    T = ql.shape[1]
    Tc = qc.shape[1]

    def gates(low_ref, g_ref):
        low = low_ref[0]
        zf = jnp.dot(low, upf[...], preferred_element_type=F32) + bf[...]
        zb = jnp.dot(low, upb[...], preferred_element_type=F32) + bb[...]
        g_ref[0] = _log_sigmoid(zf) * (1.0 / GLA_GATE_NORM)
        g_ref[1] = _log_sigmoid(zb) * (1.0 / GLA_GATE_NORM)

    def prepare(q_ref, k_ref, v_ref, g_ref, qg_ref, u_ref, dec_ref, o_ref, Tx):
        rb = min(Tx, 256)
        nb = Tx // rb
        per = 2 if nb % 2 == 0 else 1
        consts = _gla_block_consts(rb)

        def body(j, carry):
            items = []
            for p in range(per):
                blk = j * per + p
                rows = pl.ds(pl.multiple_of(blk * rb, rb), rb)
                items += [(d, rows, blk * (rb // C)) for d in (0, 1)]
            _gla_prepare(items, consts, q_ref, k_ref, v_ref, g_ref, qg_ref, u_ref, dec_ref, o_ref)
            return carry

        lax.fori_loop(0, nb // per, body, 0)

    def recur(u_ref, dec_ref, sb_ref, n, Sf, Sb):
        def step(d, c, S):
            sb_ref[d, c] = S.astype(BF16)
            return S * dec_ref[d, c, 0:1, :] + u_ref[d, c]

        def body(i, carry):
            Sf, Sb = carry
            return step(0, i, Sf), step(1, n - 1 - i, Sb)

        return lax.fori_loop(0, n, body, (Sf, Sb), unroll=min(n, 4))

    def inter(qg_ref, sb_ref, o_ref, n):
        grp = min(n, 4)
        per = 2 if (n // grp) % 2 == 0 else 1

        def body(j, carry):
            work = []
            for p in range(per):
                blk = j * per + p
                rows = pl.ds(pl.multiple_of(blk * grp * C, grp * C), grp * C)
                for d in (0, 1):
                    qg = qg_ref[d, rows, :]
                    parts = [lax.dot_general(qg[ci * C:(ci + 1) * C], sb_ref[d, blk * grp + ci], NT_DIMS,
                                             preferred_element_type=F32) for ci in range(grp)]
                    work.append((d, rows, parts))
            for d, rows, parts in work:
                o_ref[d, rows, :] += jnp.concatenate(parts, axis=0)
            return carry

        lax.fori_loop(0, n // (grp * per), body, 0)

    def finish(o_s, gt_ref, o_ref):
        o = _rms(o_s[0] + o_s[1], ng[...])
        o_ref[0] = (o * _silu(gt_ref[0].astype(F32))).astype(BF16)

    gates(lowc, gc)
    gates(lowl, gl)
    prepare(qc, kc, vc, gc, qgc, uc, decc, osc, Tc)
    prepare(ql, kl, vl, gl, qgl, ul, decl, osl, T)
    S0 = jnp.zeros((HEAD_DIM, HEAD_DIM), F32)
    Sf, Sb = recur(uc, decc, sbc, Tc // C, S0, S0)
    recur(ul, decl, sbl, T // C, Sf, Sb)
    inter(qgc, sbc, osc, Tc // C)
    inter(qgl, sbl, osl, T // C)
    finish(osl, gtl, ol_ref)
    finish(osc, gtc, oc_ref)


def _gla_call(Pl, Pc, upf, bias_f, upb, bias_b, norm_g, l):
    B, T, _ = Pl.shape
    Tc = Pc.shape[1]

    def colspec(Tx, col):
        return pl.BlockSpec((1, Tx, LANE), lambda b, h: (b, 0, col + h))

    def lowspec(Tx):
        return pl.BlockSpec((1, Tx, LANE), lambda b, h: (b, 0, COL_LOW))

    headw = pl.BlockSpec((None, LANE, LANE), lambda b, h: (l, 0, h))
    headv = pl.BlockSpec((None, 1, LANE), lambda b, h: (l, 0, h))
    outspec = lambda Tx: pl.BlockSpec((1, Tx, LANE), lambda b, h: (b, 0, h))

    def scratch(Tx):
        n = Tx // GLA_CHUNK
        return [pltpu.VMEM((2, Tx, LANE), F32),
                pltpu.VMEM((2, Tx, LANE), BF16),
                pltpu.VMEM((2, n, HEAD_DIM, HEAD_DIM), F32),
                pltpu.VMEM((2, n, 8, LANE), F32),
                pltpu.VMEM((2, n, HEAD_DIM, HEAD_DIM), BF16),
                pltpu.VMEM((2, Tx, LANE), F32)]

    return pl.pallas_call(
        _gla_kernel,
        out_shape=(jax.ShapeDtypeStruct((B, T, GROUP_W), BF16), jax.ShapeDtypeStruct((B, Tc, GROUP_W), BF16)),
        grid=(B, N_HEADS),
        in_specs=[colspec(T, COL_GQ), colspec(T, COL_GK), colspec(T, COL_GV), colspec(T, COL_GGATE), lowspec(T),
                  colspec(Tc, COL_GQ), colspec(Tc, COL_GK), colspec(Tc, COL_GV), colspec(Tc, COL_GGATE), lowspec(Tc),
                  headw, headv, headw, headv, _layer_spec(l, 1, LANE)],
        out_specs=(outspec(T), outspec(Tc)),
        scratch_shapes=scratch(T) + scratch(Tc),
        compiler_params=_cparams(2),
        name="gla_mixer",
    )(Pl, Pl, Pl, Pl, Pl, Pc, Pc, Pc, Pc, Pc, upf, bias_f, upb, bias_b, norm_g)


def _conv_kernel(prev_ref, cur_ref, next_ref, dw_ref, dwb_ref, lng_ref, lnb_ref, pw_ref, pwb_ref,
                 o_ref, pad_ref, sh_ref, acc_ref, *, tt):
    c = pl.program_id(1)
    H = HALO
    W = GROUP_W

    def glu(u):
        u = u.astype(F32)
        return u[:, :W] * jax.nn.sigmoid(u[:, W:])

    pad_ref[0:H, :] = jnp.where(c > 0, glu(prev_ref[0]), 0.0)
    pad_ref[H:H + tt, :] = glu(cur_ref[0])
    pad_ref[H + tt:H + tt + H, :] = jnp.where(c < pl.num_programs(1) - 1, glu(next_ref[0]), 0.0)

    n_sh = tt + 2 * H - SUBLANE
    sh_ref[0, :, :] = pad_ref[...]
    for r in range(1, SUBLANE):
        sh_ref[r, 0:n_sh, :] = pad_ref[r:r + n_sh, :]

    rs = min(tt, 128)
    off = H - CONV_K // 2
    for r0 in range(0, tt, rs):
        for lb in range(W // LANE):
            cols = slice(lb * LANE, (lb + 1) * LANE)
            acc = jnp.zeros((rs, LANE), F32) + dwb_ref[:, cols]
            for k in range(CONV_K):
                r = (off + k) % SUBLANE
                a0 = r0 + off + k - r
                acc = acc + sh_ref[r, a0:a0 + rs, cols] * dw_ref[k:k + 1, cols]
            acc_ref[r0:r0 + rs, cols] = acc

    h = acc_ref[...]
    mu = jnp.mean(h, axis=-1, keepdims=True)
    hc = h - mu
    var = jnp.mean(hc * hc, axis=-1, keepdims=True)
    y = hc * lax.rsqrt(var + NORM_EPS) * lng_ref[...] + lnb_ref[...]
    y = jnp.dot(_silu(y).astype(BF16), pw_ref[...], preferred_element_type=F32) + pwb_ref[...]
    o_ref[0] = y.astype(BF16)


def _conv_call(P, dw, dw_b, ln_g, ln_b, pw, pw_b, l):
    B, T, _ = P.shape
    tt = min(T, 256)
    vec = _layer_spec(l, 1, GROUP_W)
    return pl.pallas_call(
        functools.partial(_conv_kernel, tt=tt),
        out_shape=jax.ShapeDtypeStruct((B, T, GROUP_W), BF16),
        grid=(B, T // tt),
        in_specs=_halo_specs(T, tt, 2 * GROUP_W, 0) + [
            _layer_spec(l, CONV_K, GROUP_W), vec, vec, vec, _layer_spec(l, GROUP_W, GROUP_W), vec],
        out_specs=pl.BlockSpec((1, tt, GROUP_W), lambda b, c: (b, c, 0)),
        scratch_shapes=[pltpu.VMEM((tt + 2 * HALO, GROUP_W), F32),
                        pltpu.VMEM((SUBLANE, tt + 2 * HALO, GROUP_W), F32),
                        pltpu.VMEM((tt, GROUP_W), F32)],
        compiler_params=_cparams(2),
        name="conv_mixer",
    )(P, P, P, dw, dw_b, ln_g, ln_b, pw, pw_b)


def _rope(x, cos, sin):
    hw = ROPE_AXIS_DIM // 2
    lane = lax.broadcasted_iota(jnp.int32, x.shape, 1)
    first_half = (lane % (2 * hw)) < hw
    swapped = jnp.where(first_half, pltpu.roll(x, LANE - hw, 1), pltpu.roll(x, hw, 1))
    return x * cos + swapped * sin


def _diff_kernel(*refs, rope, lam_init, n_kv):
    it = iter(refs)
    q_ref = next(it)
    kv = [(next(it), next(it)) for _ in range(n_kv)]
    if rope:
        cq, sq, ck, sk = next(it), next(it), next(it), next(it)
    lq1, lk1, lq2, lk2, sg = next(it), next(it), next(it), next(it), next(it)
    o_ref, kbuf, vbuf = next(it), next(it), next(it)

    @pl.when(pl.program_id(2) == 0)
    def _():
        r0 = 0
        for i, (k_ref, v_ref) in enumerate(kv):
            n = k_ref.shape[1]
            k = k_ref[0]
            if rope and i == 0:
                k = _rope(k.astype(F32), ck[...], sk[...]).astype(BF16)
            kbuf[r0:r0 + n, :] = k
            vbuf[r0:r0 + n, 0:LANE] = v_ref[0]
            r0 += n
        vbuf[:, LANE:2 * LANE] = jnp.ones((vbuf.shape[0], LANE), BF16)

    q = q_ref[0].astype(F32)
    if rope:
        q = _rope(q, cq[...], sq[...])
    q = q * (DIFF_QK_DIM ** -0.5 * math.log2(math.e))
    lane = lax.broadcasted_iota(jnp.int32, q.shape, 1)
    q1 = jnp.where(lane < DIFF_QK_DIM, q, 0.0).astype(BF16)
    q2 = jnp.where(lane >= DIFF_QK_DIM, q, 0.0).astype(BF16)
    k = kbuf[...]
    v1 = vbuf[...]
    s1 = lax.dot_general(q1, k, NT_DIMS, preferred_element_type=F32)
    s2 = lax.dot_general(q2, k, NT_DIMS, preferred_element_type=F32)
    e1 = jnp.exp2(s1 - jnp.max(s1, axis=-1, keepdims=True)).astype(BF16)
    e2 = jnp.exp2(s2 - jnp.max(s2, axis=-1, keepdims=True)).astype(BF16)
    r1 = jnp.dot(e1, v1, preferred_element_type=F32)
    r2 = jnp.dot(e2, v1, preferred_element_type=F32)
    lam = (jnp.exp(jnp.sum(lq1[...] * lk1[...], axis=-1, keepdims=True))
           - jnp.exp(jnp.sum(lq2[...] * lk2[...], axis=-1, keepdims=True)) + lam_init)
    o = r1[:, :LANE] * (1.0 / r1[:, LANE:LANE + 1]) - r2[:, :LANE] * (lam / r2[:, LANE:LANE + 1])
    o_ref[0] = (_rms(o, sg[...]) * (1.0 - lam_init)).astype(BF16)


def _diff_call(Pq, kv_sources, rope_tabs, lq1, lk1, lq2, lk2, subln_g, lam_init, l):
    B, T, _ = Pq.shape
    tq = min(T, 256)
    rope = rope_tabs is not None
    in_specs = [pl.BlockSpec((1, tq, LANE), lambda b, h, i: (b, i, COL_DQ + h))]
    args = [Pq]
    Tk = 0
    for Ps in kv_sources:
        n = Ps.shape[1]
        in_specs.append(pl.BlockSpec((1, n, LANE), lambda b, h, i: (b, 0, COL_DK + h)))
        in_specs.append(pl.BlockSpec((1, n, LANE), lambda b, h, i: (b, 0, COL_DV + h)))
        args += [Ps, Ps]
        Tk += n
    if rope:
        cos, sin = rope_tabs
        in_specs += [pl.BlockSpec((tq, LANE), lambda b, h, i: (i, 0)), pl.BlockSpec((tq, LANE), lambda b, h, i: (i, 0)),
                     pl.BlockSpec((T, LANE), lambda b, h, i: (0, 0)), pl.BlockSpec((T, LANE), lambda b, h, i: (0, 0))]
        args += [cos, sin, cos, sin]
    small = _layer_spec(l, 1, DIFF_QK_DIM)
    in_specs += [small, small, small, small, _layer_spec(l, 1, LANE)]
    args += [lq1, lk1, lq2, lk2, subln_g]
    return pl.pallas_call(
        functools.partial(_diff_kernel, rope=rope, lam_init=lam_init, n_kv=len(kv_sources)),
        out_shape=jax.ShapeDtypeStruct((B, T, GROUP_W), BF16),
        grid=(B, N_HEADS, T // tq),
        in_specs=in_specs,
        out_specs=pl.BlockSpec((1, tq, LANE), lambda b, h, i: (b, i, h)),
        scratch_shapes=[pltpu.VMEM((Tk, LANE), BF16), pltpu.VMEM((Tk, 2 * LANE), BF16)],
        compiler_params=_cparams(3),
        name="diff_attn",
    )(*args)


def _outproj_kernel(a_ref, b_ref, c_ref, d_ref, w_ref, x_ref, gate_ref, sh_ref, sc_ref, ng_ref, wrh_ref, wrl_ref,
                    x1_ref, h2_ref, aff_ref, mix_ref):
    W = GROUP_W
    for p, m_ref in enumerate((a_ref, b_ref, c_ref, d_ref)):
        mix_ref[:, p * W:(p + 1) * W] = m_ref[0]
    y = jnp.dot(mix_ref[...], w_ref[...], preferred_element_type=F32)
    x1 = x_ref[0] + gate_ref[...] * y
    x1_ref[0] = x1
    h = _norm_mod(x1, ng_ref[...], sc_ref[...], sh_ref[...])
    hh = h.astype(BF16)
    hl = (h - hh.astype(F32)).astype(BF16)
    h2_ref[0] = hh
    lg = (jnp.dot(hh, wrh_ref[...], preferred_element_type=F32)
          + jnp.dot(hl, wrh_ref[...], preferred_element_type=F32)
          + jnp.dot(hh, wrl_ref[...], preferred_element_type=F32))
    lane = lax.broadcasted_iota(jnp.int32, lg.shape, 1)
    lg = jnp.where(lane < N_EXPERTS, lg, -jnp.inf)
    e = jnp.exp(lg - jnp.max(lg, axis=-1, keepdims=True))
    aff_ref[0] = e / jnp.sum(e, axis=-1, keepdims=True)


def _outproj_call(mix, w_out, x, mod, row_of, ng, wr_hi, wr_lo, l):
    Bx, T, D = x.shape
    tm = min(T, 256)
    mixspec = pl.BlockSpec((1, tm, GROUP_W), lambda b, i: (b, i, 0))
    rowspec = pl.BlockSpec((1, tm, D), lambda b, i: (b, i, 0))
    return pl.pallas_call(
        _outproj_kernel,
        out_shape=(jax.ShapeDtypeStruct((Bx, T, D), F32), jax.ShapeDtypeStruct((Bx, T, D), BF16),
                   jax.ShapeDtypeStruct((Bx, T, LANE), F32)),
        grid=(Bx, T // tm),
        in_specs=[mixspec, mixspec, mixspec, mixspec,
                  _layer_spec(l, D, D),
                  rowspec, _mod_spec(l, row_of, 2), _mod_spec(l, row_of, 3), _mod_spec(l, row_of, 4),
                  _layer_spec(l, 1, D), _layer_spec(l, D, LANE), _layer_spec(l, D, LANE)],
        out_specs=(rowspec, rowspec, pl.BlockSpec((1, tm, LANE), lambda b, i: (b, i, 0))),
        scratch_shapes=[pltpu.VMEM((tm, D), BF16)],
        compiler_params=_cparams(2),
        name="out_proj",
    )(*mix, w_out, x, mod, mod, mod, ng, wr_hi, wr_lo)


def _excl_prefix(x):
    rows, T = x.shape
    ri = lax.broadcasted_iota(jnp.int32, (LANE, LANE), 0)
    ci = lax.broadcasted_iota(jnp.int32, (LANE, LANE), 1)
    upper = jnp.where(ri <= ci, 1.0, 0.0).astype(BF16)
    carry = jnp.zeros((rows, 1), F32)
    out = []
    for b in range(T // LANE):
        xb = x[:, b * LANE:(b + 1) * LANE]
        inc = jnp.dot(xb.astype(BF16), upper, preferred_element_type=F32)
        out.append(inc - xb + carry)
        carry = carry + jnp.sum(xb, axis=1, keepdims=True)
    return jnp.concatenate(out, axis=1)


def _route_kernel(aff_ref, slot_row_ref, slot_col_ref, aff_row_ref, st_ref, *, cap):
    T = aff_ref.shape[1]
    E = N_EXPERTS
    arow = aff_ref[0].T[0:E, :]
    keys = lax.bitcast_convert_type(arow, jnp.int32)
    v = jnp.zeros((E, 1), jnp.int32)
    for bit in range(30, -1, -1):
        cand = v | (1 << bit)
        cnt = jnp.sum(jnp.where(keys >= cand, 1.0, 0.0), axis=1, keepdims=True)
        v = jnp.where(cnt >= cap, cand, v)
    above = keys > v
    tied = jnp.where(keys == v, 1.0, 0.0)
    room = cap - jnp.sum(jnp.where(above, 1.0, 0.0), axis=1, keepdims=True)
    kept = jnp.where(above, 1.0, jnp.where(_excl_prefix(tied) < room, tied, 0.0))
    slot = jnp.where(kept > 0.0, _excl_prefix(kept), float(T))
    for e in range(E):
        slot_row_ref[0, e] = slot[e:e + 1, :].astype(jnp.int32)
        aff_row_ref[0, e] = arow[e:e + 1, :]
    st_ref[...] = jnp.full(st_ref.shape, float(T), F32)
    st_ref[0:E, :] = slot
    slot_col_ref[0] = st_ref[...].T.astype(jnp.int32)


def _route_call(aff, cap):
    B, T, _ = aff.shape
    E = N_EXPERTS
    return pl.pallas_call(
        functools.partial(_route_kernel, cap=cap),
        out_shape=(jax.ShapeDtypeStruct((B, E, 1, T), jnp.int32),
                   jax.ShapeDtypeStruct((B, T, LANE), jnp.int32),
                   jax.ShapeDtypeStruct((B, E, 1, T), F32)),
        grid=(B,),
        in_specs=[pl.BlockSpec((1, T, LANE), lambda b: (b, 0, 0))],
        out_specs=(pl.BlockSpec((1, E, 1, T), lambda b: (b, 0, 0, 0)),
                   pl.BlockSpec((1, T, LANE), lambda b: (b, 0, 0)),
                   pl.BlockSpec((1, E, 1, T), lambda b: (b, 0, 0, 0))),
        scratch_shapes=[pltpu.VMEM((LANE, T), F32)],
        compiler_params=_cparams(1),
        name="router_route",
    )(aff)


def _dispatch_kernel(slot_ref, affr_ref, h_ref, xe_ref, gs_ref):
    eg, cap, D = xe_ref.shape
    T = h_ref.shape[1]
    slot = lax.broadcasted_iota(jnp.int32, (cap, T), 0)
    hits = [slot_ref[0, e] == slot for e in range(eg)]
    onehot = jnp.concatenate([jnp.where(h, 1.0, 0.0).astype(BF16) for h in hits], axis=0)
    x = jnp.dot(onehot, h_ref[0], preferred_element_type=F32).astype(BF16)
    xe_ref[...] = x.reshape(eg, cap, D)
    for e in range(eg):
        g = jnp.sum(jnp.where(hits[e], affr_ref[0, e], 0.0), axis=-1, keepdims=True)
        gs_ref[e] = jnp.broadcast_to(g, (cap, LANE))


def _dispatch_call(slot_row, aff_row, h2, cap):
    B, T, D = h2.shape
    E = N_EXPERTS
    eg = max(1, min(E, 512 // cap))
    return pl.pallas_call(
        _dispatch_kernel,
        out_shape=(jax.ShapeDtypeStruct((E, B * cap, D), BF16), jax.ShapeDtypeStruct((E, B * cap, LANE), F32)),
        grid=(B, E // eg),
        in_specs=[pl.BlockSpec((1, eg, 1, T), lambda b, g: (b, g, 0, 0)),
                  pl.BlockSpec((1, eg, 1, T), lambda b, g: (b, g, 0, 0)),
                  pl.BlockSpec((1, T, D), lambda b, g: (b, 0, 0))],
        out_specs=(pl.BlockSpec((eg, cap, D), lambda b, g: (g, b, 0)),
                   pl.BlockSpec((eg, cap, LANE), lambda b, g: (g, b, 0))),
        compiler_params=_cparams(2),
        name="moe_dispatch",
    )(slot_row, aff_row, h2)


def _expert_kernel(*refs, n_groups, nf):
    xs = refs[:n_groups]
    wg_ref, wu_ref, wd_ref = refs[n_groups:n_groups + 3]
    gss = refs[n_groups + 3:2 * n_groups + 3]
    ys = refs[2 * n_groups + 3:3 * n_groups + 3]
    hms = refs[3 * n_groups + 3:]
    s = pl.program_id(1)
    tf = wg_ref.shape[-1]

    @pl.when(s < nf)
    def _():
        wg = wg_ref[...].astype(BF16)
        wu = wu_ref[...].astype(BF16)
        for x_ref, hm_ref in zip(xs, hms):
            x = x_ref[...]
            a = jnp.dot(x, wg, preferred_element_type=F32)
            u = jnp.dot(x, wu, preferred_element_type=F32)
            hm_ref[s] = (_silu(a) * u).astype(BF16)

    @pl.when(s >= nf)
    def _():
        for hm_ref, gs_ref, y_ref in zip(hms, gss, ys):
            acc = jnp.dot(hm_ref[0], wd_ref[0:tf, :].astype(BF16), preferred_element_type=F32)
            for f in range(1, nf):
                acc = acc + jnp.dot(hm_ref[f], wd_ref[f * tf:(f + 1) * tf, :].astype(BF16),
                                    preferred_element_type=F32)
            y_ref[...] = (acc * gs_ref[...][:, 0:1]).astype(BF16)


def _expert_call(groups, w_gate, w_up, w_down, l):
    E, _, D = groups[0][0].shape
    F = w_gate.shape[-1]
    tf = 256
    tn = 512
    nf = F // tf
    nn = D // tn
    n = len(groups)
    Ms = [g[0].shape[1] for g in groups]
    fidx = lambda s: jnp.minimum(s, nf - 1)
    nidx = lambda s: jnp.maximum(s - nf, 0)
    in_specs = ([pl.BlockSpec((None, M, D), lambda e, s: (e, 0, 0)) for M in Ms]
                + [pl.BlockSpec((None, None, D, tf), lambda e, s: (l, e, 0, fidx(s))),
                   pl.BlockSpec((None, None, D, tf), lambda e, s: (l, e, 0, fidx(s))),
                   pl.BlockSpec((None, None, F, tn), lambda e, s: (l, e, 0, nidx(s)))]
                + [pl.BlockSpec((None, M, LANE), lambda e, s: (e, 0, 0)) for M in Ms])
    return pl.pallas_call(
        functools.partial(_expert_kernel, n_groups=n, nf=nf),
        out_shape=[jax.ShapeDtypeStruct((E, M, D), BF16) for M in Ms],
        grid=(E, nf + nn),
        in_specs=in_specs,
        out_specs=[pl.BlockSpec((None, M, tn), lambda e, s: (e, 0, nidx(s))) for M in Ms],
        scratch_shapes=[pltpu.VMEM((nf, M, tf), BF16) for M in Ms],
        compiler_params=_cparams(2),
        name="moe_experts",
    )(*[g[0] for g in groups], w_gate, w_up, w_down, *[g[1] for g in groups])


def _combine_kernel(slot_ref, y_ref, x_ref, gate_ref, fg_ref, o_ref, *, final_norm):
    cap = y_ref.shape[1]
    tm = x_ref.shape[1]
    lane_slot = lax.broadcasted_iota(jnp.int32, (tm, cap), 1)
    slot = slot_ref[0]
    acc = jnp.zeros(x_ref.shape[1:], F32)
    for e in range(N_EXPERTS):
        hit = jnp.where(slot[:, e:e + 1] == lane_slot, 1.0, 0.0).astype(BF16)
        acc = acc + jnp.dot(hit, y_ref[e], preferred_element_type=F32)
    x2 = x_ref[0] + gate_ref[...] * acc
    if final_norm:
        x2 = _rms(x2, fg_ref[...])
    o_ref[0] = x2


def _combine_call(slot_col, y, x1, mod, row_of, final_g, cap, final_norm, l):
    B, T, D = x1.shape
    tm = min(T, 512)
    return pl.pallas_call(
        functools.partial(_combine_kernel, final_norm=final_norm),
        out_shape=jax.ShapeDtypeStruct((B, T, D), F32),
        grid=(B, T // tm),
        in_specs=[pl.BlockSpec((1, tm, LANE), lambda b, i: (b, i, 0)),
                  pl.BlockSpec((N_EXPERTS, cap, D), lambda b, i: (0, b, 0)),
                  pl.BlockSpec((1, tm, D), lambda b, i: (b, i, 0)),
                  _mod_spec(l, row_of, 5),
                  pl.BlockSpec((1, D), lambda b, i: (0, 0))],
        out_specs=pl.BlockSpec((1, tm, D), lambda b, i: (b, i, 0)),
        compiler_params=_cparams(2),
        name="moe_combine",
    )(slot_col, y, x1, mod, final_g)


def _rope_tables(n):
    rows = n // GRID_W
    row = jnp.repeat(jnp.arange(rows), GRID_W).astype(F32)
    col = jnp.tile(jnp.arange(GRID_W), rows).astype(F32)
    inv_freq = ROPE_BASE ** (-jnp.arange(0, ROPE_AXIS_DIM, 2, dtype=F32) / ROPE_AXIS_DIM)
    ar = row[:, None] * inv_freq
    ac = col[:, None] * inv_freq
    cos = jnp.concatenate([jnp.cos(ar), jnp.cos(ar), jnp.cos(ac), jnp.cos(ac)], axis=-1)
    sin = jnp.concatenate([-jnp.sin(ar), jnp.sin(ar), -jnp.sin(ac), jnp.sin(ac)], axis=-1)
    return jnp.tile(cos, (1, 2)), jnp.tile(sin, (1, 2))


def _reorder_w_in(w_in):
    G = GROUP_W
    o = 0
    parts = {}
    for name, size in (("pool", G), ("gq", G), ("gk", G), ("gv", G), ("gg", G), ("low", 2 * GLA_GATE_RANK),
                       ("conv", 2 * G), ("dq", G), ("dk", G), ("dv", G)):
        parts[name] = w_in[..., o:o + size]
        o += size
    pad = jnp.zeros(w_in.shape[:-1] + (IN_COLS - COL_LOW * LANE - 2 * GLA_GATE_RANK,), w_in.dtype)
    order = ["conv", "pool", "gq", "gk", "gv", "gg", "dq", "dk", "dv", "low"]
    return jnp.concatenate([parts[n] for n in order] + [pad], axis=-1).astype(BF16)


def kernel(x, c, ctx, c_ctx, w_ada, b_ada, norm1_g, norm2_g, w_in, pool_w, pool_scale, gla_gk_up_f, gla_gk_bias_f, gla_gk_up_b, gla_gk_bias_b, gla_norm_g, conv_dw, conv_dw_b, conv_ln_g, conv_ln_b, conv_pw, conv_pw_b, diff_lq1, diff_lk1, diff_lq2, diff_lk2, diff_subln_g, w_out, w_router, w_exp_gate, w_exp_up, w_exp_down, final_norm_g):
    B, N, D = x.shape
    Tc = ctx.shape[1]
    L = w_ada.shape[0]
    assert D == D_MODEL and N % 256 == 0 and Tc % 128 == 0 and B < MOD_ROWS

    cc = jnp.concatenate([c, c_ctx[None, :], jnp.zeros((MOD_ROWS - B - 1, D), F32)], axis=0)
    mod = _ada_call(cc, w_ada, b_ada).reshape(L, MOD_ROWS, 6, 1, D)
    lat_row = lambda b, *_: b
    ctx_row = lambda *_: B

    rows3 = lambda a: a.reshape(L, 1, -1)
    w_in_r = _reorder_w_in(w_in)
    w_out_b = w_out.astype(BF16)
    pool_w_b = pool_w.astype(BF16)
    conv_pw_b16 = conv_pw.astype(BF16)
    R = GLA_GATE_RANK
    zpad = lambda a, lo: jnp.pad(a, ((0, 0), (lo, LANE - R - lo), (0, 0))).astype(BF16)
    upf = zpad(gla_gk_up_f, 0)
    upb = zpad(gla_gk_up_b, R)
    wr = jnp.pad(w_router, ((0, 0), (0, 0), (0, LANE - N_EXPERTS)))
    wr_hi = wr.astype(BF16)
    wr_lo = (wr - wr_hi.astype(F32)).astype(BF16)
    rope_tabs = _rope_tables(N)
    fg = final_norm_g.reshape(1, D)
    n1, n2 = rows3(norm1_g), rows3(norm2_g)
    gla_args = (upf, rows3(gla_gk_bias_f), upb, rows3(gla_gk_bias_b), rows3(gla_norm_g))
    conv_args = (conv_dw, rows3(conv_dw_b), rows3(conv_ln_g), rows3(conv_ln_b), conv_pw_b16, rows3(conv_pw_b))
    diff_vecs = (rows3(diff_lq1), rows3(diff_lk1), rows3(diff_lq2), rows3(diff_lk2), rows3(diff_subln_g))
    pool_args = (pool_w_b, rows3(pool_scale))
    cap_l = EC_CAPACITY * N // N_EXPERTS
    cap_c = EC_CAPACITY * Tc // N_EXPERTS
    flat = lambda a: a.reshape(1, B * Tc, a.shape[-1])
    unflat = lambda a: a.reshape(B, Tc, a.shape[-1])

    for l in range(L):
        last = l == L - 1
        lam_init = 0.8 - 0.6 * math.exp(-0.3 * l)

        Pl = _inproj_call(x, mod, lat_row, n1, w_in_r, l)
        Pc = unflat(_inproj_call(flat(ctx), mod, ctx_row, n1, w_in_r, l))

        gla_l, gla_c = _gla_call(Pl, Pc, *gla_args, l)
        pool_l = _pool_call(Pl, *pool_args, l)
        conv_l = _conv_call(Pl, *conv_args, l)
        diff_l = _diff_call(Pl, [Pl, Pc], rope_tabs, *diff_vecs, lam_init, l)
        x1, h2, aff = _outproj_call((pool_l, gla_l, conv_l, diff_l), w_out_b, x, mod, lat_row, n2, wr_hi, wr_lo, l)
        slot_row, slot_col, aff_row = _route_call(aff, cap_l)
        groups = [_dispatch_call(slot_row, aff_row, h2, cap_l)]

        if not last:
            pool_c = _pool_call(Pc, *pool_args, l)
            conv_c = _conv_call(Pc, *conv_args, l)
            diff_c = _diff_call(Pc, [Pc], None, *diff_vecs, lam_init, l)
            c1, h2c, affc = _outproj_call((flat(pool_c), flat(gla_c), flat(conv_c), flat(diff_c)), w_out_b,
                                          flat(ctx), mod, ctx_row, n2, wr_hi, wr_lo, l)
            slot_row_c, slot_col_c, aff_row_c = _route_call(unflat(affc), cap_c)
            groups.append(_dispatch_call(slot_row_c, aff_row_c, unflat(h2c), cap_c))

        ys = _expert_call(groups, w_exp_gate, w_exp_up, w_exp_down, l)
        x = _combine_call(slot_col, ys[0], x1, mod, lat_row, fg, cap_l, last, l)
        if not last:
            ctx = _combine_call(slot_col_c, ys[1], unflat(c1), mod, ctx_row, fg, cap_c, False, l)

    return x
```

```python
import functools
import math

import jax
import jax.numpy as jnp
from jax import lax
from jax.experimental import pallas as pl
from jax.experimental.pallas import tpu as pltpu

F32 = jnp.float32
BF16 = jnp.bfloat16

D_MODEL = 2048
GRID_W = 64
GROUP_W = D_MODEL // 4
POOL_WINDOWS = (2, 4, 8, 16)
HEAD_DIM = 128
N_HEADS = GROUP_W // HEAD_DIM
GLA_GATE_RANK = 16
GLA_GATE_NORM = 16.0
GLA_CHUNK = 64
CONV_K = 31
DIFF_QK_DIM = 64
ROPE_BASE = 10000.0
ROPE_AXIS_DIM = DIFF_QK_DIM // 2
N_EXPERTS = 16
EXPERT_FF = D_MODEL // 2
EC_CAPACITY = 2
NORM_EPS = 1e-6

LANE = 128
SUBLANE = 8
HALO = 16
VMEM_LIMIT = 56 * 1024 * 1024
MOD_ROWS = 16

COL_CONV = 0
COL_POOL = 8
COL_GQ = 12
COL_GK = 16
COL_GV = 20
COL_GGATE = 24
COL_DQ = 28
COL_DK = 32
COL_DV = 36
COL_LOW = 40
MXU_N = 256
IN_COLS = 42 * LANE
IN_TN = 7 * MXU_N

NT_DIMS = (((1,), (1,)), ((), ()))
TN_DIMS = (((0,), (0,)), ((), ()))


def _cparams(n_axes):
    return pltpu.CompilerParams(dimension_semantics=("arbitrary",) * n_axes, vmem_limit_bytes=VMEM_LIMIT)


def _silu(x):
    return x * jax.nn.sigmoid(x)


def _rms(x, g):
    return x * lax.rsqrt(jnp.mean(x * x, axis=-1, keepdims=True) + NORM_EPS) * g


def _norm_mod(x, g, scale, shift):
    inv = lax.rsqrt(jnp.mean(x * x, axis=-1, keepdims=True) + NORM_EPS)
    return x * inv * (g * (1.0 + scale)) + shift


def _layer_spec(l, *tail):
    return pl.BlockSpec((None,) + tail, lambda *_: (l,) + (0,) * len(tail))


def _mod_spec(l, row_of, k):
    return pl.BlockSpec((None, None, None, 1, D_MODEL), lambda *g: (l, row_of(*g), k, 0, 0))


def _ada_kernel(c_ref, w_ref, b_ref, o_ref):
    sc = _silu(c_ref[...]).astype(BF16)
    o_ref[0] = jnp.dot(sc, w_ref[0].astype(BF16), preferred_element_type=F32) + b_ref[0]


def _ada_call(cc, w_ada, b_ada):
    L, D, W = w_ada.shape
    R = cc.shape[0]
    tn = 1536
    return pl.pallas_call(
        _ada_kernel,
        out_shape=jax.ShapeDtypeStruct((L, R, W), F32),
        grid=(L, W // tn),
        in_specs=[
            pl.BlockSpec((R, D), lambda l, j: (0, 0)),
            pl.BlockSpec((1, D, tn), lambda l, j: (l, 0, j)),
            pl.BlockSpec((1, 1, tn), lambda l, j: (l, 0, j)),
        ],
        out_specs=pl.BlockSpec((1, R, tn), lambda l, j: (l, 0, j)),
        compiler_params=_cparams(2),
        name="ada_mod",
    )(cc, w_ada, b_ada.reshape(L, 1, W))


def _inproj_kernel(x_ref, sh_ref, sc_ref, g_ref, w_ref, o_ref, h_ref):
    @pl.when(pl.program_id(2) == 0)
    def _():
        h_ref[...] = _norm_mod(x_ref[0], g_ref[...], sc_ref[...], sh_ref[...]).astype(BF16)

    o_ref[0] = jnp.dot(h_ref[...], w_ref[...], preferred_element_type=F32).astype(BF16)


def _inproj_call(x, mod, row_of, g, w, l):
    Bx, T, D = x.shape
    tm = min(T, 1024)
    return pl.pallas_call(
        _inproj_kernel,
        out_shape=jax.ShapeDtypeStruct((Bx, T, IN_COLS), BF16),
        grid=(Bx, T // tm, IN_COLS // IN_TN),
        in_specs=[
            pl.BlockSpec((1, tm, D), lambda b, i, j: (b, i, 0)),
            _mod_spec(l, row_of, 0),
            _mod_spec(l, row_of, 1),
            _layer_spec(l, 1, D),
            pl.BlockSpec((None, D, IN_TN), lambda b, i, j: (l, 0, j)),
        ],
        out_specs=pl.BlockSpec((1, tm, IN_TN), lambda b, i, j: (b, i, j)),
        scratch_shapes=[pltpu.VMEM((tm, D), BF16)],
        compiler_params=_cparams(3),
        name="in_proj",
    )(x, mod, mod, g, w)


def _halo_specs(T, tt, width, col_block):
    r = tt // HALO
    nh = T // HALO
    return [
        pl.BlockSpec((1, HALO, width), lambda b, c: (b, jnp.maximum(c * r - 1, 0), col_block)),
        pl.BlockSpec((1, tt, width), lambda b, c: (b, c, col_block)),
        pl.BlockSpec((1, HALO, width), lambda b, c: (b, jnp.minimum((c + 1) * r, nh - 1), col_block)),
    ]


def _pool_kernel(prev_ref, cur_ref, next_ref, w_ref, s_ref, o_ref, pad_ref, *, T, tt):
    c = pl.program_id(1)
    H = HALO
    pad_ref[0:H, :] = jnp.where(c > 0, prev_ref[0].astype(F32), 0.0)
    pad_ref[H:H + tt, :] = cur_ref[0].astype(F32)
    pad_ref[H + tt:H + tt + H, :] = jnp.where(c < pl.num_programs(1) - 1, next_ref[0].astype(F32), 0.0)
    t = c * tt + lax.broadcasted_iota(jnp.int32, (tt, HEAD_DIM), 0)
    for gi, w in enumerate(POOL_WINDOWS):
        cols = slice(gi * HEAD_DIM, (gi + 1) * HEAD_DIM)
        acc = pad_ref[H - w // 2:H - w // 2 + tt, cols]
        for d in range(-w // 2 + 1, w // 2):
            acc = acc + pad_ref[H + d:H + d + tt, cols]
        cnt = jnp.minimum(t + w // 2, T) - jnp.maximum(t - w // 2, 0)
        p = acc / cnt.astype(F32) - pad_ref[H:H + tt, cols]
        y = jnp.dot(p.astype(BF16), w_ref[gi], preferred_element_type=F32)
        o_ref[0, :, cols] = (y * s_ref[:, cols]).astype(BF16)


def _pool_call(P, pool_w, pool_scale, l):
    B, T, _ = P.shape
    tt = min(T, 256)
    return pl.pallas_call(
        functools.partial(_pool_kernel, T=T, tt=tt),
        out_shape=jax.ShapeDtypeStruct((B, T, GROUP_W), BF16),
        grid=(B, T // tt),
        in_specs=_halo_specs(T, tt, GROUP_W, COL_POOL * LANE // GROUP_W) + [
            _layer_spec(l, len(POOL_WINDOWS), HEAD_DIM, HEAD_DIM),
            _layer_spec(l, 1, GROUP_W),
        ],
        out_specs=pl.BlockSpec((1, tt, GROUP_W), lambda b, c: (b, c, 0)),
        scratch_shapes=[pltpu.VMEM((tt + 2 * HALO, GROUP_W), F32)],
        compiler_params=_cparams(2),
        name="pool_mixer",
    )(P, P, P, pool_w, pool_scale)


def _log_sigmoid(z):
    return jnp.minimum(z, 0.0) - jnp.log(1.0 + jnp.exp(-jnp.abs(z)))


def _gla_block_consts(rb):
    ri = lax.broadcasted_iota(jnp.int32, (rb, rb), 0)
    ci = lax.broadcasted_iota(jnp.int32, (rb, rb), 1)
    shift = GLA_CHUNK.bit_length() - 1
    same = jnp.right_shift(ri, shift) == jnp.right_shift(ci, shift)
    out = []
    for causal in (ci <= ri, ci >= ri):
        tri = jnp.where(causal, jnp.where(same, 1.0, 0.0), 0.0)
        out.append((tri.astype(BF16), tri > 0.0))
    return out


def _gla_prepare(items, consts, q_ref, k_ref, v_ref, g_ref, qg_ref, u_ref, dec_ref, o_ref):
    C = GLA_CHUNK
    rb = consts[0][1].shape[0]
    nchunk = rb // C
    Gs = []
    for d, rows, _ in items:
        tri = consts[d][0]
        g = g_ref[d, rows, :]
        g_hi = g.astype(BF16)
        g_lo = (g - g_hi.astype(F32)).astype(BF16)
        Gs.append(jnp.dot(tri, g_hi, preferred_element_type=F32) + jnp.dot(tri, g_lo, preferred_element_type=F32))
    staged = []
    for (d, rows, c0), G in zip(items, Gs):
        tot = C - 1 if d == 0 else 0
        tots = [G[ci * C + tot:ci * C + tot + 1, :] for ci in range(nchunk)]
        Gt = jnp.concatenate([jnp.broadcast_to(t, (C, LANE)) for t in tots], axis=0)
        r = 0.5 * Gt
        q = q_ref[0, rows, :].astype(F32) * (HEAD_DIM ** -0.5)
        k = k_ref[0, rows, :].astype(F32)
        qg = (q * jnp.exp(G - r)).astype(BF16)
        kg = (k * jnp.exp(r - G)).astype(BF16)
        qg_ref[d, rows, :] = (q * jnp.exp(G)).astype(BF16)
        kd = (k * jnp.exp(Gt - G)).astype(BF16)
        for ci in range(nchunk):
            dec_ref[d, c0 + ci] = jnp.exp(jnp.broadcast_to(tots[ci], (8, LANE)))
        staged.append((qg, kg, kd))
    atts = [lax.dot_general(qg, kg, NT_DIMS, preferred_element_type=F32) for qg, kg, _ in staged]
    for (d, rows, c0), att, (_, _, kd) in zip(items, atts, staged):
        v = v_ref[0, rows, :]
        att = jnp.where(consts[d][1], att, 0.0).astype(BF16)
        o_ref[d, rows, :] = jnp.dot(att, v, preferred_element_type=F32)
        for ci in range(nchunk):
            cr = slice(ci * C, (ci + 1) * C)
            u_ref[d, c0 + ci] = lax.dot_general(v[cr], kd[cr], TN_DIMS, preferred_element_type=F32)


def _gla_kernel(ql, kl, vl, gtl, lowl, qc, kc, vc, gtc, lowc, upf, bf, upb, bb, ng,
                ol_ref, oc_ref, gl, qgl, ul, decl, sbl, osl, gc, qgc, uc, decc, sbc, osc):
    C = GLA_CHUNK
    T = ql.shape[1]
    Tc = qc.shape[1]

    def gates(low_ref, g_ref):
        low = low_ref[0]
        zf = jnp.dot(low, upf[...], preferred_element_type=F32) + bf[...]
        zb = jnp.dot(low, upb[...], preferred_element_type=F32) + bb[...]
        g_ref[0] = _log_sigmoid(zf) * (1.0 / GLA_GATE_NORM)
        g_ref[1] = _log_sigmoid(zb) * (1.0 / GLA_GATE_NORM)

    def prepare(q_ref, k_ref, v_ref, g_ref, qg_ref, u_ref, dec_ref, o_ref, Tx):
        rb = min(Tx, 256)
        nb = Tx // rb
        per = 2 if nb % 2 == 0 else 1
        consts = _gla_block_consts(rb)

        def body(j, carry):
            items = []
            for p in range(per):
                blk = j * per + p
                rows = pl.ds(pl.multiple_of(blk * rb, rb), rb)
                items += [(d, rows, blk * (rb // C)) for d in (0, 1)]
            _gla_prepare(items, consts, q_ref, k_ref, v_ref, g_ref, qg_ref, u_ref, dec_ref, o_ref)
            return carry

        lax.fori_loop(0, nb // per, body, 0)

    def recur(u_ref, dec_ref, sb_ref, n, Sf, Sb):
        def step(d, c, S):
            sb_ref[d, c] = S.astype(BF16)
            return S * dec_ref[d, c, 0:1, :] + u_ref[d, c]

        def body(i, carry):
            Sf, Sb = carry
            return step(0, i, Sf), step(1, n - 1 - i, Sb)

        return lax.fori_loop(0, n, body, (Sf, Sb), unroll=min(n, 4))

    def inter(qg_ref, sb_ref, o_ref, n):
        grp = min(n, 4)
        per = 2 if (n // grp) % 2 == 0 else 1

        def body(j, carry):
            work = []
            for p in range(per):
                blk = j * per + p
                rows = pl.ds(pl.multiple_of(blk * grp * C, grp * C), grp * C)
                for d in (0, 1):
                    qg = qg_ref[d, rows, :]
                    parts = [lax.dot_general(qg[ci * C:(ci + 1) * C], sb_ref[d, blk * grp + ci], NT_DIMS,
                                             preferred_element_type=F32) for ci in range(grp)]
                    work.append((d, rows, parts))
            for d, rows, parts in work:
                o_ref[d, rows, :] += jnp.concatenate(parts, axis=0)
            return carry

        lax.fori_loop(0, n // (grp * per), body, 0)

    def finish(o_s, gt_ref, o_ref):
        o = _rms(o_s[0] + o_s[1], ng[...])
        o_ref[0] = (o * _silu(gt_ref[0].astype(F32))).astype(BF16)

    gates(lowc, gc)
    gates(lowl, gl)
    prepare(qc, kc, vc, gc, qgc, uc, decc, osc, Tc)
    prepare(ql, kl, vl, gl, qgl, ul, decl, osl, T)
    S0 = jnp.zeros((HEAD_DIM, HEAD_DIM), F32)
    Sf, Sb = recur(uc, decc, sbc, Tc // C, S0, S0)
    recur(ul, decl, sbl, T // C, Sf, Sb)
    inter(qgc, sbc, osc, Tc // C)
    inter(qgl, sbl, osl, T // C)
    finish(osl, gtl, ol_ref)
    finish(osc, gtc, oc_ref)


def _gla_call(Pl, Pc, upf, bias_f, upb, bias_b, norm_g, l):
    B, T, _ = Pl.shape
    Tc = Pc.shape[1]

    def colspec(Tx, col):
        return pl.BlockSpec((1, Tx, LANE), lambda b, h: (b, 0, col + h))

    def lowspec(Tx):
        return pl.BlockSpec((1, Tx, LANE), lambda b, h: (b, 0, COL_LOW))

    headw = pl.BlockSpec((None, LANE, LANE), lambda b, h: (l, 0, h))
    headv = pl.BlockSpec((None, 1, LANE), lambda b, h: (l, 0, h))
    outspec = lambda Tx: pl.BlockSpec((1, Tx, LANE), lambda b, h: (b, 0, h))

    def scratch(Tx):
        n = Tx // GLA_CHUNK
        return [pltpu.VMEM((2, Tx, LANE), F32),
                pltpu.VMEM((2, Tx, LANE), BF16),
                pltpu.VMEM((2, n, HEAD_DIM, HEAD_DIM), F32),
                pltpu.VMEM((2, n, 8, LANE), F32),
                pltpu.VMEM((2, n, HEAD_DIM, HEAD_DIM), BF16),
                pltpu.VMEM((2, Tx, LANE), F32)]

    return pl.pallas_call(
        _gla_kernel,
        out_shape=(jax.ShapeDtypeStruct((B, T, GROUP_W), BF16), jax.ShapeDtypeStruct((B, Tc, GROUP_W), BF16)),
        grid=(B, N_HEADS),
        in_specs=[colspec(T, COL_GQ), colspec(T, COL_GK), colspec(T, COL_GV), colspec(T, COL_GGATE), lowspec(T),
                  colspec(Tc, COL_GQ), colspec(Tc, COL_GK), colspec(Tc, COL_GV), colspec(Tc, COL_GGATE), lowspec(Tc),
                  headw, headv, headw, headv, _layer_spec(l, 1, LANE)],
        out_specs=(outspec(T), outspec(Tc)),
        scratch_shapes=scratch(T) + scratch(Tc),
        compiler_params=_cparams(2),
        name="gla_mixer",
    )(Pl, Pl, Pl, Pl, Pl, Pc, Pc, Pc, Pc, Pc, upf, bias_f, upb, bias_b, norm_g)


def _conv_kernel(prev_ref, cur_ref, next_ref, dw_ref, dwb_ref, lng_ref, lnb_ref, pw_ref, pwb_ref,
                 o_ref, pad_ref, sh_ref, acc_ref, *, tt):
    c = pl.program_id(1)
    H = HALO
    W = GROUP_W

    def glu(u):
        u = u.astype(F32)
        return u[:, :W] * jax.nn.sigmoid(u[:, W:])

    pad_ref[0:H, :] = jnp.where(c > 0, glu(prev_ref[0]), 0.0)
    pad_ref[H:H + tt, :] = glu(cur_ref[0])
    pad_ref[H + tt:H + tt + H, :] = jnp.where(c < pl.num_programs(1) - 1, glu(next_ref[0]), 0.0)

    n_sh = tt + 2 * H - SUBLANE
    sh_ref[0, :, :] = pad_ref[...]
    for r in range(1, SUBLANE):
        sh_ref[r, 0:n_sh, :] = pad_ref[r:r + n_sh, :]

    rs = min(tt, 128)
    off = H - CONV_K // 2
    for r0 in range(0, tt, rs):
        for lb in range(W // LANE):
            cols = slice(lb * LANE, (lb + 1) * LANE)
            acc = jnp.zeros((rs, LANE), F32) + dwb_ref[:, cols]
            for k in range(CONV_K):
                r = (off + k) % SUBLANE
                a0 = r0 + off + k - r
                acc = acc + sh_ref[r, a0:a0 + rs, cols] * dw_ref[k:k + 1, cols]
            acc_ref[r0:r0 + rs, cols] = acc

    h = acc_ref[...]
    mu = jnp.mean(h, axis=-1, keepdims=True)
    hc = h - mu
    var = jnp.mean(hc * hc, axis=-1, keepdims=True)
    y = hc * lax.rsqrt(var + NORM_EPS) * lng_ref[...] + lnb_ref[...]
    y = jnp.dot(_silu(y).astype(BF16), pw_ref[...], preferred_element_type=F32) + pwb_ref[...]
    o_ref[0] = y.astype(BF16)


def _conv_call(P, dw, dw_b, ln_g, ln_b, pw, pw_b, l):
    B, T, _ = P.shape
    tt = min(T, 256)
    vec = _layer_spec(l, 1, GROUP_W)
    return pl.pallas_call(
        functools.partial(_conv_kernel, tt=tt),
        out_shape=jax.ShapeDtypeStruct((B, T, GROUP_W), BF16),
        grid=(B, T // tt),
        in_specs=_halo_specs(T, tt, 2 * GROUP_W, 0) + [
            _layer_spec(l, CONV_K, GROUP_W), vec, vec, vec, _layer_spec(l, GROUP_W, GROUP_W), vec],
        out_specs=pl.BlockSpec((1, tt, GROUP_W), lambda b, c: (b, c, 0)),
        scratch_shapes=[pltpu.VMEM((tt + 2 * HALO, GROUP_W), F32),
                        pltpu.VMEM((SUBLANE, tt + 2 * HALO, GROUP_W), F32),
                        pltpu.VMEM((tt, GROUP_W), F32)],
        compiler_params=_cparams(2),
        name="conv_mixer",
    )(P, P, P, dw, dw_b, ln_g, ln_b, pw, pw_b)


def _rope(x, cos, sin):
    hw = ROPE_AXIS_DIM // 2
    lane = lax.broadcasted_iota(jnp.int32, x.shape, 1)
    first_half = (lane % (2 * hw)) < hw
    swapped = jnp.where(first_half, pltpu.roll(x, LANE - hw, 1), pltpu.roll(x, hw, 1))
    return x * cos + swapped * sin


def _diff_kernel(*refs, rope, lam_init, n_kv):
    it = iter(refs)
    q_ref = next(it)
    kv = [(next(it), next(it)) for _ in range(n_kv)]
    if rope:
        cq, sq, ck, sk = next(it), next(it), next(it), next(it)
    lq1, lk1, lq2, lk2, sg = next(it), next(it), next(it), next(it), next(it)
    o_ref, kbuf, vbuf = next(it), next(it), next(it)

    @pl.when(pl.program_id(2) == 0)
    def _():
        r0 = 0
        for i, (k_ref, v_ref) in enumerate(kv):
            n = k_ref.shape[1]
            k = k_ref[0]
            if rope and i == 0:
                k = _rope(k.astype(F32), ck[...], sk[...]).astype(BF16)
            kbuf[r0:r0 + n, :] = k
            vbuf[r0:r0 + n, 0:LANE] = v_ref[0]
            r0 += n
        vbuf[:, LANE:2 * LANE] = jnp.ones((vbuf.shape[0], LANE), BF16)

    q = q_ref[0].astype(F32)
    if rope:
        q = _rope(q, cq[...], sq[...])
    q = q * (DIFF_QK_DIM ** -0.5 * math.log2(math.e))
    lane = lax.broadcasted_iota(jnp.int32, q.shape, 1)
    q1 = jnp.where(lane < DIFF_QK_DIM, q, 0.0).astype(BF16)
    q2 = jnp.where(lane >= DIFF_QK_DIM, q, 0.0).astype(BF16)
    k = kbuf[...]
    v1 = vbuf[...]
    s1 = lax.dot_general(q1, k, NT_DIMS, preferred_element_type=F32)
    s2 = lax.dot_general(q2, k, NT_DIMS, preferred_element_type=F32)
    e1 = jnp.exp2(s1 - jnp.max(s1, axis=-1, keepdims=True)).astype(BF16)
    e2 = jnp.exp2(s2 - jnp.max(s2, axis=-1, keepdims=True)).astype(BF16)
    r1 = jnp.dot(e1, v1, preferred_element_type=F32)
    r2 = jnp.dot(e2, v1, preferred_element_type=F32)
    lam = (jnp.exp(jnp.sum(lq1[...] * lk1[...], axis=-1, keepdims=True))
           - jnp.exp(jnp.sum(lq2[...] * lk2[...], axis=-1, keepdims=True)) + lam_init)
    o = r1[:, :LANE] * (1.0 / r1[:, LANE:LANE + 1]) - r2[:, :LANE] * (lam / r2[:, LANE:LANE + 1])
    o_ref[0] = (_rms(o, sg[...]) * (1.0 - lam_init)).astype(BF16)


def _diff_call(Pq, kv_sources, rope_tabs, lq1, lk1, lq2, lk2, subln_g, lam_init, l):
    B, T, _ = Pq.shape
    tq = min(T, 256)
    rope = rope_tabs is not None
    in_specs = [pl.BlockSpec((1, tq, LANE), lambda b, h, i: (b, i, COL_DQ + h))]
    args = [Pq]
    Tk = 0
    for Ps in kv_sources:
        n = Ps.shape[1]
        in_specs.append(pl.BlockSpec((1, n, LANE), lambda b, h, i: (b, 0, COL_DK + h)))
        in_specs.append(pl.BlockSpec((1, n, LANE), lambda b, h, i: (b, 0, COL_DV + h)))
        args += [Ps, Ps]
        Tk += n
    if rope:
        cos, sin = rope_tabs
        in_specs += [pl.BlockSpec((tq, LANE), lambda b, h, i: (i, 0)), pl.BlockSpec((tq, LANE), lambda b, h, i: (i, 0)),
                     pl.BlockSpec((T, LANE), lambda b, h, i: (0, 0)), pl.BlockSpec((T, LANE), lambda b, h, i: (0, 0))]
        args += [cos, sin, cos, sin]
    small = _layer_spec(l, 1, DIFF_QK_DIM)
    in_specs += [small, small, small, small, _layer_spec(l, 1, LANE)]
    args += [lq1, lk1, lq2, lk2, subln_g]
    return pl.pallas_call(
        functools.partial(_diff_kernel, rope=rope, lam_init=lam_init, n_kv=len(kv_sources)),
        out_shape=jax.ShapeDtypeStruct((B, T, GROUP_W), BF16),
        grid=(B, N_HEADS, T // tq),
        in_specs=in_specs,
        out_specs=pl.BlockSpec((1, tq, LANE), lambda b, h, i: (b, i, h)),
        scratch_shapes=[pltpu.VMEM((Tk, LANE), BF16), pltpu.VMEM((Tk, 2 * LANE), BF16)],
        compiler_params=_cparams(3),
        name="diff_attn",
    )(*args)


def _outproj_kernel(a_ref, b_ref, c_ref, d_ref, w_ref, x_ref, gate_ref, sh_ref, sc_ref, ng_ref, wr_ref,
                    x1_ref, h2_ref, aff_ref, mix_ref):
    W = GROUP_W
    tm = x_ref.shape[1]
    rs = min(tm, 256)
    for p, m_ref in enumerate((a_ref, b_ref, c_ref, d_ref)):
        mix_ref[:, p * W:(p + 1) * W] = m_ref[0]
    ys = [jnp.dot(mix_ref[r0:r0 + rs, :], w_ref[...], preferred_element_type=F32) for r0 in range(0, tm, rs)]
    for r0, y in zip(range(0, tm, rs), ys):
        rows = slice(r0, r0 + rs)
        x1 = x_ref[0, rows, :] + gate_ref[...] * y
        x1_ref[0, rows, :] = x1
        h = _norm_mod(x1, ng_ref[...], sc_ref[...], sh_ref[...])
        hh = h.astype(BF16)
        hl = (h - hh.astype(F32)).astype(BF16)
        h2_ref[0, rows, :] = hh
        lg2 = jnp.dot(hh, wr_ref[...], preferred_element_type=F32)
        lg = lg2[:, :LANE] + lg2[:, LANE:] + jnp.dot(hl, wr_ref[:, 0:LANE], preferred_element_type=F32)
        lane = lax.broadcasted_iota(jnp.int32, lg.shape, 1)
        lg = jnp.where(lane < N_EXPERTS, lg, -jnp.inf)
        e = jnp.exp(lg - jnp.max(lg, axis=-1, keepdims=True))
        aff_ref[0, rows, :] = e / jnp.sum(e, axis=-1, keepdims=True)


def _outproj_call(mix, w_out, x, mod, row_of, ng, wr2, l):
    Bx, T, D = x.shape
    tm = min(T, 512)
    mixspec = pl.BlockSpec((1, tm, GROUP_W), lambda b, i: (b, i, 0))
    rowspec = pl.BlockSpec((1, tm, D), lambda b, i: (b, i, 0))
    return pl.pallas_call(
        _outproj_kernel,
        out_shape=(jax.ShapeDtypeStruct((Bx, T, D), F32), jax.ShapeDtypeStruct((Bx, T, D), BF16),
                   jax.ShapeDtypeStruct((Bx, T, LANE), F32)),
        grid=(Bx, T // tm),
        in_specs=[mixspec, mixspec, mixspec, mixspec,
                  _layer_spec(l, D, D),
                  rowspec, _mod_spec(l, row_of, 2), _mod_spec(l, row_of, 3), _mod_spec(l, row_of, 4),
                  _layer_spec(l, 1, D), _layer_spec(l, D, 2 * LANE)],
        out_specs=(rowspec, rowspec, pl.BlockSpec((1, tm, LANE), lambda b, i: (b, i, 0))),
        scratch_shapes=[pltpu.VMEM((tm, D), BF16)],
        compiler_params=_cparams(2),
        name="out_proj",
    )(*mix, w_out, x, mod, mod, mod, ng, wr2)


def _excl_prefix(x):
    rows, T = x.shape
    ri = lax.broadcasted_iota(jnp.int32, (LANE, LANE), 0)
    ci = lax.broadcasted_iota(jnp.int32, (LANE, LANE), 1)
    upper = jnp.where(ri <= ci, 1.0, 0.0).astype(BF16)
    carry = jnp.zeros((rows, 1), F32)
    out = []
    for b in range(T // LANE):
        xb = x[:, b * LANE:(b + 1) * LANE]
        inc = jnp.dot(xb.astype(BF16), upper, preferred_element_type=F32)
        out.append(inc - xb + carry)
        carry = carry + jnp.sum(xb, axis=1, keepdims=True)
    return jnp.concatenate(out, axis=1)


def _route_kernel(aff_ref, slot_row_ref, slot_col_ref, aff_row_ref, st_ref, *, cap):
    T = aff_ref.shape[1]
    E = N_EXPERTS
    arow = aff_ref[0].T[0:E, :]
    keys = lax.bitcast_convert_type(arow, jnp.int32)
    v = jnp.zeros((E, 1), jnp.int32)
    for bit in range(30, -1, -1):
        cand = v | (1 << bit)
        cnt = jnp.sum(jnp.where(keys >= cand, 1.0, 0.0), axis=1, keepdims=True)
        v = jnp.where(cnt >= cap, cand, v)
    above = keys > v
    tied = jnp.where(keys == v, 1.0, 0.0)
    room = cap - jnp.sum(jnp.where(above, 1.0, 0.0), axis=1, keepdims=True)
    kept = jnp.where(above, 1.0, jnp.where(_excl_prefix(tied) < room, tied, 0.0))
    slot = jnp.where(kept > 0.0, _excl_prefix(kept), float(T))
    for e in range(E):
        slot_row_ref[0, e] = slot[e:e + 1, :].astype(jnp.int32)
        aff_row_ref[0, e] = arow[e:e + 1, :]
    st_ref[...] = jnp.full(st_ref.shape, float(T), F32)
    st_ref[0:E, :] = slot
    slot_col_ref[0] = st_ref[...].T.astype(jnp.int32)


def _route_call(aff, cap):
    B, T, _ = aff.shape
    E = N_EXPERTS
    return pl.pallas_call(
        functools.partial(_route_kernel, cap=cap),
        out_shape=(jax.ShapeDtypeStruct((B, E, 1, T), jnp.int32),
                   jax.ShapeDtypeStruct((B, T, LANE), jnp.int32),
                   jax.ShapeDtypeStruct((B, E, 1, T), F32)),
        grid=(B,),
        in_specs=[pl.BlockSpec((1, T, LANE), lambda b: (b, 0, 0))],
        out_specs=(pl.BlockSpec((1, E, 1, T), lambda b: (b, 0, 0, 0)),
                   pl.BlockSpec((1, T, LANE), lambda b: (b, 0, 0)),
                   pl.BlockSpec((1, E, 1, T), lambda b: (b, 0, 0, 0))),
        scratch_shapes=[pltpu.VMEM((LANE, T), F32)],
        compiler_params=_cparams(1),
        name="router_route",
    )(aff)


def _dispatch_kernel(slot_ref, affr_ref, h_ref, xe_ref, gs_ref):
    eg, cap, D = xe_ref.shape
    T = h_ref.shape[1]
    slot = lax.broadcasted_iota(jnp.int32, (cap, T), 0)
    hits = [slot_ref[0, e] == slot for e in range(eg)]
    onehot = jnp.concatenate([jnp.where(h, 1.0, 0.0).astype(BF16) for h in hits], axis=0)
    x = jnp.dot(onehot, h_ref[0], preferred_element_type=F32).astype(BF16)
    xe_ref[...] = x.reshape(eg, cap, D)
    for e in range(eg):
        g = jnp.sum(jnp.where(hits[e], affr_ref[0, e], 0.0), axis=-1, keepdims=True)
        gs_ref[e] = jnp.broadcast_to(g, (cap, LANE))


def _dispatch_call(slot_row, aff_row, h2, cap):
    B, T, D = h2.shape
    E = N_EXPERTS
    eg = max(1, min(E, 512 // cap))
    return pl.pallas_call(
        _dispatch_kernel,
        out_shape=(jax.ShapeDtypeStruct((E, B * cap, D), BF16), jax.ShapeDtypeStruct((E, B * cap, LANE), F32)),
        grid=(B, E // eg),
        in_specs=[pl.BlockSpec((1, eg, 1, T), lambda b, g: (b, g, 0, 0)),
                  pl.BlockSpec((1, eg, 1, T), lambda b, g: (b, g, 0, 0)),
                  pl.BlockSpec((1, T, D), lambda b, g: (b, 0, 0))],
        out_specs=(pl.BlockSpec((eg, cap, D), lambda b, g: (g, b, 0)),
                   pl.BlockSpec((eg, cap, LANE), lambda b, g: (g, b, 0))),
        compiler_params=_cparams(2),
        name="moe_dispatch",
    )(slot_row, aff_row, h2)


def _expert_kernel(*refs, n_groups, nf):
    xs = refs[:n_groups]
    wg_ref, wu_ref, wd_ref = refs[n_groups:n_groups + 3]
    gss = refs[n_groups + 3:2 * n_groups + 3]
    ys = refs[2 * n_groups + 3:3 * n_groups + 3]
    hms = refs[3 * n_groups + 3:]
    s = pl.program_id(1)
    tf = wg_ref.shape[-1]

    @pl.when(s < nf)
    def _():
        wg = wg_ref[...].astype(BF16)
        wu = wu_ref[...].astype(BF16)
        for x_ref, hm_ref in zip(xs, hms):
            x = x_ref[...]
            a = jnp.dot(x, wg, preferred_element_type=F32)
            u = jnp.dot(x, wu, preferred_element_type=F32)
            hm_ref[s] = (_silu(a) * u).astype(BF16)

    @pl.when(s >= nf)
    def _():
        for hm_ref, gs_ref, y_ref in zip(hms, gss, ys):
            acc = jnp.dot(hm_ref[0], wd_ref[0:tf, :].astype(BF16), preferred_element_type=F32)
            for f in range(1, nf):
                acc = acc + jnp.dot(hm_ref[f], wd_ref[f * tf:(f + 1) * tf, :].astype(BF16),
                                    preferred_element_type=F32)
            y_ref[...] = (acc * gs_ref[...][:, 0:1]).astype(BF16)


def _expert_call(groups, w_gate, w_up, w_down, l):
    E, _, D = groups[0][0].shape
    F = w_gate.shape[-1]
    tf = 256
    tn = 512
    nf = F // tf
    nn = D // tn
    n = len(groups)
    Ms = [g[0].shape[1] for g in groups]
    nidx = lambda s: jnp.maximum(s - nf, 0)
    x_e = lambda e, s: jnp.minimum(e + (s >= nf).astype(jnp.int32), E - 1)
    w_e = lambda e, s: jnp.minimum(e + (s > nf).astype(jnp.int32), E - 1)
    w_f = lambda s: jnp.where(s > nf, 0, jnp.minimum(s, nf - 1))
    in_specs = ([pl.BlockSpec((None, M, D), lambda e, s: (x_e(e, s), 0, 0)) for M in Ms]
                + [pl.BlockSpec((None, None, D, tf), lambda e, s: (l, w_e(e, s), 0, w_f(s))),
                   pl.BlockSpec((None, None, D, tf), lambda e, s: (l, w_e(e, s), 0, w_f(s))),
                   pl.BlockSpec((None, None, F, tn), lambda e, s: (l, e, 0, nidx(s)))]
                + [pl.BlockSpec((None, M, LANE), lambda e, s: (e, 0, 0)) for M in Ms])
    return pl.pallas_call(
        functools.partial(_expert_kernel, n_groups=n, nf=nf),
        out_shape=[jax.ShapeDtypeStruct((E, M, D), BF16) for M in Ms],
        grid=(E, nf + nn),
        in_specs=in_specs,
        out_specs=[pl.BlockSpec((None, M, tn), lambda e, s: (e, 0, nidx(s))) for M in Ms],
        scratch_shapes=[pltpu.VMEM((nf, M, tf), BF16) for M in Ms],
        compiler_params=_cparams(2),
        name="moe_experts",
    )(*[g[0] for g in groups], w_gate, w_up, w_down, *[g[1] for g in groups])


def _combine_kernel(slot_ref, y_ref, x_ref, gate_ref, fg_ref, o_ref, *, final_norm):
    cap = y_ref.shape[1]
    tm = x_ref.shape[1]
    lane_slot = lax.broadcasted_iota(jnp.int32, (tm, cap), 1)
    slot = slot_ref[0]
    acc = jnp.zeros(x_ref.shape[1:], F32)
    for e in range(N_EXPERTS):
        hit = jnp.where(slot[:, e:e + 1] == lane_slot, 1.0, 0.0).astype(BF16)
        acc = acc + jnp.dot(hit, y_ref[e], preferred_element_type=F32)
    x2 = x_ref[0] + gate_ref[...] * acc
    if final_norm:
        x2 = _rms(x2, fg_ref[...])
    o_ref[0] = x2


def _combine_call(slot_col, y, x1, mod, row_of, final_g, cap, final_norm, l):
    B, T, D = x1.shape
    tm = min(T, 512)
    return pl.pallas_call(
        functools.partial(_combine_kernel, final_norm=final_norm),
        out_shape=jax.ShapeDtypeStruct((B, T, D), F32),
        grid=(B, T // tm),
        in_specs=[pl.BlockSpec((1, tm, LANE), lambda b, i: (b, i, 0)),
                  pl.BlockSpec((N_EXPERTS, cap, D), lambda b, i: (0, b, 0)),
                  pl.BlockSpec((1, tm, D), lambda b, i: (b, i, 0)),
                  _mod_spec(l, row_of, 5),
                  pl.BlockSpec((1, D), lambda b, i: (0, 0))],
        out_specs=pl.BlockSpec((1, tm, D), lambda b, i: (b, i, 0)),
        compiler_params=_cparams(2),
        name="moe_combine",
    )(slot_col, y, x1, mod, final_g)


def _rope_tables(n):
    rows = n // GRID_W
    row = jnp.repeat(jnp.arange(rows), GRID_W).astype(F32)
    col = jnp.tile(jnp.arange(GRID_W), rows).astype(F32)
    inv_freq = ROPE_BASE ** (-jnp.arange(0, ROPE_AXIS_DIM, 2, dtype=F32) / ROPE_AXIS_DIM)
    ar = row[:, None] * inv_freq
    ac = col[:, None] * inv_freq
    cos = jnp.concatenate([jnp.cos(ar), jnp.cos(ar), jnp.cos(ac), jnp.cos(ac)], axis=-1)
    sin = jnp.concatenate([-jnp.sin(ar), jnp.sin(ar), -jnp.sin(ac), jnp.sin(ac)], axis=-1)
    return jnp.tile(cos, (1, 2)), jnp.tile(sin, (1, 2))


def _reorder_kernel(w_ref, o_ref):
    G = GROUP_W
    low = 2 * GLA_GATE_RANK
    o = 0
    for src, n in ((5 * G + low, 2 * G), (0, 5 * G), (7 * G + low, 3 * G), (5 * G, low)):
        o_ref[:, o:o + n] = w_ref[:, src:src + n].astype(BF16)
        o += n
    o_ref[:, o:] = jnp.zeros((o_ref.shape[0], o_ref.shape[1] - o), BF16)


def _reorder_w_in(w_in):
    L, D, W = w_in.shape
    tr = 256
    return pl.pallas_call(
        _reorder_kernel,
        out_shape=jax.ShapeDtypeStruct((L, D, IN_COLS), BF16),
        grid=(L, D // tr),
        in_specs=[pl.BlockSpec((None, tr, W), lambda l, i: (l, i, 0))],
        out_specs=pl.BlockSpec((None, tr, IN_COLS), lambda l, i: (l, i, 0)),
        compiler_params=_cparams(2),
        name="w_in_reorder",
    )(w_in)


def kernel(x, c, ctx, c_ctx, w_ada, b_ada, norm1_g, norm2_g, w_in, pool_w, pool_scale, gla_gk_up_f, gla_gk_bias_f, gla_gk_up_b, gla_gk_bias_b, gla_norm_g, conv_dw, conv_dw_b, conv_ln_g, conv_ln_b, conv_pw, conv_pw_b, diff_lq1, diff_lk1, diff_lq2, diff_lk2, diff_subln_g, w_out, w_router, w_exp_gate, w_exp_up, w_exp_down, final_norm_g):
    B, N, D = x.shape
    Tc = ctx.shape[1]
    L = w_ada.shape[0]
    assert D == D_MODEL and N % 256 == 0 and Tc % 128 == 0 and B < MOD_ROWS

    cc = jnp.concatenate([c, c_ctx[None, :], jnp.zeros((MOD_ROWS - B - 1, D), F32)], axis=0)
    mod = _ada_call(cc, w_ada, b_ada).reshape(L, MOD_ROWS, 6, 1, D)
    lat_row = lambda b, *_: b
    ctx_row = lambda *_: B

    rows3 = lambda a: a.reshape(L, 1, -1)
    w_in_r = _reorder_w_in(w_in)
    w_out_b = w_out.astype(BF16)
    pool_w_b = pool_w.astype(BF16)
    conv_pw_b16 = conv_pw.astype(BF16)
    R = GLA_GATE_RANK
    zpad = lambda a, lo: jnp.pad(a, ((0, 0), (lo, LANE - R - lo), (0, 0))).astype(BF16)
    upf = zpad(gla_gk_up_f, 0)
    upb = zpad(gla_gk_up_b, R)
    wr = jnp.pad(w_router, ((0, 0), (0, 0), (0, LANE - N_EXPERTS)))
    wr_hi = wr.astype(BF16)
    wr2 = jnp.concatenate([wr_hi, (wr - wr_hi.astype(F32)).astype(BF16)], axis=-1)
    rope_tabs = _rope_tables(N)
    fg = final_norm_g.reshape(1, D)
    n1, n2 = rows3(norm1_g), rows3(norm2_g)
    gla_args = (upf, rows3(gla_gk_bias_f), upb, rows3(gla_gk_bias_b), rows3(gla_norm_g))
    conv_args = (conv_dw, rows3(conv_dw_b), rows3(conv_ln_g), rows3(conv_ln_b), conv_pw_b16, rows3(conv_pw_b))
    diff_vecs = (rows3(diff_lq1), rows3(diff_lk1), rows3(diff_lq2), rows3(diff_lk2), rows3(diff_subln_g))
    pool_args = (pool_w_b, rows3(pool_scale))
    cap_l = EC_CAPACITY * N // N_EXPERTS
    cap_c = EC_CAPACITY * Tc // N_EXPERTS
    flat = lambda a: a.reshape(1, B * Tc, a.shape[-1])
    unflat = lambda a: a.reshape(B, Tc, a.shape[-1])

    for l in range(L):
        last = l == L - 1
        lam_init = 0.8 - 0.6 * math.exp(-0.3 * l)

        Pl = _inproj_call(x, mod, lat_row, n1, w_in_r, l)
        Pc = unflat(_inproj_call(flat(ctx), mod, ctx_row, n1, w_in_r, l))

        gla_l, gla_c = _gla_call(Pl, Pc, *gla_args, l)
        pool_l = _pool_call(Pl, *pool_args, l)
        conv_l = _conv_call(Pl, *conv_args, l)
        diff_l = _diff_call(Pl, [Pl, Pc], rope_tabs, *diff_vecs, lam_init, l)
        x1, h2, aff = _outproj_call((pool_l, gla_l, conv_l, diff_l), w_out_b, x, mod, lat_row, n2, wr2, l)
        slot_row, slot_col, aff_row = _route_call(aff, cap_l)
        groups = [_dispatch_call(slot_row, aff_row, h2, cap_l)]

        if not last:
            pool_c = _pool_call(Pc, *pool_args, l)
            conv_c = _conv_call(Pc, *conv_args, l)
            diff_c = _diff_call(Pc, [Pc], None, *diff_vecs, lam_init, l)
            c1, h2c, affc = _outproj_call((flat(pool_c), flat(gla_c), flat(conv_c), flat(diff_c)), w_out_b,
                                          flat(ctx), mod, ctx_row, n2, wr2, l)
            slot_row_c, slot_col_c, aff_row_c = _route_call(unflat(affc), cap_c)
            groups.append(_dispatch_call(slot_row_c, aff_row_c, unflat(h2c), cap_c))

        ys = _expert_call(groups, w_exp_gate, w_exp_up, w_exp_down, l)
        x = _combine_call(slot_col, ys[0], x1, mod, lat_row, fg, cap_l, last, l)
        if not last:
            ctx = _combine_call(slot_col_c, ys[1], unflat(c1), mod, ctx_row, fg, cap_c, False, l)

    return x
```

```python
import functools
import math

import jax
import jax.numpy as jnp
from jax import lax
from jax.experimental import pallas as pl
from jax.experimental.pallas import tpu as pltpu

F32 = jnp.float32
BF16 = jnp.bfloat16

D_MODEL = 2048
GRID_W = 64
GROUP_W = D_MODEL // 4
POOL_WINDOWS = (2, 4, 8, 16)
HEAD_DIM = 128
N_HEADS = GROUP_W // HEAD_DIM
GLA_GATE_RANK = 16
GLA_GATE_NORM = 16.0
GLA_CHUNK = 64
CONV_K = 31
DIFF_QK_DIM = 64
ROPE_BASE = 10000.0
ROPE_AXIS_DIM = DIFF_QK_DIM // 2
N_EXPERTS = 16
EXPERT_FF = D_MODEL // 2
EC_CAPACITY = 2
NORM_EPS = 1e-6

LANE = 128
SUBLANE = 8
HALO = 16
VMEM_LIMIT = 56 * 1024 * 1024
MOD_ROWS = 16

COL_CONV = 0
COL_POOL = 8
COL_GQ = 12
COL_GK = 16
COL_GV = 20
COL_GGATE = 24
COL_DQ = 28
COL_DK = 32
COL_DV = 36
COL_LOW = 40
MXU_N = 256
IN_COLS = 42 * LANE
IN_TN = 7 * MXU_N

NT_DIMS = (((1,), (1,)), ((), ()))
TN_DIMS = (((0,), (0,)), ((), ()))


def _cparams(n_axes):
    return pltpu.CompilerParams(dimension_semantics=("arbitrary",) * n_axes, vmem_limit_bytes=VMEM_LIMIT)


def _silu(x):
    return x * jax.nn.sigmoid(x)


def _rms(x, g):
    return x * lax.rsqrt(jnp.mean(x * x, axis=-1, keepdims=True) + NORM_EPS) * g


def _norm_mod(x, g, scale, shift):
    inv = lax.rsqrt(jnp.mean(x * x, axis=-1, keepdims=True) + NORM_EPS)
    return x * inv * (g * (1.0 + scale)) + shift


def _layer_spec(l, *tail):
    return pl.BlockSpec((None,) + tail, lambda *_: (l,) + (0,) * len(tail))


def _mod_spec(l, row_of, k):
    return pl.BlockSpec((None, None, None, 1, D_MODEL), lambda *g: (l, row_of(*g), k, 0, 0))


def _ada_kernel(c_ref, w_ref, b_ref, o_ref):
    sc = _silu(c_ref[...]).astype(BF16)
    o_ref[0] = jnp.dot(sc, w_ref[0].astype(BF16), preferred_element_type=F32) + b_ref[0]


def _ada_call(cc, w_ada, b_ada):
    L, D, W = w_ada.shape
    R = cc.shape[0]
    tn = 1536
    return pl.pallas_call(
        _ada_kernel,
        out_shape=jax.ShapeDtypeStruct((L, R, W), F32),
        grid=(L, W // tn),
        in_specs=[
            pl.BlockSpec((R, D), lambda l, j: (0, 0)),
            pl.BlockSpec((1, D, tn), lambda l, j: (l, 0, j)),
            pl.BlockSpec((1, 1, tn), lambda l, j: (l, 0, j)),
        ],
        out_specs=pl.BlockSpec((1, R, tn), lambda l, j: (l, 0, j)),
        compiler_params=_cparams(2),
        name="ada_mod",
    )(cc, w_ada, b_ada.reshape(L, 1, W))


def _inproj_kernel(x_ref, sh_ref, sc_ref, g_ref, w_ref, o_ref, h_ref):
    @pl.when(pl.program_id(2) == 0)
    def _():
        h_ref[...] = _norm_mod(x_ref[0], g_ref[...], sc_ref[...], sh_ref[...]).astype(BF16)

    o_ref[0] = jnp.dot(h_ref[...], w_ref[...], preferred_element_type=F32).astype(BF16)


def _inproj_call(x, mod, row_of, g, w, l):
    Bx, T, D = x.shape
    tm = min(T, 1024)
    return pl.pallas_call(
        _inproj_kernel,
        out_shape=jax.ShapeDtypeStruct((Bx, T, IN_COLS), BF16),
        grid=(Bx, T // tm, IN_COLS // IN_TN),
        in_specs=[
            pl.BlockSpec((1, tm, D), lambda b, i, j: (b, i, 0)),
            _mod_spec(l, row_of, 0),
            _mod_spec(l, row_of, 1),
            _layer_spec(l, 1, D),
            pl.BlockSpec((None, D, IN_TN), lambda b, i, j: (l, 0, j)),
        ],
        out_specs=pl.BlockSpec((1, tm, IN_TN), lambda b, i, j: (b, i, j)),
        scratch_shapes=[pltpu.VMEM((tm, D), BF16)],
        compiler_params=_cparams(3),
        name="in_proj",
    )(x, mod, mod, g, w)


def _halo_specs(T, tt, width, col_block):
    r = tt // HALO
    nh = T // HALO
    return [
        pl.BlockSpec((1, HALO, width), lambda b, c: (b, jnp.maximum(c * r - 1, 0), col_block)),
        pl.BlockSpec((1, tt, width), lambda b, c: (b, c, col_block)),
        pl.BlockSpec((1, HALO, width), lambda b, c: (b, jnp.minimum((c + 1) * r, nh - 1), col_block)),
    ]


def _pool_kernel(prev_ref, cur_ref, next_ref, w_ref, s_ref, o_ref, pad_ref, *, T, tt):
    c = pl.program_id(1)
    H = HALO
    pad_ref[0:H, :] = jnp.where(c > 0, prev_ref[0].astype(F32), 0.0)
    pad_ref[H:H + tt, :] = cur_ref[0].astype(F32)
    pad_ref[H + tt:H + tt + H, :] = jnp.where(c < pl.num_programs(1) - 1, next_ref[0].astype(F32), 0.0)
    t = c * tt + lax.broadcasted_iota(jnp.int32, (tt, HEAD_DIM), 0)
    for gi, w in enumerate(POOL_WINDOWS):
        cols = slice(gi * HEAD_DIM, (gi + 1) * HEAD_DIM)
        acc = pad_ref[H - w // 2:H - w // 2 + tt, cols]
        for d in range(-w // 2 + 1, w // 2):
            acc = acc + pad_ref[H + d:H + d + tt, cols]
        cnt = jnp.minimum(t + w // 2, T) - jnp.maximum(t - w // 2, 0)
        p = acc / cnt.astype(F32) - pad_ref[H:H + tt, cols]
        y = jnp.dot(p.astype(BF16), w_ref[gi], preferred_element_type=F32)
        o_ref[0, :, cols] = (y * s_ref[:, cols]).astype(BF16)


def _pool_call(P, pool_w, pool_scale, l):
    B, T, _ = P.shape
    tt = min(T, 256)
    return pl.pallas_call(
        functools.partial(_pool_kernel, T=T, tt=tt),
        out_shape=jax.ShapeDtypeStruct((B, T, GROUP_W), BF16),
        grid=(B, T // tt),
        in_specs=_halo_specs(T, tt, GROUP_W, COL_POOL * LANE // GROUP_W) + [
            _layer_spec(l, len(POOL_WINDOWS), HEAD_DIM, HEAD_DIM),
            _layer_spec(l, 1, GROUP_W),
        ],
        out_specs=pl.BlockSpec((1, tt, GROUP_W), lambda b, c: (b, c, 0)),
        scratch_shapes=[pltpu.VMEM((tt + 2 * HALO, GROUP_W), F32)],
        compiler_params=_cparams(2),
        name="pool_mixer",
    )(P, P, P, pool_w, pool_scale)


def _log_sigmoid(z):
    return jnp.minimum(z, 0.0) - jnp.log(1.0 + jnp.exp(-jnp.abs(z)))


def _gla_block_consts(rb):
    ri = lax.broadcasted_iota(jnp.int32, (rb, rb), 0)
    ci = lax.broadcasted_iota(jnp.int32, (rb, rb), 1)
    shift = GLA_CHUNK.bit_length() - 1
    same = jnp.right_shift(ri, shift) == jnp.right_shift(ci, shift)
    out = []
    for causal in (ci <= ri, ci >= ri):
        tri = jnp.where(causal, jnp.where(same, 1.0, 0.0), 0.0)
        out.append((tri.astype(BF16), tri > 0.0))
    return out


def _gla_prepare(items, consts, q_ref, k_ref, v_ref, g_ref, qg_ref, u_ref, dec_ref, o_ref):
    C = GLA_CHUNK
    rb = consts[0][1].shape[0]
    nchunk = rb // C
    Gs = []
    for d, rows, _ in items:
        tri = consts[d][0]
        g = g_ref[d, rows, :]
        g_hi = g.astype(BF16)
        g_lo = (g - g_hi.astype(F32)).astype(BF16)
        Gs.append(jnp.dot(tri, g_hi, preferred_element_type=F32) + jnp.dot(tri, g_lo, preferred_element_type=F32))
    staged = []
    for (d, rows, c0), G in zip(items, Gs):
        tot = C - 1 if d == 0 else 0
        tots = [G[ci * C + tot:ci * C + tot + 1, :] for ci in range(nchunk)]
        Gt = jnp.concatenate([jnp.broadcast_to(t, (C, LANE)) for t in tots], axis=0)
        r = 0.5 * Gt
        q = q_ref[0, rows, :].astype(F32) * (HEAD_DIM ** -0.5)
        k = k_ref[0, rows, :].astype(F32)
        qg = (q * jnp.exp(G - r)).astype(BF16)
        kg = (k * jnp.exp(r - G)).astype(BF16)
        qg_ref[d, rows, :] = (q * jnp.exp(G)).astype(BF16)
        kd = (k * jnp.exp(Gt - G)).astype(BF16)
        for ci in range(nchunk):
            dec_ref[d, c0 + ci] = jnp.exp(jnp.broadcast_to(tots[ci], (8, LANE)))
        staged.append((qg, kg, kd))
    atts = [lax.dot_general(qg, kg, NT_DIMS, preferred_element_type=F32) for qg, kg, _ in staged]
    for (d, rows, c0), att, (_, _, kd) in zip(items, atts, staged):
        v = v_ref[0, rows, :]
        att = jnp.where(consts[d][1], att, 0.0).astype(BF16)
        o_ref[d, rows, :] = jnp.dot(att, v, preferred_element_type=F32)
        for ci in range(nchunk):
            cr = slice(ci * C, (ci + 1) * C)
            u_ref[d, c0 + ci] = lax.dot_general(v[cr], kd[cr], TN_DIMS, preferred_element_type=F32)


def _gla_kernel(ql, kl, vl, gtl, lowl, qc, kc, vc, gtc, lowc, upf, bf, upb, bb, ng,
                ol_ref, oc_ref, gl, qgl, ul, decl, sbl, osl, gc, qgc, uc, decc, sbc, osc):
    C = GLA_CHUNK
    T = ql.shape[1]
    Tc = qc.shape[1]

    def gates(low_ref, g_ref):
        low = low_ref[0]
        zf = jnp.dot(low, upf[...], preferred_element_type=F32) + bf[...]
        zb = jnp.dot(low, upb[...], preferred_element_type=F32) + bb[...]
        g_ref[0] = _log_sigmoid(zf) * (1.0 / GLA_GATE_NORM)
        g_ref[1] = _log_sigmoid(zb) * (1.0 / GLA_GATE_NORM)

    def prepare(q_ref, k_ref, v_ref, g_ref, qg_ref, u_ref, dec_ref, o_ref, Tx):
        rb = min(Tx, 256)
        nb = Tx // rb
        per = 4 if nb % 4 == 0 else (2 if nb % 2 == 0 else 1)
        consts = _gla_block_consts(rb)

        def body(j, carry):
            items = []
            for p in range(per):
                blk = j * per + p
                rows = pl.ds(pl.multiple_of(blk * rb, rb), rb)
                items += [(d, rows, blk * (rb // C)) for d in (0, 1)]
            _gla_prepare(items, consts, q_ref, k_ref, v_ref, g_ref, qg_ref, u_ref, dec_ref, o_ref)
            return carry

        lax.fori_loop(0, nb // per, body, 0)

    def recur(u_ref, dec_ref, sb_ref, n, Sf, Sb):
        def step(d, c, S):
            sb_ref[d, c] = S.astype(BF16)
            return S * dec_ref[d, c, 0:1, :] + u_ref[d, c]

        def body(i, carry):
            Sf, Sb = carry
            return step(0, i, Sf), step(1, n - 1 - i, Sb)

        return lax.fori_loop(0, n, body, (Sf, Sb), unroll=min(n, 4))

    def inter(qg_ref, sb_ref, o_ref, n):
        grp = min(n, 4)
        per = 2 if (n // grp) % 2 == 0 else 1

        def body(j, carry):
            work = []
            for p in range(per):
                blk = j * per + p
                rows = pl.ds(pl.multiple_of(blk * grp * C, grp * C), grp * C)
                for d in (0, 1):
                    qg = qg_ref[d, rows, :]
                    parts = [lax.dot_general(qg[ci * C:(ci + 1) * C], sb_ref[d, blk * grp + ci], NT_DIMS,
                                             preferred_element_type=F32) for ci in range(grp)]
                    work.append((d, rows, parts))
            for d, rows, parts in work:
                o_ref[d, rows, :] += jnp.concatenate(parts, axis=0)
            return carry

        lax.fori_loop(0, n // (grp * per), body, 0)

    def finish(o_s, gt_ref, o_ref):
        o = _rms(o_s[0] + o_s[1], ng[...])
        o_ref[0] = (o * _silu(gt_ref[0].astype(F32))).astype(BF16)

    gates(lowc, gc)
    gates(lowl, gl)
    prepare(qc, kc, vc, gc, qgc, uc, decc, osc, Tc)
    prepare(ql, kl, vl, gl, qgl, ul, decl, osl, T)
    S0 = jnp.zeros((HEAD_DIM, HEAD_DIM), F32)
    Sf, Sb = recur(uc, decc, sbc, Tc // C, S0, S0)
    recur(ul, decl, sbl, T // C, Sf, Sb)
    inter(qgc, sbc, osc, Tc // C)
    inter(qgl, sbl, osl, T // C)
    finish(osl, gtl, ol_ref)
    finish(osc, gtc, oc_ref)


def _gla_call(Pl, Pc, upf, bias_f, upb, bias_b, norm_g, l):
    B, T, _ = Pl.shape
    Tc = Pc.shape[1]

    def colspec(Tx, col):
        return pl.BlockSpec((1, Tx, LANE), lambda b, h: (b, 0, col + h))

    def lowspec(Tx):
        return pl.BlockSpec((1, Tx, LANE), lambda b, h: (b, 0, COL_LOW))

    headw = pl.BlockSpec((None, LANE, LANE), lambda b, h: (l, 0, h))
    headv = pl.BlockSpec((None, 1, LANE), lambda b, h: (l, 0, h))
    outspec = lambda Tx: pl.BlockSpec((1, Tx, LANE), lambda b, h: (b, 0, h))

    def scratch(Tx):
        n = Tx // GLA_CHUNK
        return [pltpu.VMEM((2, Tx, LANE), F32),
                pltpu.VMEM((2, Tx, LANE), BF16),
                pltpu.VMEM((2, n, HEAD_DIM, HEAD_DIM), F32),
                pltpu.VMEM((2, n, 8, LANE), F32),
                pltpu.VMEM((2, n, HEAD_DIM, HEAD_DIM), BF16),
                pltpu.VMEM((2, Tx, LANE), F32)]

    return pl.pallas_call(
        _gla_kernel,
        out_shape=(jax.ShapeDtypeStruct((B, T, GROUP_W), BF16), jax.ShapeDtypeStruct((B, Tc, GROUP_W), BF16)),
        grid=(B, N_HEADS),
        in_specs=[colspec(T, COL_GQ), colspec(T, COL_GK), colspec(T, COL_GV), colspec(T, COL_GGATE), lowspec(T),
                  colspec(Tc, COL_GQ), colspec(Tc, COL_GK), colspec(Tc, COL_GV), colspec(Tc, COL_GGATE), lowspec(Tc),
                  headw, headv, headw, headv, _layer_spec(l, 1, LANE)],
        out_specs=(outspec(T), outspec(Tc)),
        scratch_shapes=scratch(T) + scratch(Tc),
        compiler_params=_cparams(2),
        name="gla_mixer",
    )(Pl, Pl, Pl, Pl, Pl, Pc, Pc, Pc, Pc, Pc, upf, bias_f, upb, bias_b, norm_g)


def _conv_kernel(prev_ref, cur_ref, next_ref, dw_ref, dwb_ref, lng_ref, lnb_ref, pw_ref, pwb_ref,
                 o_ref, pad_ref, sh_ref, acc_ref, *, tt):
    c = pl.program_id(1)
    H = HALO
    W = GROUP_W

    def glu(u):
        u = u.astype(F32)
        return u[:, :W] * jax.nn.sigmoid(u[:, W:])

    pad_ref[0:H, :] = jnp.where(c > 0, glu(prev_ref[0]), 0.0)
    pad_ref[H:H + tt, :] = glu(cur_ref[0])
    pad_ref[H + tt:H + tt + H, :] = jnp.where(c < pl.num_programs(1) - 1, glu(next_ref[0]), 0.0)

    n_sh = tt + 2 * H - SUBLANE
    sh_ref[0, :, :] = pad_ref[...]
    for r in range(1, SUBLANE):
        sh_ref[r, 0:n_sh, :] = pad_ref[r:r + n_sh, :]

    rs = min(tt, 128)
    off = H - CONV_K // 2
    for r0 in range(0, tt, rs):
        for lb in range(W // LANE):
            cols = slice(lb * LANE, (lb + 1) * LANE)
            acc = jnp.zeros((rs, LANE), F32) + dwb_ref[:, cols]
            for k in range(CONV_K):
                r = (off + k) % SUBLANE
                a0 = r0 + off + k - r
                acc = acc + sh_ref[r, a0:a0 + rs, cols] * dw_ref[k:k + 1, cols]
            acc_ref[r0:r0 + rs, cols] = acc

    h = acc_ref[...]
    mu = jnp.mean(h, axis=-1, keepdims=True)
    hc = h - mu
    var = jnp.mean(hc * hc, axis=-1, keepdims=True)
    y = hc * lax.rsqrt(var + NORM_EPS) * lng_ref[...] + lnb_ref[...]
    y = jnp.dot(_silu(y).astype(BF16), pw_ref[...], preferred_element_type=F32) + pwb_ref[...]
    o_ref[0] = y.astype(BF16)


def _conv_call(P, dw, dw_b, ln_g, ln_b, pw, pw_b, l):
    B, T, _ = P.shape
    tt = min(T, 256)
    vec = _layer_spec(l, 1, GROUP_W)
    return pl.pallas_call(
        functools.partial(_conv_kernel, tt=tt),
        out_shape=jax.ShapeDtypeStruct((B, T, GROUP_W), BF16),
        grid=(B, T // tt),
        in_specs=_halo_specs(T, tt, 2 * GROUP_W, 0) + [
            _layer_spec(l, CONV_K, GROUP_W), vec, vec, vec, _layer_spec(l, GROUP_W, GROUP_W), vec],
        out_specs=pl.BlockSpec((1, tt, GROUP_W), lambda b, c: (b, c, 0)),
        scratch_shapes=[pltpu.VMEM((tt + 2 * HALO, GROUP_W), F32),
                        pltpu.VMEM((SUBLANE, tt + 2 * HALO, GROUP_W), F32),
                        pltpu.VMEM((tt, GROUP_W), F32)],
        compiler_params=_cparams(2),
        name="conv_mixer",
    )(P, P, P, dw, dw_b, ln_g, ln_b, pw, pw_b)


def _rope(x, cos, sin):
    hw = ROPE_AXIS_DIM // 2
    lane = lax.broadcasted_iota(jnp.int32, x.shape, 1)
    first_half = (lane % (2 * hw)) < hw
    swapped = jnp.where(first_half, pltpu.roll(x, LANE - hw, 1), pltpu.roll(x, hw, 1))
    return x * cos + swapped * sin


def _diff_kernel(*refs, rope, lam_init, n_kv):
    it = iter(refs)
    q_ref = next(it)
    kv = [(next(it), next(it)) for _ in range(n_kv)]
    if rope:
        cq, sq, ck, sk = next(it), next(it), next(it), next(it)
    lq1, lk1, lq2, lk2, sg = next(it), next(it), next(it), next(it), next(it)
    o_ref, kbuf, vbuf = next(it), next(it), next(it)

    @pl.when(pl.program_id(2) == 0)
    def _():
        r0 = 0
        for i, (k_ref, v_ref) in enumerate(kv):
            n = k_ref.shape[1]
            k = k_ref[0]
            if rope and i == 0:
                k = _rope(k.astype(F32), ck[...], sk[...]).astype(BF16)
            kbuf[r0:r0 + n, :] = k
            vbuf[r0:r0 + n, 0:LANE] = v_ref[0]
            r0 += n
        vbuf[:, LANE:2 * LANE] = jnp.ones((vbuf.shape[0], LANE), BF16)

    tq = q_ref.shape[1]
    rs = min(tq, 256)
    k = kbuf[...]
    v1 = vbuf[...]
    lam = (jnp.exp(jnp.sum(lq1[...] * lk1[...], axis=-1, keepdims=True))
           - jnp.exp(jnp.sum(lq2[...] * lk2[...], axis=-1, keepdims=True)) + lam_init)
    scores = []
    for r0 in range(0, tq, rs):
        q = q_ref[0, r0:r0 + rs, :].astype(F32)
        if rope:
            q = _rope(q, cq[r0:r0 + rs, :], sq[r0:r0 + rs, :])
        q = q * (DIFF_QK_DIM ** -0.5 * math.log2(math.e))
        lane = lax.broadcasted_iota(jnp.int32, q.shape, 1)
        q1 = jnp.where(lane < DIFF_QK_DIM, q, 0.0).astype(BF16)
        q2 = jnp.where(lane >= DIFF_QK_DIM, q, 0.0).astype(BF16)
        scores.append((lax.dot_general(q1, k, NT_DIMS, preferred_element_type=F32),
                       lax.dot_general(q2, k, NT_DIMS, preferred_element_type=F32)))
    for r0, (s1, s2) in zip(range(0, tq, rs), scores):
        e1 = jnp.exp2(s1 - jnp.max(s1, axis=-1, keepdims=True)).astype(BF16)
        e2 = jnp.exp2(s2 - jnp.max(s2, axis=-1, keepdims=True)).astype(BF16)
        r1 = jnp.dot(e1, v1, preferred_element_type=F32)
        r2 = jnp.dot(e2, v1, preferred_element_type=F32)
        o = r1[:, :LANE] * (1.0 / r1[:, LANE:LANE + 1]) - r2[:, :LANE] * (lam / r2[:, LANE:LANE + 1])
        o_ref[0, r0:r0 + rs, :] = (_rms(o, sg[...]) * (1.0 - lam_init)).astype(BF16)


def _diff_call(Pq, kv_sources, rope_tabs, lq1, lk1, lq2, lk2, subln_g, lam_init, l):
    B, T, _ = Pq.shape
    tq = min(T, 1024)
    rope = rope_tabs is not None
    in_specs = [pl.BlockSpec((1, tq, LANE), lambda b, h, i: (b, i, COL_DQ + h))]
    args = [Pq]
    Tk = 0
    for Ps in kv_sources:
        n = Ps.shape[1]
        in_specs.append(pl.BlockSpec((1, n, LANE), lambda b, h, i: (b, 0, COL_DK + h)))
        in_specs.append(pl.BlockSpec((1, n, LANE), lambda b, h, i: (b, 0, COL_DV + h)))
        args += [Ps, Ps]
        Tk += n
    if rope:
        cos, sin = rope_tabs
        in_specs += [pl.BlockSpec((tq, LANE), lambda b, h, i: (i, 0)), pl.BlockSpec((tq, LANE), lambda b, h, i: (i, 0)),
                     pl.BlockSpec((T, LANE), lambda b, h, i: (0, 0)), pl.BlockSpec((T, LANE), lambda b, h, i: (0, 0))]
        args += [cos, sin, cos, sin]
    small = _layer_spec(l, 1, DIFF_QK_DIM)
    in_specs += [small, small, small, small, _layer_spec(l, 1, LANE)]
    args += [lq1, lk1, lq2, lk2, subln_g]
    return pl.pallas_call(
        functools.partial(_diff_kernel, rope=rope, lam_init=lam_init, n_kv=len(kv_sources)),
        out_shape=jax.ShapeDtypeStruct((B, T, GROUP_W), BF16),
        grid=(B, N_HEADS, T // tq),
        in_specs=in_specs,
        out_specs=pl.BlockSpec((1, tq, LANE), lambda b, h, i: (b, i, h)),
        scratch_shapes=[pltpu.VMEM((Tk, LANE), BF16), pltpu.VMEM((Tk, 2 * LANE), BF16)],
        compiler_params=_cparams(3),
        name="diff_attn",
    )(*args)


def _outproj_kernel(a_ref, b_ref, c_ref, d_ref, w_ref, x_ref, gate_ref, sh_ref, sc_ref, ng_ref, wr_ref,
                    x1_ref, h2_ref, aff_ref, mix_ref):
    W = GROUP_W
    tm = x_ref.shape[1]
    rs = min(tm, 256)
    for p, m_ref in enumerate((a_ref, b_ref, c_ref, d_ref)):
        mix_ref[:, p * W:(p + 1) * W] = m_ref[0]
    ys = [jnp.dot(mix_ref[r0:r0 + rs, :], w_ref[...], preferred_element_type=F32) for r0 in range(0, tm, rs)]
    for r0, y in zip(range(0, tm, rs), ys):
        rows = slice(r0, r0 + rs)
        x1 = x_ref[0, rows, :] + gate_ref[...] * y
        x1_ref[0, rows, :] = x1
        h = _norm_mod(x1, ng_ref[...], sc_ref[...], sh_ref[...])
        hh = h.astype(BF16)
        hl = (h - hh.astype(F32)).astype(BF16)
        h2_ref[0, rows, :] = hh
        lg2 = jnp.dot(hh, wr_ref[...], preferred_element_type=F32)
        lg = lg2[:, :LANE] + lg2[:, LANE:] + jnp.dot(hl, wr_ref[:, 0:LANE], preferred_element_type=F32)
        lane = lax.broadcasted_iota(jnp.int32, lg.shape, 1)
        lg = jnp.where(lane < N_EXPERTS, lg, -jnp.inf)
        e = jnp.exp(lg - jnp.max(lg, axis=-1, keepdims=True))
        aff_ref[0, rows, :] = e / jnp.sum(e, axis=-1, keepdims=True)


def _outproj_call(mix, w_out, x, mod, row_of, ng, wr2, l):
    Bx, T, D = x.shape
    tm = min(T, 512)
    mixspec = pl.BlockSpec((1, tm, GROUP_W), lambda b, i: (b, i, 0))
    rowspec = pl.BlockSpec((1, tm, D), lambda b, i: (b, i, 0))
    return pl.pallas_call(
        _outproj_kernel,
        out_shape=(jax.ShapeDtypeStruct((Bx, T, D), F32), jax.ShapeDtypeStruct((Bx, T, D), BF16),
                   jax.ShapeDtypeStruct((Bx, T, LANE), F32)),
        grid=(Bx, T // tm),
        in_specs=[mixspec, mixspec, mixspec, mixspec,
                  _layer_spec(l, D, D),
                  rowspec, _mod_spec(l, row_of, 2), _mod_spec(l, row_of, 3), _mod_spec(l, row_of, 4),
                  _layer_spec(l, 1, D), _layer_spec(l, D, 2 * LANE)],
        out_specs=(rowspec, rowspec, pl.BlockSpec((1, tm, LANE), lambda b, i: (b, i, 0))),
        scratch_shapes=[pltpu.VMEM((tm, D), BF16)],
        compiler_params=_cparams(2),
        name="out_proj",
    )(*mix, w_out, x, mod, mod, mod, ng, wr2)


def _excl_prefix(x):
    rows, T = x.shape
    ri = lax.broadcasted_iota(jnp.int32, (LANE, LANE), 0)
    ci = lax.broadcasted_iota(jnp.int32, (LANE, LANE), 1)
    upper = jnp.where(ri <= ci, 1.0, 0.0).astype(BF16)
    carry = jnp.zeros((rows, 1), F32)
    out = []
    for b in range(T // LANE):
        xb = x[:, b * LANE:(b + 1) * LANE]
        inc = jnp.dot(xb.astype(BF16), upper, preferred_element_type=F32)
        out.append(inc - xb + carry)
        carry = carry + jnp.sum(xb, axis=1, keepdims=True)
    return jnp.concatenate(out, axis=1)


def _route_kernel(aff_ref, slot_row_ref, slot_col_ref, aff_row_ref, st_ref, *, cap):
    T = aff_ref.shape[1]
    E = N_EXPERTS
    arow = aff_ref[0].T[0:E, :]
    keys = lax.bitcast_convert_type(arow, jnp.int32)
    v = jnp.zeros((E, 1), jnp.int32)
    for bit in range(30, -1, -1):
        cand = v | (1 << bit)
        cnt = jnp.sum(jnp.where(keys >= cand, 1.0, 0.0), axis=1, keepdims=True)
        v = jnp.where(cnt >= cap, cand, v)
    above = keys > v
    tied = jnp.where(keys == v, 1.0, 0.0)
    room = cap - jnp.sum(jnp.where(above, 1.0, 0.0), axis=1, keepdims=True)
    kept = jnp.where(above, 1.0, jnp.where(_excl_prefix(tied) < room, tied, 0.0))
    slot = jnp.where(kept > 0.0, _excl_prefix(kept), float(T))
    for e in range(E):
        slot_row_ref[0, e] = slot[e:e + 1, :].astype(jnp.int32)
        aff_row_ref[0, e] = arow[e:e + 1, :]
    st_ref[...] = jnp.full(st_ref.shape, float(T), F32)
    st_ref[0:E, :] = slot
    slot_col_ref[0] = st_ref[...].T.astype(jnp.int32)


def _route_call(aff, cap):
    B, T, _ = aff.shape
    E = N_EXPERTS
    return pl.pallas_call(
        functools.partial(_route_kernel, cap=cap),
        out_shape=(jax.ShapeDtypeStruct((B, E, 1, T), jnp.int32),
                   jax.ShapeDtypeStruct((B, T, LANE), jnp.int32),
                   jax.ShapeDtypeStruct((B, E, 1, T), F32)),
        grid=(B,),
        in_specs=[pl.BlockSpec((1, T, LANE), lambda b: (b, 0, 0))],
        out_specs=(pl.BlockSpec((1, E, 1, T), lambda b: (b, 0, 0, 0)),
                   pl.BlockSpec((1, T, LANE), lambda b: (b, 0, 0)),
                   pl.BlockSpec((1, E, 1, T), lambda b: (b, 0, 0, 0))),
        scratch_shapes=[pltpu.VMEM((LANE, T), F32)],
        compiler_params=_cparams(1),
        name="router_route",
    )(aff)


def _dispatch_kernel(slot_ref, affr_ref, h_ref, xe_ref, gs_ref):
    eg, cap, D = xe_ref.shape
    T = h_ref.shape[1]
    slot = lax.broadcasted_iota(jnp.int32, (cap, T), 0)
    hits = [slot_ref[0, e] == slot for e in range(eg)]
    onehot = jnp.concatenate([jnp.where(h, 1.0, 0.0).astype(BF16) for h in hits], axis=0)
    x = jnp.dot(onehot, h_ref[0], preferred_element_type=F32).astype(BF16)
    xe_ref[...] = x.reshape(eg, cap, D)
    for e in range(eg):
        g = jnp.sum(jnp.where(hits[e], affr_ref[0, e], 0.0), axis=-1, keepdims=True)
        gs_ref[e] = jnp.broadcast_to(g, (cap, LANE))


def _dispatch_call(slot_row, aff_row, h2, cap):
    B, T, D = h2.shape
    E = N_EXPERTS
    eg = max(1, min(E, 512 // cap))
    return pl.pallas_call(
        _dispatch_kernel,
        out_shape=(jax.ShapeDtypeStruct((E, B * cap, D), BF16), jax.ShapeDtypeStruct((E, B * cap, LANE), F32)),
        grid=(B, E // eg),
        in_specs=[pl.BlockSpec((1, eg, 1, T), lambda b, g: (b, g, 0, 0)),
                  pl.BlockSpec((1, eg, 1, T), lambda b, g: (b, g, 0, 0)),
                  pl.BlockSpec((1, T, D), lambda b, g: (b, 0, 0))],
        out_specs=(pl.BlockSpec((eg, cap, D), lambda b, g: (g, b, 0)),
                   pl.BlockSpec((eg, cap, LANE), lambda b, g: (g, b, 0))),
        compiler_params=_cparams(2),
        name="moe_dispatch",
    )(slot_row, aff_row, h2)


def _expert_kernel(*refs, n_groups, nf):
    xs = refs[:n_groups]
    wg_ref, wu_ref, wd_ref = refs[n_groups:n_groups + 3]
    gss = refs[n_groups + 3:2 * n_groups + 3]
    ys = refs[2 * n_groups + 3:3 * n_groups + 3]
    hms = refs[3 * n_groups + 3:]
    s = pl.program_id(1)
    tf = wg_ref.shape[-1]

    @pl.when(s < nf)
    def _():
        wg = wg_ref[...].astype(BF16)
        wu = wu_ref[...].astype(BF16)
        for x_ref, hm_ref in zip(xs, hms):
            x = x_ref[...]
            a = jnp.dot(x, wg, preferred_element_type=F32)
            u = jnp.dot(x, wu, preferred_element_type=F32)
            hm_ref[s] = (_silu(a) * u).astype(BF16)

    @pl.when(s >= nf)
    def _():
        for hm_ref, gs_ref, y_ref in zip(hms, gss, ys):
            acc = jnp.dot(hm_ref[0], wd_ref[0:tf, :].astype(BF16), preferred_element_type=F32)
            for f in range(1, nf):
                acc = acc + jnp.dot(hm_ref[f], wd_ref[f * tf:(f + 1) * tf, :].astype(BF16),
                                    preferred_element_type=F32)
            y_ref[...] = (acc * gs_ref[...][:, 0:1]).astype(BF16)


def _expert_call(groups, w_gate, w_up, w_down, l):
    E, _, D = groups[0][0].shape
    F = w_gate.shape[-1]
    tf = 256
    tn = 512
    nf = F // tf
    nn = D // tn
    n = len(groups)
    Ms = [g[0].shape[1] for g in groups]
    nidx = lambda s: jnp.maximum(s - nf, 0)
    x_e = lambda e, s: jnp.minimum(e + (s >= nf).astype(jnp.int32), E - 1)
    w_e = lambda e, s: jnp.minimum(e + (s > nf).astype(jnp.int32), E - 1)
    w_f = lambda s: jnp.where(s > nf, 0, jnp.minimum(s, nf - 1))
    in_specs = ([pl.BlockSpec((None, M, D), lambda e, s: (x_e(e, s), 0, 0)) for M in Ms]
                + [pl.BlockSpec((None, None, D, tf), lambda e, s: (l, w_e(e, s), 0, w_f(s))),
                   pl.BlockSpec((None, None, D, tf), lambda e, s: (l, w_e(e, s), 0, w_f(s))),
                   pl.BlockSpec((None, None, F, tn), lambda e, s: (l, e, 0, nidx(s)))]
                + [pl.BlockSpec((None, M, LANE), lambda e, s: (e, 0, 0)) for M in Ms])
    return pl.pallas_call(
        functools.partial(_expert_kernel, n_groups=n, nf=nf),
        out_shape=[jax.ShapeDtypeStruct((E, M, D), BF16) for M in Ms],
        grid=(E, nf + nn),
        in_specs=in_specs,
        out_specs=[pl.BlockSpec((None, M, tn), lambda e, s: (e, 0, nidx(s))) for M in Ms],
        scratch_shapes=[pltpu.VMEM((nf, M, tf), BF16) for M in Ms],
        compiler_params=_cparams(2),
        name="moe_experts",
    )(*[g[0] for g in groups], w_gate, w_up, w_down, *[g[1] for g in groups])


def _combine_kernel(slot_ref, y_ref, x_ref, gate_ref, fg_ref, o_ref, *, final_norm):
    cap = y_ref.shape[1]
    tm = x_ref.shape[1]
    lane_slot = lax.broadcasted_iota(jnp.int32, (tm, cap), 1)
    slot = slot_ref[0]
    acc = jnp.zeros(x_ref.shape[1:], F32)
    for e in range(N_EXPERTS):
        hit = jnp.where(slot[:, e:e + 1] == lane_slot, 1.0, 0.0).astype(BF16)
        acc = acc + jnp.dot(hit, y_ref[e], preferred_element_type=F32)
    x2 = x_ref[0] + gate_ref[...] * acc
    if final_norm:
        x2 = _rms(x2, fg_ref[...])
    o_ref[0] = x2


def _combine_call(slot_col, y, x1, mod, row_of, final_g, cap, final_norm, l):
    B, T, D = x1.shape
    tm = min(T, 512)
    return pl.pallas_call(
        functools.partial(_combine_kernel, final_norm=final_norm),
        out_shape=jax.ShapeDtypeStruct((B, T, D), F32),
        grid=(B, T // tm),
        in_specs=[pl.BlockSpec((1, tm, LANE), lambda b, i: (b, i, 0)),
                  pl.BlockSpec((N_EXPERTS, cap, D), lambda b, i: (0, b, 0)),
                  pl.BlockSpec((1, tm, D), lambda b, i: (b, i, 0)),
                  _mod_spec(l, row_of, 5),
                  pl.BlockSpec((1, D), lambda b, i: (0, 0))],
        out_specs=pl.BlockSpec((1, tm, D), lambda b, i: (b, i, 0)),
        compiler_params=_cparams(2),
        name="moe_combine",
    )(slot_col, y, x1, mod, final_g)


def _rope_tables(n):
    rows = n // GRID_W
    row = jnp.repeat(jnp.arange(rows), GRID_W).astype(F32)
    col = jnp.tile(jnp.arange(GRID_W), rows).astype(F32)
    inv_freq = ROPE_BASE ** (-jnp.arange(0, ROPE_AXIS_DIM, 2, dtype=F32) / ROPE_AXIS_DIM)
    ar = row[:, None] * inv_freq
    ac = col[:, None] * inv_freq
    cos = jnp.concatenate([jnp.cos(ar), jnp.cos(ar), jnp.cos(ac), jnp.cos(ac)], axis=-1)
    sin = jnp.concatenate([-jnp.sin(ar), jnp.sin(ar), -jnp.sin(ac), jnp.sin(ac)], axis=-1)
    return jnp.tile(cos, (1, 2)), jnp.tile(sin, (1, 2))


def _reorder_kernel(w_ref, o_ref):
    G = GROUP_W
    low = 2 * GLA_GATE_RANK
    o = 0
    for src, n in ((5 * G + low, 2 * G), (0, 5 * G), (7 * G + low, 3 * G), (5 * G, low)):
        o_ref[:, o:o + n] = w_ref[:, src:src + n].astype(BF16)
        o += n
    o_ref[:, o:] = jnp.zeros((o_ref.shape[0], o_ref.shape[1] - o), BF16)


def _reorder_w_in(w_in):
    L, D, W = w_in.shape
    tr = 256
    return pl.pallas_call(
        _reorder_kernel,
        out_shape=jax.ShapeDtypeStruct((L, D, IN_COLS), BF16),
        grid=(L, D // tr),
        in_specs=[pl.BlockSpec((None, tr, W), lambda l, i: (l, i, 0))],
        out_specs=pl.BlockSpec((None, tr, IN_COLS), lambda l, i: (l, i, 0)),
        compiler_params=_cparams(2),
        name="w_in_reorder",
    )(w_in)


def kernel(x, c, ctx, c_ctx, w_ada, b_ada, norm1_g, norm2_g, w_in, pool_w, pool_scale, gla_gk_up_f, gla_gk_bias_f, gla_gk_up_b, gla_gk_bias_b, gla_norm_g, conv_dw, conv_dw_b, conv_ln_g, conv_ln_b, conv_pw, conv_pw_b, diff_lq1, diff_lk1, diff_lq2, diff_lk2, diff_subln_g, w_out, w_router, w_exp_gate, w_exp_up, w_exp_down, final_norm_g):
    B, N, D = x.shape
    Tc = ctx.shape[1]
    L = w_ada.shape[0]
    assert D == D_MODEL and N % 256 == 0 and Tc % 128 == 0 and B < MOD_ROWS

    cc = jnp.concatenate([c, c_ctx[None, :], jnp.zeros((MOD_ROWS - B - 1, D), F32)], axis=0)
    mod = _ada_call(cc, w_ada, b_ada).reshape(L, MOD_ROWS, 6, 1, D)
    lat_row = lambda b, *_: b
    ctx_row = lambda *_: B

    rows3 = lambda a: a.reshape(L, 1, -1)
    w_in_r = _reorder_w_in(w_in)
    w_out_b = w_out.astype(BF16)
    pool_w_b = pool_w.astype(BF16)
    conv_pw_b16 = conv_pw.astype(BF16)
    R = GLA_GATE_RANK
    zpad = lambda a, lo: jnp.pad(a, ((0, 0), (lo, LANE - R - lo), (0, 0))).astype(BF16)
    upf = zpad(gla_gk_up_f, 0)
    upb = zpad(gla_gk_up_b, R)
    wr = jnp.pad(w_router, ((0, 0), (0, 0), (0, LANE - N_EXPERTS)))
    wr_hi = wr.astype(BF16)
    wr2 = jnp.concatenate([wr_hi, (wr - wr_hi.astype(F32)).astype(BF16)], axis=-1)
    rope_tabs = _rope_tables(N)
    fg = final_norm_g.reshape(1, D)
    n1, n2 = rows3(norm1_g), rows3(norm2_g)
    gla_args = (upf, rows3(gla_gk_bias_f), upb, rows3(gla_gk_bias_b), rows3(gla_norm_g))
    conv_args = (conv_dw, rows3(conv_dw_b), rows3(conv_ln_g), rows3(conv_ln_b), conv_pw_b16, rows3(conv_pw_b))
    diff_vecs = (rows3(diff_lq1), rows3(diff_lk1), rows3(diff_lq2), rows3(diff_lk2), rows3(diff_subln_g))
    pool_args = (pool_w_b, rows3(pool_scale))
    cap_l = EC_CAPACITY * N // N_EXPERTS
    cap_c = EC_CAPACITY * Tc // N_EXPERTS
    flat = lambda a: a.reshape(1, B * Tc, a.shape[-1])
    unflat = lambda a: a.reshape(B, Tc, a.shape[-1])

    for l in range(L):
        last = l == L - 1
        lam_init = 0.8 - 0.6 * math.exp(-0.3 * l)

        Pl = _inproj_call(x, mod, lat_row, n1, w_in_r, l)
        Pc = unflat(_inproj_call(flat(ctx), mod, ctx_row, n1, w_in_r, l))

        gla_l, gla_c = _gla_call(Pl, Pc, *gla_args, l)
        pool_l = _pool_call(Pl, *pool_args, l)
        conv_l = _conv_call(Pl, *conv_args, l)
        diff_l = _diff_call(Pl, [Pl, Pc], rope_tabs, *diff_vecs, lam_init, l)
        x1, h2, aff = _outproj_call((pool_l, gla_l, conv_l, diff_l), w_out_b, x, mod, lat_row, n2, wr2, l)
        slot_row, slot_col, aff_row = _route_call(aff, cap_l)
        groups = [_dispatch_call(slot_row, aff_row, h2, cap_l)]

        if not last:
            pool_c = _pool_call(Pc, *pool_args, l)
            conv_c = _conv_call(Pc, *conv_args, l)
            diff_c = _diff_call(Pc, [Pc], None, *diff_vecs, lam_init, l)
            c1, h2c, affc = _outproj_call((flat(pool_c), flat(gla_c), flat(conv_c), flat(diff_c)), w_out_b,
                                          flat(ctx), mod, ctx_row, n2, wr2, l)
            slot_row_c, slot_col_c, aff_row_c = _route_call(unflat(affc), cap_c)
            groups.append(_dispatch_call(slot_row_c, aff_row_c, unflat(h2c), cap_c))

        ys = _expert_call(groups, w_exp_gate, w_exp_up, w_exp_down, l)
        x = _combine_call(slot_col, ys[0], x1, mod, lat_row, fg, cap_l, last, l)
        if not last:
            ctx = _combine_call(slot_col_c, ys[1], unflat(c1), mod, ctx_row, fg, cap_c, False, l)

    return x
```

```python
import functools
import math

import jax
import jax.numpy as jnp
from jax import lax
from jax.experimental import pallas as pl
from jax.experimental.pallas import tpu as pltpu

F32 = jnp.float32
BF16 = jnp.bfloat16

D_MODEL = 2048
GRID_W = 64
GROUP_W = D_MODEL // 4
POOL_WINDOWS = (2, 4, 8, 16)
HEAD_DIM = 128
N_HEADS = GROUP_W // HEAD_DIM
GLA_GATE_RANK = 16
GLA_GATE_NORM = 16.0
GLA_CHUNK = 64
CONV_K = 31
DIFF_QK_DIM = 64
ROPE_BASE = 10000.0
ROPE_AXIS_DIM = DIFF_QK_DIM // 2
N_EXPERTS = 16
EXPERT_FF = D_MODEL // 2
EC_CAPACITY = 2
NORM_EPS = 1e-6

LANE = 128
SUBLANE = 8
HALO = 16
VMEM_LIMIT = 56 * 1024 * 1024
MOD_ROWS = 16

COL_CONV = 0
COL_POOL = 8
COL_GQ = 12
COL_GK = 16
COL_GV = 20
COL_GGATE = 24
COL_DQ = 28
COL_DK = 32
COL_DV = 36
COL_LOW = 40
MXU_N = 256
IN_COLS = 42 * LANE
IN_TN = 7 * MXU_N

NT_DIMS = (((1,), (1,)), ((), ()))
TN_DIMS = (((0,), (0,)), ((), ()))


def _cparams(n_axes):
    return pltpu.CompilerParams(dimension_semantics=("arbitrary",) * n_axes, vmem_limit_bytes=VMEM_LIMIT)


def _silu(x):
    return x * jax.nn.sigmoid(x)


def _rms(x, g):
    return x * lax.rsqrt(jnp.mean(x * x, axis=-1, keepdims=True) + NORM_EPS) * g


def _norm_mod(x, g, scale, shift):
    inv = lax.rsqrt(jnp.mean(x * x, axis=-1, keepdims=True) + NORM_EPS)
    return x * inv * (g * (1.0 + scale)) + shift


def _layer_spec(l, *tail):
    return pl.BlockSpec((None,) + tail, lambda *_: (l,) + (0,) * len(tail))


def _mod_spec(l, row_of, k):
    return pl.BlockSpec((None, None, None, 1, D_MODEL), lambda *g: (l, row_of(*g), k, 0, 0))


def _ada_kernel(c_ref, w_ref, b_ref, o_ref):
    sc = _silu(c_ref[...]).astype(BF16)
    o_ref[0] = jnp.dot(sc, w_ref[0].astype(BF16), preferred_element_type=F32) + b_ref[0]


def _ada_call(cc, w_ada, b_ada):
    L, D, W = w_ada.shape
    R = cc.shape[0]
    tn = 1536
    return pl.pallas_call(
        _ada_kernel,
        out_shape=jax.ShapeDtypeStruct((L, R, W), F32),
        grid=(L, W // tn),
        in_specs=[
            pl.BlockSpec((R, D), lambda l, j: (0, 0)),
            pl.BlockSpec((1, D, tn), lambda l, j: (l, 0, j)),
            pl.BlockSpec((1, 1, tn), lambda l, j: (l, 0, j)),
        ],
        out_specs=pl.BlockSpec((1, R, tn), lambda l, j: (l, 0, j)),
        compiler_params=_cparams(2),
        name="ada_mod",
    )(cc, w_ada, b_ada.reshape(L, 1, W))


def _inproj_kernel(x_ref, sh_ref, sc_ref, g_ref, w_ref, o_ref, h_ref):
    @pl.when(pl.program_id(2) == 0)
    def _():
        h_ref[...] = _norm_mod(x_ref[0], g_ref[...], sc_ref[...], sh_ref[...]).astype(BF16)

    o_ref[0] = jnp.dot(h_ref[...], w_ref[...], preferred_element_type=F32).astype(BF16)


def _inproj_call(x, mod, row_of, g, w, l):
    Bx, T, D = x.shape
    tm = min(T, 1024)
    return pl.pallas_call(
        _inproj_kernel,
        out_shape=jax.ShapeDtypeStruct((Bx, T, IN_COLS), BF16),
        grid=(Bx, T // tm, IN_COLS // IN_TN),
        in_specs=[
            pl.BlockSpec((1, tm, D), lambda b, i, j: (b, i, 0)),
            _mod_spec(l, row_of, 0),
            _mod_spec(l, row_of, 1),
            _layer_spec(l, 1, D),
            pl.BlockSpec((None, D, IN_TN), lambda b, i, j: (l, 0, j)),
        ],
        out_specs=pl.BlockSpec((1, tm, IN_TN), lambda b, i, j: (b, i, j)),
        scratch_shapes=[pltpu.VMEM((tm, D), BF16)],
        compiler_params=_cparams(3),
        name="in_proj",
    )(x, mod, mod, g, w)


def _halo_specs(T, tt, width, col_block):
    r = tt // HALO
    nh = T // HALO
    return [
        pl.BlockSpec((1, HALO, width), lambda b, c: (b, jnp.maximum(c * r - 1, 0), col_block)),
        pl.BlockSpec((1, tt, width), lambda b, c: (b, c, col_block)),
        pl.BlockSpec((1, HALO, width), lambda b, c: (b, jnp.minimum((c + 1) * r, nh - 1), col_block)),
    ]


def _pool_kernel(prev_ref, cur_ref, next_ref, w_ref, s_ref, o_ref, pad_ref, *, T, tt):
    c = pl.program_id(1)
    H = HALO
    pad_ref[0:H, :] = jnp.where(c > 0, prev_ref[0].astype(F32), 0.0)
    pad_ref[H:H + tt, :] = cur_ref[0].astype(F32)
    pad_ref[H + tt:H + tt + H, :] = jnp.where(c < pl.num_programs(1) - 1, next_ref[0].astype(F32), 0.0)
    rs = min(tt, 256)
    for r0 in range(0, tt, rs):
        t = c * tt + r0 + lax.broadcasted_iota(jnp.int32, (rs, HEAD_DIM), 0)
        for gi, w in enumerate(POOL_WINDOWS):
            cols = slice(gi * HEAD_DIM, (gi + 1) * HEAD_DIM)
            base = r0 + H
            acc = pad_ref[base - w // 2:base - w // 2 + rs, cols]
            for d in range(-w // 2 + 1, w // 2):
                acc = acc + pad_ref[base + d:base + d + rs, cols]
            cnt = jnp.minimum(t + w // 2, T) - jnp.maximum(t - w // 2, 0)
            p = acc / cnt.astype(F32) - pad_ref[base:base + rs, cols]
            y = jnp.dot(p.astype(BF16), w_ref[gi], preferred_element_type=F32)
            o_ref[0, r0:r0 + rs, cols] = (y * s_ref[:, cols]).astype(BF16)


def _pool_call(P, pool_w, pool_scale, l):
    B, T, _ = P.shape
    tt = min(T, 1024)
    return pl.pallas_call(
        functools.partial(_pool_kernel, T=T, tt=tt),
        out_shape=jax.ShapeDtypeStruct((B, T, GROUP_W), BF16),
        grid=(B, T // tt),
        in_specs=_halo_specs(T, tt, GROUP_W, COL_POOL * LANE // GROUP_W) + [
            _layer_spec(l, len(POOL_WINDOWS), HEAD_DIM, HEAD_DIM),
            _layer_spec(l, 1, GROUP_W),
        ],
        out_specs=pl.BlockSpec((1, tt, GROUP_W), lambda b, c: (b, c, 0)),
        scratch_shapes=[pltpu.VMEM((tt + 2 * HALO, GROUP_W), F32)],
        compiler_params=_cparams(2),
        name="pool_mixer",
    )(P, P, P, pool_w, pool_scale)


def _log_decay(z):
    t = jnp.exp2(jnp.abs(z) * (-math.log2(math.e)))
    return jnp.minimum(z, 0.0) * (1.0 / GLA_GATE_NORM) - jnp.log2(1.0 + t) * (math.log(2.0) / GLA_GATE_NORM)


def _gla_block_consts(rb):
    ri = lax.broadcasted_iota(jnp.int32, (rb, rb), 0)
    ci = lax.broadcasted_iota(jnp.int32, (rb, rb), 1)
    shift = GLA_CHUNK.bit_length() - 1
    same = jnp.right_shift(ri, shift) == jnp.right_shift(ci, shift)
    out = []
    for causal in (ci <= ri, ci >= ri):
        tri = jnp.where(causal, jnp.where(same, 1.0, 0.0), 0.0)
        out.append((tri.astype(BF16), tri > 0.0))
    return out


def _gla_prepare(items, consts, q_ref, k_ref, v_ref, g_ref, qg_ref, u_ref, dec_ref, o_ref):
    C = GLA_CHUNK
    rb = consts[0][1].shape[0]
    nchunk = rb // C
    Gs = []
    for d, rows, _ in items:
        tri = consts[d][0]
        g = g_ref[d, rows, :]
        g_hi = g.astype(BF16)
        g_lo = (g - g_hi.astype(F32)).astype(BF16)
        Gs.append(jnp.dot(tri, g_hi, preferred_element_type=F32) + jnp.dot(tri, g_lo, preferred_element_type=F32))
    staged = []
    for (d, rows, c0), G in zip(items, Gs):
        tot = C - 1 if d == 0 else 0
        tots = [G[ci * C + tot:ci * C + tot + 1, :] for ci in range(nchunk)]
        Gt = jnp.concatenate([jnp.broadcast_to(t, (C, LANE)) for t in tots], axis=0)
        r = 0.5 * Gt
        q = q_ref[0, rows, :].astype(F32) * (HEAD_DIM ** -0.5)
        k = k_ref[0, rows, :].astype(F32)
        qg = (q * jnp.exp(G - r)).astype(BF16)
        kg = (k * jnp.exp(r - G)).astype(BF16)
        qg_ref[d, rows, :] = (q * jnp.exp(G)).astype(BF16)
        kd = (k * jnp.exp(Gt - G)).astype(BF16)
        for ci in range(nchunk):
            dec_ref[d, c0 + ci] = jnp.exp(jnp.broadcast_to(tots[ci], (8, LANE)))
        staged.append((qg, kg, kd))
    atts = [lax.dot_general(qg, kg, NT_DIMS, preferred_element_type=F32) for qg, kg, _ in staged]
    for (d, rows, c0), att, (_, _, kd) in zip(items, atts, staged):
        v = v_ref[0, rows, :]
        att = jnp.where(consts[d][1], att, 0.0).astype(BF16)
        o_ref[d, rows, :] = jnp.dot(att, v, preferred_element_type=F32)
        for ci in range(nchunk):
            cr = slice(ci * C, (ci + 1) * C)
            u_ref[d, c0 + ci] = lax.dot_general(v[cr], kd[cr], TN_DIMS, preferred_element_type=F32)


def _gla_kernel(ql, kl, vl, gtl, lowl, qc, kc, vc, gtc, lowc, upf, bf, upb, bb, ng,
                ol_ref, oc_ref, gl, qgl, ul, decl, sbl, osl, gc, qgc, uc, decc, sbc, osc):
    C = GLA_CHUNK
    T = ql.shape[1]
    Tc = qc.shape[1]

    def gates(low_ref, g_ref):
        low = low_ref[0]
        zf = jnp.dot(low, upf[...], preferred_element_type=F32) + bf[...]
        zb = jnp.dot(low, upb[...], preferred_element_type=F32) + bb[...]
        g_ref[0] = _log_decay(zf)
        g_ref[1] = _log_decay(zb)

    def prepare(q_ref, k_ref, v_ref, g_ref, qg_ref, u_ref, dec_ref, o_ref, Tx):
        rb = min(Tx, 256)
        nb = Tx // rb
        per = 4 if nb % 4 == 0 else (2 if nb % 2 == 0 else 1)
        consts = _gla_block_consts(rb)

        def body(j, carry):
            items = []
            for p in range(per):
                blk = j * per + p
                rows = pl.ds(pl.multiple_of(blk * rb, rb), rb)
                items += [(d, rows, blk * (rb // C)) for d in (0, 1)]
            _gla_prepare(items, consts, q_ref, k_ref, v_ref, g_ref, qg_ref, u_ref, dec_ref, o_ref)
            return carry

        lax.fori_loop(0, nb // per, body, 0)

    def recur(u_ref, dec_ref, sb_ref, n, Sf, Sb):
        def step(d, c, S):
            sb_ref[d, c] = S.astype(BF16)
            return S * dec_ref[d, c, 0:1, :] + u_ref[d, c]

        def body(i, carry):
            Sf, Sb = carry
            return step(0, i, Sf), step(1, n - 1 - i, Sb)

        return lax.fori_loop(0, n, body, (Sf, Sb), unroll=min(n, 4))

    def inter(qg_ref, sb_ref, o_ref, n):
        grp = min(n, 4)
        nb = n // grp
        per = 4 if nb % 4 == 0 else (2 if nb % 2 == 0 else 1)

        def body(j, carry):
            work = []
            for p in range(per):
                blk = j * per + p
                rows = pl.ds(pl.multiple_of(blk * grp * C, grp * C), grp * C)
                for d in (0, 1):
                    qg = qg_ref[d, rows, :]
                    parts = [lax.dot_general(qg[ci * C:(ci + 1) * C], sb_ref[d, blk * grp + ci], NT_DIMS,
                                             preferred_element_type=F32) for ci in range(grp)]
                    work.append((d, rows, parts))
            for d, rows, parts in work:
                o_ref[d, rows, :] += jnp.concatenate(parts, axis=0)
            return carry

        lax.fori_loop(0, n // (grp * per), body, 0)

    def finish(o_s, gt_ref, o_ref):
        o = _rms(o_s[0] + o_s[1], ng[...])
        o_ref[0] = (o * _silu(gt_ref[0].astype(F32))).astype(BF16)

    gates(lowc, gc)
    gates(lowl, gl)
    prepare(qc, kc, vc, gc, qgc, uc, decc, osc, Tc)
    prepare(ql, kl, vl, gl, qgl, ul, decl, osl, T)
    S0 = jnp.zeros((HEAD_DIM, HEAD_DIM), F32)
    Sf, Sb = recur(uc, decc, sbc, Tc // C, S0, S0)
    recur(ul, decl, sbl, T // C, Sf, Sb)
    inter(qgc, sbc, osc, Tc // C)
    inter(qgl, sbl, osl, T // C)
    finish(osl, gtl, ol_ref)
    finish(osc, gtc, oc_ref)


def _gla_call(Pl, Pc, upf, bias_f, upb, bias_b, norm_g, l):
    B, T, _ = Pl.shape
    Tc = Pc.shape[1]

    def colspec(Tx, col):
        return pl.BlockSpec((1, Tx, LANE), lambda b, h: (b, 0, col + h))

    def lowspec(Tx):
        return pl.BlockSpec((1, Tx, LANE), lambda b, h: (b, 0, COL_LOW))

    headw = pl.BlockSpec((None, LANE, LANE), lambda b, h: (l, 0, h))
    headv = pl.BlockSpec((None, 1, LANE), lambda b, h: (l, 0, h))
    outspec = lambda Tx: pl.BlockSpec((1, Tx, LANE), lambda b, h: (b, 0, h))

    def scratch(Tx):
        n = Tx // GLA_CHUNK
        return [pltpu.VMEM((2, Tx, LANE), F32),
                pltpu.VMEM((2, Tx, LANE), BF16),
                pltpu.VMEM((2, n, HEAD_DIM, HEAD_DIM), F32),
                pltpu.VMEM((2, n, 8, LANE), F32),
                pltpu.VMEM((2, n, HEAD_DIM, HEAD_DIM), BF16),
                pltpu.VMEM((2, Tx, LANE), F32)]

    return pl.pallas_call(
        _gla_kernel,
        out_shape=(jax.ShapeDtypeStruct((B, T, GROUP_W), BF16), jax.ShapeDtypeStruct((B, Tc, GROUP_W), BF16)),
        grid=(B, N_HEADS),
        in_specs=[colspec(T, COL_GQ), colspec(T, COL_GK), colspec(T, COL_GV), colspec(T, COL_GGATE), lowspec(T),
                  colspec(Tc, COL_GQ), colspec(Tc, COL_GK), colspec(Tc, COL_GV), colspec(Tc, COL_GGATE), lowspec(Tc),
                  headw, headv, headw, headv, _layer_spec(l, 1, LANE)],
        out_specs=(outspec(T), outspec(Tc)),
        scratch_shapes=scratch(T) + scratch(Tc),
        compiler_params=_cparams(2),
        name="gla_mixer",
    )(Pl, Pl, Pl, Pl, Pl, Pc, Pc, Pc, Pc, Pc, upf, bias_f, upb, bias_b, norm_g)


def _conv_kernel(prev_ref, cur_ref, next_ref, dw_ref, dwb_ref, lng_ref, lnb_ref, pw_ref, pwb_ref,
                 o_ref, pad_ref, sh_ref, acc_ref, *, tt):
    c = pl.program_id(1)
    H = HALO
    W = GROUP_W

    def glu(u):
        u = u.astype(F32)
        return u[:, :W] * jax.nn.sigmoid(u[:, W:])

    pad_ref[0:H, :] = jnp.where(c > 0, glu(prev_ref[0]), 0.0)
    pad_ref[H:H + tt, :] = glu(cur_ref[0])
    pad_ref[H + tt:H + tt + H, :] = jnp.where(c < pl.num_programs(1) - 1, glu(next_ref[0]), 0.0)

    n_sh = tt + 2 * H - SUBLANE
    sh_ref[0, :, :] = pad_ref[...]
    for r in range(1, SUBLANE):
        sh_ref[r, 0:n_sh, :] = pad_ref[r:r + n_sh, :]

    rs = min(tt, 128)
    off = H - CONV_K // 2
    for r0 in range(0, tt, rs):
        for lb in range(W // LANE):
            cols = slice(lb * LANE, (lb + 1) * LANE)
            acc = jnp.zeros((rs, LANE), F32) + dwb_ref[:, cols]
            for k in range(CONV_K):
                r = (off + k) % SUBLANE
                a0 = r0 + off + k - r
                acc = acc + sh_ref[r, a0:a0 + rs, cols] * dw_ref[k:k + 1, cols]
            acc_ref[r0:r0 + rs, cols] = acc

    h = acc_ref[...]
    mu = jnp.mean(h, axis=-1, keepdims=True)
    hc = h - mu
    var = jnp.mean(hc * hc, axis=-1, keepdims=True)
    y = hc * lax.rsqrt(var + NORM_EPS) * lng_ref[...] + lnb_ref[...]
    y = jnp.dot(_silu(y).astype(BF16), pw_ref[...], preferred_element_type=F32) + pwb_ref[...]
    o_ref[0] = y.astype(BF16)


def _conv_call(P, dw, dw_b, ln_g, ln_b, pw, pw_b, l):
    B, T, _ = P.shape
    tt = min(T, 512)
    vec = _layer_spec(l, 1, GROUP_W)
    return pl.pallas_call(
        functools.partial(_conv_kernel, tt=tt),
        out_shape=jax.ShapeDtypeStruct((B, T, GROUP_W), BF16),
        grid=(B, T // tt),
        in_specs=_halo_specs(T, tt, 2 * GROUP_W, 0) + [
            _layer_spec(l, CONV_K, GROUP_W), vec, vec, vec, _layer_spec(l, GROUP_W, GROUP_W), vec],
        out_specs=pl.BlockSpec((1, tt, GROUP_W), lambda b, c: (b, c, 0)),
        scratch_shapes=[pltpu.VMEM((tt + 2 * HALO, GROUP_W), F32),
                        pltpu.VMEM((SUBLANE, tt + 2 * HALO, GROUP_W), F32),
                        pltpu.VMEM((tt, GROUP_W), F32)],
        compiler_params=_cparams(2),
        name="conv_mixer",
    )(P, P, P, dw, dw_b, ln_g, ln_b, pw, pw_b)


def _rope(x, cos, sin):
    hw = ROPE_AXIS_DIM // 2
    lane = lax.broadcasted_iota(jnp.int32, x.shape, 1)
    first_half = (lane % (2 * hw)) < hw
    swapped = jnp.where(first_half, pltpu.roll(x, LANE - hw, 1), pltpu.roll(x, hw, 1))
    return x * cos + swapped * sin


def _diff_kernel(*refs, rope, lam_init, n_kv):
    it = iter(refs)
    q_ref = next(it)
    kv = [(next(it), next(it)) for _ in range(n_kv)]
    if rope:
        cq, sq, ck, sk = next(it), next(it), next(it), next(it)
    lq1, lk1, lq2, lk2, sg = next(it), next(it), next(it), next(it), next(it)
    o_ref, kbuf, vbuf = next(it), next(it), next(it)

    @pl.when(pl.program_id(2) == 0)
    def _():
        r0 = 0
        for i, (k_ref, v_ref) in enumerate(kv):
            n = k_ref.shape[1]
            k = k_ref[0]
            if rope and i == 0:
                k = _rope(k.astype(F32), ck[...], sk[...]).astype(BF16)
            kbuf[r0:r0 + n, :] = k
            vbuf[r0:r0 + n, 0:LANE] = v_ref[0]
            r0 += n
        vbuf[:, LANE:2 * LANE] = jnp.ones((vbuf.shape[0], LANE), BF16)

    tq = q_ref.shape[1]
    rs = min(tq, 256)
    k = kbuf[...]
    v1 = vbuf[...]
    lam = (jnp.exp(jnp.sum(lq1[...] * lk1[...], axis=-1, keepdims=True))
           - jnp.exp(jnp.sum(lq2[...] * lk2[...], axis=-1, keepdims=True)) + lam_init)
    scores = []
    for r0 in range(0, tq, rs):
        q = q_ref[0, r0:r0 + rs, :].astype(F32)
        if rope:
            q = _rope(q, cq[r0:r0 + rs, :], sq[r0:r0 + rs, :])
        q = q * (DIFF_QK_DIM ** -0.5 * math.log2(math.e))
        lane = lax.broadcasted_iota(jnp.int32, q.shape, 1)
        q1 = jnp.where(lane < DIFF_QK_DIM, q, 0.0).astype(BF16)
        q2 = jnp.where(lane >= DIFF_QK_DIM, q, 0.0).astype(BF16)
        scores.append((lax.dot_general(q1, k, NT_DIMS, preferred_element_type=F32),
                       lax.dot_general(q2, k, NT_DIMS, preferred_element_type=F32)))
    for r0, (s1, s2) in zip(range(0, tq, rs), scores):
        e1 = jnp.exp2(s1 - jnp.max(s1, axis=-1, keepdims=True)).astype(BF16)
        e2 = jnp.exp2(s2 - jnp.max(s2, axis=-1, keepdims=True)).astype(BF16)
        r1 = jnp.dot(e1, v1, preferred_element_type=F32)
        r2 = jnp.dot(e2, v1, preferred_element_type=F32)
        o = r1[:, :LANE] * (1.0 / r1[:, LANE:LANE + 1]) - r2[:, :LANE] * (lam / r2[:, LANE:LANE + 1])
        o_ref[0, r0:r0 + rs, :] = (_rms(o, sg[...]) * (1.0 - lam_init)).astype(BF16)


def _diff_call(Pq, kv_sources, rope_tabs, lq1, lk1, lq2, lk2, subln_g, lam_init, l):
    B, T, _ = Pq.shape
    tq = min(T, 1024)
    rope = rope_tabs is not None
    in_specs = [pl.BlockSpec((1, tq, LANE), lambda b, h, i: (b, i, COL_DQ + h))]
    args = [Pq]
    Tk = 0
    for Ps in kv_sources:
        n = Ps.shape[1]
        in_specs.append(pl.BlockSpec((1, n, LANE), lambda b, h, i: (b, 0, COL_DK + h)))
        in_specs.append(pl.BlockSpec((1, n, LANE), lambda b, h, i: (b, 0, COL_DV + h)))
        args += [Ps, Ps]
        Tk += n
    if rope:
        cos, sin = rope_tabs
        in_specs += [pl.BlockSpec((tq, LANE), lambda b, h, i: (i, 0)), pl.BlockSpec((tq, LANE), lambda b, h, i: (i, 0)),
                     pl.BlockSpec((T, LANE), lambda b, h, i: (0, 0)), pl.BlockSpec((T, LANE), lambda b, h, i: (0, 0))]
        args += [cos, sin, cos, sin]
    small = _layer_spec(l, 1, DIFF_QK_DIM)
    in_specs += [small, small, small, small, _layer_spec(l, 1, LANE)]
    args += [lq1, lk1, lq2, lk2, subln_g]
    return pl.pallas_call(
        functools.partial(_diff_kernel, rope=rope, lam_init=lam_init, n_kv=len(kv_sources)),
        out_shape=jax.ShapeDtypeStruct((B, T, GROUP_W), BF16),
        grid=(B, N_HEADS, T // tq),
        in_specs=in_specs,
        out_specs=pl.BlockSpec((1, tq, LANE), lambda b, h, i: (b, i, h)),
        scratch_shapes=[pltpu.VMEM((Tk, LANE), BF16), pltpu.VMEM((Tk, 2 * LANE), BF16)],
        compiler_params=_cparams(3),
        name="diff_attn",
    )(*args)


def _outproj_kernel(a_ref, b_ref, c_ref, d_ref, w_ref, x_ref, gate_ref, sh_ref, sc_ref, ng_ref, wr_ref,
                    x1_ref, h2_ref, aff_ref, mix_ref):
    W = GROUP_W
    tm = x_ref.shape[1]
    rs = min(tm, 256)
    for p, m_ref in enumerate((a_ref, b_ref, c_ref, d_ref)):
        mix_ref[:, p * W:(p + 1) * W] = m_ref[0]
    ys = [jnp.dot(mix_ref[r0:r0 + rs, :], w_ref[...], preferred_element_type=F32) for r0 in range(0, tm, rs)]
    for r0, y in zip(range(0, tm, rs), ys):
        rows = slice(r0, r0 + rs)
        x1 = x_ref[0, rows, :] + gate_ref[...] * y
        x1_ref[0, rows, :] = x1
        h = _norm_mod(x1, ng_ref[...], sc_ref[...], sh_ref[...])
        hh = h.astype(BF16)
        hl = (h - hh.astype(F32)).astype(BF16)
        h2_ref[0, rows, :] = hh
        lg2 = jnp.dot(hh, wr_ref[...], preferred_element_type=F32)
        lg = lg2[:, :LANE] + lg2[:, LANE:] + jnp.dot(hl, wr_ref[:, 0:LANE], preferred_element_type=F32)
        lane = lax.broadcasted_iota(jnp.int32, lg.shape, 1)
        lg = jnp.where(lane < N_EXPERTS, lg, -jnp.inf)
        e = jnp.exp(lg - jnp.max(lg, axis=-1, keepdims=True))
        aff_ref[0, rows, :] = e / jnp.sum(e, axis=-1, keepdims=True)


def _outproj_call(mix, w_out, x, mod, row_of, ng, wr2, l):
    Bx, T, D = x.shape
    tm = min(T, 512)
    mixspec = pl.BlockSpec((1, tm, GROUP_W), lambda b, i: (b, i, 0))
    rowspec = pl.BlockSpec((1, tm, D), lambda b, i: (b, i, 0))
    return pl.pallas_call(
        _outproj_kernel,
        out_shape=(jax.ShapeDtypeStruct((Bx, T, D), F32), jax.ShapeDtypeStruct((Bx, T, D), BF16),
                   jax.ShapeDtypeStruct((Bx, T, LANE), F32)),
        grid=(Bx, T // tm),
        in_specs=[mixspec, mixspec, mixspec, mixspec,
                  _layer_spec(l, D, D),
                  rowspec, _mod_spec(l, row_of, 2), _mod_spec(l, row_of, 3), _mod_spec(l, row_of, 4),
                  _layer_spec(l, 1, D), _layer_spec(l, D, 2 * LANE)],
        out_specs=(rowspec, rowspec, pl.BlockSpec((1, tm, LANE), lambda b, i: (b, i, 0))),
        scratch_shapes=[pltpu.VMEM((tm, D), BF16)],
        compiler_params=_cparams(2),
        name="out_proj",
    )(*mix, w_out, x, mod, mod, mod, ng, wr2)


def _excl_prefix(x):
    rows, T = x.shape
    ri = lax.broadcasted_iota(jnp.int32, (LANE, LANE), 0)
    ci = lax.broadcasted_iota(jnp.int32, (LANE, LANE), 1)
    upper = jnp.where(ri <= ci, 1.0, 0.0).astype(BF16)
    carry = jnp.zeros((rows, 1), F32)
    out = []
    for b in range(T // LANE):
        xb = x[:, b * LANE:(b + 1) * LANE]
        inc = jnp.dot(xb.astype(BF16), upper, preferred_element_type=F32)
        out.append(inc - xb + carry)
        carry = carry + jnp.sum(xb, axis=1, keepdims=True)
    return jnp.concatenate(out, axis=1)


def _route_kernel(aff_ref, slot_row_ref, slot_col_ref, aff_row_ref, st_ref, *, cap):
    T = aff_ref.shape[1]
    E = N_EXPERTS
    arow = aff_ref[0].T[0:E, :]
    keys = lax.bitcast_convert_type(arow, jnp.int32)
    v = jnp.zeros((E, 1), jnp.int32)
    for bit in range(30, -1, -1):
        cand = v | (1 << bit)
        cnt = jnp.sum(jnp.where(keys >= cand, 1.0, 0.0), axis=1, keepdims=True)
        v = jnp.where(cnt >= cap, cand, v)
    above = keys > v
    tied = jnp.where(keys == v, 1.0, 0.0)
    room = cap - jnp.sum(jnp.where(above, 1.0, 0.0), axis=1, keepdims=True)
    kept = jnp.where(above, 1.0, jnp.where(_excl_prefix(tied) < room, tied, 0.0))
    slot = jnp.where(kept > 0.0, _excl_prefix(kept), float(T))
    for e in range(E):
        slot_row_ref[0, e] = slot[e:e + 1, :].astype(jnp.int32)
        aff_row_ref[0, e] = arow[e:e + 1, :]
    st_ref[...] = jnp.full(st_ref.shape, float(T), F32)
    st_ref[0:E, :] = slot
    slot_col_ref[0] = st_ref[...].T.astype(jnp.int32)


def _route_call(aff, cap):
    B, T, _ = aff.shape
    E = N_EXPERTS
    return pl.pallas_call(
        functools.partial(_route_kernel, cap=cap),
        out_shape=(jax.ShapeDtypeStruct((B, E, 1, T), jnp.int32),
                   jax.ShapeDtypeStruct((B, T, LANE), jnp.int32),
                   jax.ShapeDtypeStruct((B, E, 1, T), F32)),
        grid=(B,),
        in_specs=[pl.BlockSpec((1, T, LANE), lambda b: (b, 0, 0))],
        out_specs=(pl.BlockSpec((1, E, 1, T), lambda b: (b, 0, 0, 0)),
                   pl.BlockSpec((1, T, LANE), lambda b: (b, 0, 0)),
                   pl.BlockSpec((1, E, 1, T), lambda b: (b, 0, 0, 0))),
        scratch_shapes=[pltpu.VMEM((LANE, T), F32)],
        compiler_params=_cparams(1),
        name="router_route",
    )(aff)


def _dispatch_kernel(slot_ref, affr_ref, h_ref, xe_ref, gs_ref):
    eg, cap, D = xe_ref.shape
    T = h_ref.shape[1]
    slot = lax.broadcasted_iota(jnp.int32, (cap, T), 0)
    hits = [slot_ref[0, e] == slot for e in range(eg)]
    onehot = jnp.concatenate([jnp.where(h, 1.0, 0.0).astype(BF16) for h in hits], axis=0)
    x = jnp.dot(onehot, h_ref[0], preferred_element_type=F32).astype(BF16)
    xe_ref[...] = x.reshape(eg, cap, D)
    for e in range(eg):
        g = jnp.sum(jnp.where(hits[e], affr_ref[0, e], 0.0), axis=-1, keepdims=True)
        gs_ref[e] = jnp.broadcast_to(g, (cap, LANE))


def _dispatch_call(slot_row, aff_row, h2, cap):
    B, T, D = h2.shape
    E = N_EXPERTS
    eg = max(1, min(E, 512 // cap))
    return pl.pallas_call(
        _dispatch_kernel,
        out_shape=(jax.ShapeDtypeStruct((E, B * cap, D), BF16), jax.ShapeDtypeStruct((E, B * cap, LANE), F32)),
        grid=(B, E // eg),
        in_specs=[pl.BlockSpec((1, eg, 1, T), lambda b, g: (b, g, 0, 0)),
                  pl.BlockSpec((1, eg, 1, T), lambda b, g: (b, g, 0, 0)),
                  pl.BlockSpec((1, T, D), lambda b, g: (b, 0, 0))],
        out_specs=(pl.BlockSpec((eg, cap, D), lambda b, g: (g, b, 0)),
                   pl.BlockSpec((eg, cap, LANE), lambda b, g: (g, b, 0))),
        compiler_params=_cparams(2),
        name="moe_dispatch",
    )(slot_row, aff_row, h2)


def _expert_kernel(*refs, n_groups, nf):
    xs = refs[:n_groups]
    wg_ref, wu_ref, wd_ref = refs[n_groups:n_groups + 3]
    gss = refs[n_groups + 3:2 * n_groups + 3]
    ys = refs[2 * n_groups + 3:3 * n_groups + 3]
    hms = refs[3 * n_groups + 3:]
    s = pl.program_id(1)
    tf = wg_ref.shape[-1]

    @pl.when(s < nf)
    def _():
        wg = wg_ref[...].astype(BF16)
        wu = wu_ref[...].astype(BF16)
        for x_ref, hm_ref in zip(xs, hms):
            x = x_ref[...]
            a = jnp.dot(x, wg, preferred_element_type=F32)
            u = jnp.dot(x, wu, preferred_element_type=F32)
            hm_ref[s] = (_silu(a) * u).astype(BF16)

    @pl.when(s >= nf)
    def _():
        for hm_ref, gs_ref, y_ref in zip(hms, gss, ys):
            acc = jnp.dot(hm_ref[0], wd_ref[0:tf, :].astype(BF16), preferred_element_type=F32)
            for f in range(1, nf):
                acc = acc + jnp.dot(hm_ref[f], wd_ref[f * tf:(f + 1) * tf, :].astype(BF16),
                                    preferred_element_type=F32)
            y_ref[...] = (acc * gs_ref[...][:, 0:1]).astype(BF16)


def _expert_call(groups, w_gate, w_up, w_down, l):
    E, _, D = groups[0][0].shape
    F = w_gate.shape[-1]
    tf = 256
    tn = 512
    nf = F // tf
    nn = D // tn
    n = len(groups)
    Ms = [g[0].shape[1] for g in groups]
    nidx = lambda s: jnp.maximum(s - nf, 0)
    x_e = lambda e, s: jnp.minimum(e + (s >= nf).astype(jnp.int32), E - 1)
    w_e = lambda e, s: jnp.minimum(e + (s > nf).astype(jnp.int32), E - 1)
    w_f = lambda s: jnp.where(s > nf, 0, jnp.minimum(s, nf - 1))
    in_specs = ([pl.BlockSpec((None, M, D), lambda e, s: (x_e(e, s), 0, 0)) for M in Ms]
                + [pl.BlockSpec((None, None, D, tf), lambda e, s: (l, w_e(e, s), 0, w_f(s))),
                   pl.BlockSpec((None, None, D, tf), lambda e, s: (l, w_e(e, s), 0, w_f(s))),
                   pl.BlockSpec((None, None, F, tn), lambda e, s: (l, e, 0, nidx(s)))]
                + [pl.BlockSpec((None, M, LANE), lambda e, s: (e, 0, 0)) for M in Ms])
    return pl.pallas_call(
        functools.partial(_expert_kernel, n_groups=n, nf=nf),
        out_shape=[jax.ShapeDtypeStruct((E, M, D), BF16) for M in Ms],
        grid=(E, nf + nn),
        in_specs=in_specs,
        out_specs=[pl.BlockSpec((None, M, tn), lambda e, s: (e, 0, nidx(s))) for M in Ms],
        scratch_shapes=[pltpu.VMEM((nf, M, tf), BF16) for M in Ms],
        compiler_params=_cparams(2),
        name="moe_experts",
    )(*[g[0] for g in groups], w_gate, w_up, w_down, *[g[1] for g in groups])


def _combine_kernel(slot_ref, y_ref, x_ref, gate_ref, fg_ref, o_ref, *, final_norm):
    cap = y_ref.shape[1]
    tm = x_ref.shape[1]
    lane_slot = lax.broadcasted_iota(jnp.int32, (tm, cap), 1)
    slot = slot_ref[0]
    acc = jnp.zeros(x_ref.shape[1:], F32)
    for e in range(N_EXPERTS):
        hit = jnp.where(slot[:, e:e + 1] == lane_slot, 1.0, 0.0).astype(BF16)
        acc = acc + jnp.dot(hit, y_ref[e], preferred_element_type=F32)
    x2 = x_ref[0] + gate_ref[...] * acc
    if final_norm:
        x2 = _rms(x2, fg_ref[...])
    o_ref[0] = x2


def _combine_call(slot_col, y, x1, mod, row_of, final_g, cap, final_norm, l):
    B, T, D = x1.shape
    tm = min(T, 512)
    return pl.pallas_call(
        functools.partial(_combine_kernel, final_norm=final_norm),
        out_shape=jax.ShapeDtypeStruct((B, T, D), F32),
        grid=(B, T // tm),
        in_specs=[pl.BlockSpec((1, tm, LANE), lambda b, i: (b, i, 0)),
                  pl.BlockSpec((N_EXPERTS, cap, D), lambda b, i: (0, b, 0)),
                  pl.BlockSpec((1, tm, D), lambda b, i: (b, i, 0)),
                  _mod_spec(l, row_of, 5),
                  pl.BlockSpec((1, D), lambda b, i: (0, 0))],
        out_specs=pl.BlockSpec((1, tm, D), lambda b, i: (b, i, 0)),
        compiler_params=_cparams(2),
        name="moe_combine",
    )(slot_col, y, x1, mod, final_g)


def _rope_tables(n):
    rows = n // GRID_W
    row = jnp.repeat(jnp.arange(rows), GRID_W).astype(F32)
    col = jnp.tile(jnp.arange(GRID_W), rows).astype(F32)
    inv_freq = ROPE_BASE ** (-jnp.arange(0, ROPE_AXIS_DIM, 2, dtype=F32) / ROPE_AXIS_DIM)
    ar = row[:, None] * inv_freq
    ac = col[:, None] * inv_freq
    cos = jnp.concatenate([jnp.cos(ar), jnp.cos(ar), jnp.cos(ac), jnp.cos(ac)], axis=-1)
    sin = jnp.concatenate([-jnp.sin(ar), jnp.sin(ar), -jnp.sin(ac), jnp.sin(ac)], axis=-1)
    return jnp.tile(cos, (1, 2)), jnp.tile(sin, (1, 2))


def _reorder_kernel(w_ref, o_ref):
    G = GROUP_W
    low = 2 * GLA_GATE_RANK
    o = 0
    for src, n in ((5 * G + low, 2 * G), (0, 5 * G), (7 * G + low, 3 * G), (5 * G, low)):
        o_ref[:, o:o + n] = w_ref[:, src:src + n]
        o += n
    o_ref[:, o:] = jnp.zeros((o_ref.shape[0], o_ref.shape[1] - o), BF16)


def _reorder_w_in(w_in):
    L, D, W = w_in.shape
    tr = 256
    return pl.pallas_call(
        _reorder_kernel,
        out_shape=jax.ShapeDtypeStruct((L, D, IN_COLS), BF16),
        grid=(L, D // tr),
        in_specs=[pl.BlockSpec((None, tr, W), lambda l, i: (l, i, 0))],
        out_specs=pl.BlockSpec((None, tr, IN_COLS), lambda l, i: (l, i, 0)),
        compiler_params=_cparams(2),
        name="w_in_reorder",
    )(w_in)


def kernel(x, c, ctx, c_ctx, w_ada, b_ada, norm1_g, norm2_g, w_in, pool_w, pool_scale, gla_gk_up_f, gla_gk_bias_f, gla_gk_up_b, gla_gk_bias_b, gla_norm_g, conv_dw, conv_dw_b, conv_ln_g, conv_ln_b, conv_pw, conv_pw_b, diff_lq1, diff_lk1, diff_lq2, diff_lk2, diff_subln_g, w_out, w_router, w_exp_gate, w_exp_up, w_exp_down, final_norm_g):
    B, N, D = x.shape
    Tc = ctx.shape[1]
    L = w_ada.shape[0]
    assert D == D_MODEL and N % 256 == 0 and Tc % 128 == 0 and B < MOD_ROWS

    cc = jnp.concatenate([c, c_ctx[None, :], jnp.zeros((MOD_ROWS - B - 1, D), F32)], axis=0)
    mod = _ada_call(cc, w_ada, b_ada).reshape(L, MOD_ROWS, 6, 1, D)
    lat_row = lambda b, *_: b
    ctx_row = lambda *_: B

    rows3 = lambda a: a.reshape(L, 1, -1)
    w_in_r = _reorder_w_in(w_in.astype(BF16))
    w_out_b = w_out.astype(BF16)
    pool_w_b = pool_w.astype(BF16)
    conv_pw_b16 = conv_pw.astype(BF16)
    R = GLA_GATE_RANK
    zpad = lambda a, lo: jnp.pad(a, ((0, 0), (lo, LANE - R - lo), (0, 0))).astype(BF16)
    upf = zpad(gla_gk_up_f, 0)
    upb = zpad(gla_gk_up_b, R)
    wr = jnp.pad(w_router, ((0, 0), (0, 0), (0, LANE - N_EXPERTS)))
    wr_hi = wr.astype(BF16)
    wr2 = jnp.concatenate([wr_hi, (wr - wr_hi.astype(F32)).astype(BF16)], axis=-1)
    rope_tabs = _rope_tables(N)
    fg = final_norm_g.reshape(1, D)
    n1, n2 = rows3(norm1_g), rows3(norm2_g)
    gla_args = (upf, rows3(gla_gk_bias_f), upb, rows3(gla_gk_bias_b), rows3(gla_norm_g))
    conv_args = (conv_dw, rows3(conv_dw_b), rows3(conv_ln_g), rows3(conv_ln_b), conv_pw_b16, rows3(conv_pw_b))
    diff_vecs = (rows3(diff_lq1), rows3(diff_lk1), rows3(diff_lq2), rows3(diff_lk2), rows3(diff_subln_g))
    pool_args = (pool_w_b, rows3(pool_scale))
    cap_l = EC_CAPACITY * N // N_EXPERTS
    cap_c = EC_CAPACITY * Tc // N_EXPERTS
    flat = lambda a: a.reshape(1, B * Tc, a.shape[-1])
    unflat = lambda a: a.reshape(B, Tc, a.shape[-1])

    for l in range(L):
        last = l == L - 1
        lam_init = 0.8 - 0.6 * math.exp(-0.3 * l)

        Pl = _inproj_call(x, mod, lat_row, n1, w_in_r, l)
        Pc = unflat(_inproj_call(flat(ctx), mod, ctx_row, n1, w_in_r, l))

        gla_l, gla_c = _gla_call(Pl, Pc, *gla_args, l)
        pool_l = _pool_call(Pl, *pool_args, l)
        conv_l = _conv_call(Pl, *conv_args, l)
        diff_l = _diff_call(Pl, [Pl, Pc], rope_tabs, *diff_vecs, lam_init, l)
        x1, h2, aff = _outproj_call((pool_l, gla_l, conv_l, diff_l), w_out_b, x, mod, lat_row, n2, wr2, l)
        slot_row, slot_col, aff_row = _route_call(aff, cap_l)
        groups = [_dispatch_call(slot_row, aff_row, h2, cap_l)]

        if not last:
            pool_c = _pool_call(Pc, *pool_args, l)
            conv_c = _conv_call(Pc, *conv_args, l)
            diff_c = _diff_call(Pc, [Pc], None, *diff_vecs, lam_init, l)
            c1, h2c, affc = _outproj_call((flat(pool_c), flat(gla_c), flat(conv_c), flat(diff_c)), w_out_b,
                                          flat(ctx), mod, ctx_row, n2, wr2, l)
            slot_row_c, slot_col_c, aff_row_c = _route_call(unflat(affc), cap_c)
            groups.append(_dispatch_call(slot_row_c, aff_row_c, unflat(h2c), cap_c))

        ys = _expert_call(groups, w_exp_gate, w_exp_up, w_exp_down, l)
        x = _combine_call(slot_col, ys[0], x1, mod, lat_row, fg, cap_l, last, l)
        if not last:
            ctx = _combine_call(slot_col_c, ys[1], unflat(c1), mod, ctx_row, fg, cap_c, False, l)

    return x
```

```python
import functools
import math

import jax
import jax.numpy as jnp
from jax import lax
from jax.experimental import pallas as pl
from jax.experimental.pallas import tpu as pltpu

F32 = jnp.float32
BF16 = jnp.bfloat16

D_MODEL = 2048
GRID_W = 64
GROUP_W = D_MODEL // 4
POOL_WINDOWS = (2, 4, 8, 16)
HEAD_DIM = 128
N_HEADS = GROUP_W // HEAD_DIM
GLA_GATE_RANK = 16
GLA_GATE_NORM = 16.0
GLA_CHUNK = 64
CONV_K = 31
DIFF_QK_DIM = 64
ROPE_BASE = 10000.0
ROPE_AXIS_DIM = DIFF_QK_DIM // 2
N_EXPERTS = 16
EXPERT_FF = D_MODEL // 2
EC_CAPACITY = 2
NORM_EPS = 1e-6

LANE = 128
SUBLANE = 8
HALO = 16
VMEM_LIMIT = 56 * 1024 * 1024
MOD_ROWS = 16

COL_CONV = 0
COL_POOL = 8
COL_GQ = 12
COL_GK = 16
COL_GV = 20
COL_GGATE = 24
COL_DQ = 28
COL_DK = 32
COL_DV = 36
COL_LOW = 40
MXU_N = 256
IN_COLS = 42 * LANE
IN_TN = 7 * MXU_N

NT_DIMS = (((1,), (1,)), ((), ()))
TN_DIMS = (((0,), (0,)), ((), ()))


def _cparams(n_axes):
    return pltpu.CompilerParams(dimension_semantics=("arbitrary",) * n_axes, vmem_limit_bytes=VMEM_LIMIT)


def _silu(x):
    return x * jax.nn.sigmoid(x)


def _rms(x, g):
    return x * lax.rsqrt(jnp.mean(x * x, axis=-1, keepdims=True) + NORM_EPS) * g


def _norm_mod(x, g, scale, shift):
    inv = lax.rsqrt(jnp.mean(x * x, axis=-1, keepdims=True) + NORM_EPS)
    return x * inv * (g * (1.0 + scale)) + shift


def _layer_spec(l, *tail):
    return pl.BlockSpec((None,) + tail, lambda *_: (l,) + (0,) * len(tail))


def _mod_spec(l, row_of, k):
    return pl.BlockSpec((None, None, None, 1, D_MODEL), lambda *g: (l, row_of(*g), k, 0, 0))


def _ada_kernel(c_ref, w_ref, b_ref, o_ref):
    sc = _silu(c_ref[...]).astype(BF16)
    o_ref[0] = jnp.dot(sc, w_ref[0].astype(BF16), preferred_element_type=F32) + b_ref[0]


def _ada_call(cc, w_ada, b_ada):
    L, D, W = w_ada.shape
    R = cc.shape[0]
    tn = 1536
    return pl.pallas_call(
        _ada_kernel,
        out_shape=jax.ShapeDtypeStruct((L, R, W), F32),
        grid=(L, W // tn),
        in_specs=[
            pl.BlockSpec((R, D), lambda l, j: (0, 0)),
            pl.BlockSpec((1, D, tn), lambda l, j: (l, 0, j)),
            pl.BlockSpec((1, 1, tn), lambda l, j: (l, 0, j)),
        ],
        out_specs=pl.BlockSpec((1, R, tn), lambda l, j: (l, 0, j)),
        compiler_params=_cparams(2),
        name="ada_mod",
    )(cc, w_ada, b_ada.reshape(L, 1, W))


def _inproj_kernel(x_ref, sh_ref, sc_ref, g_ref, w_ref, o_ref, h_ref):
    @pl.when(pl.program_id(2) == 0)
    def _():
        h_ref[...] = _norm_mod(x_ref[0], g_ref[...], sc_ref[...], sh_ref[...]).astype(BF16)

    o_ref[0] = jnp.dot(h_ref[...], w_ref[...], preferred_element_type=F32).astype(BF16)


def _inproj_call(x, mod, row_of, g, w, l):
    Bx, T, D = x.shape
    tm = min(T, 1024)
    return pl.pallas_call(
        _inproj_kernel,
        out_shape=jax.ShapeDtypeStruct((Bx, T, IN_COLS), BF16),
        grid=(Bx, T // tm, IN_COLS // IN_TN),
        in_specs=[
            pl.BlockSpec((1, tm, D), lambda b, i, j: (b, i, 0)),
            _mod_spec(l, row_of, 0),
            _mod_spec(l, row_of, 1),
            _layer_spec(l, 1, D),
            pl.BlockSpec((None, D, IN_TN), lambda b, i, j: (l, 0, j)),
        ],
        out_specs=pl.BlockSpec((1, tm, IN_TN), lambda b, i, j: (b, i, j)),
        scratch_shapes=[pltpu.VMEM((tm, D), BF16)],
        compiler_params=_cparams(3),
        name="in_proj",
    )(x, mod, mod, g, w)


def _halo_specs(T, tt, width, col_block):
    r = tt // HALO
    nh = T // HALO
    return [
        pl.BlockSpec((1, HALO, width), lambda b, c: (b, jnp.maximum(c * r - 1, 0), col_block)),
        pl.BlockSpec((1, tt, width), lambda b, c: (b, c, col_block)),
        pl.BlockSpec((1, HALO, width), lambda b, c: (b, jnp.minimum((c + 1) * r, nh - 1), col_block)),
    ]


def _pool_kernel(prev_ref, cur_ref, next_ref, w_ref, s_ref, o_ref, pad_ref, *, T, tt):
    c = pl.program_id(1)
    H = HALO
    pad_ref[0:H, :] = jnp.where(c > 0, prev_ref[0].astype(F32), 0.0)
    pad_ref[H:H + tt, :] = cur_ref[0].astype(F32)
    pad_ref[H + tt:H + tt + H, :] = jnp.where(c < pl.num_programs(1) - 1, next_ref[0].astype(F32), 0.0)
    rs = min(tt, 256)
    for r0 in range(0, tt, rs):
        t = c * tt + r0 + lax.broadcasted_iota(jnp.int32, (rs, HEAD_DIM), 0)
        for gi, w in enumerate(POOL_WINDOWS):
            cols = slice(gi * HEAD_DIM, (gi + 1) * HEAD_DIM)
            base = r0 + H
            acc = pad_ref[base - w // 2:base - w // 2 + rs, cols]
            for d in range(-w // 2 + 1, w // 2):
                acc = acc + pad_ref[base + d:base + d + rs, cols]
            cnt = jnp.minimum(t + w // 2, T) - jnp.maximum(t - w // 2, 0)
            p = acc / cnt.astype(F32) - pad_ref[base:base + rs, cols]
            y = jnp.dot(p.astype(BF16), w_ref[gi], preferred_element_type=F32)
            o_ref[0, r0:r0 + rs, cols] = (y * s_ref[:, cols]).astype(BF16)


def _pool_call(P, pool_w, pool_scale, l):
    B, T, _ = P.shape
    tt = min(T, 1024)
    return pl.pallas_call(
        functools.partial(_pool_kernel, T=T, tt=tt),
        out_shape=jax.ShapeDtypeStruct((B, T, GROUP_W), BF16),
        grid=(B, T // tt),
        in_specs=_halo_specs(T, tt, GROUP_W, COL_POOL * LANE // GROUP_W) + [
            _layer_spec(l, len(POOL_WINDOWS), HEAD_DIM, HEAD_DIM),
            _layer_spec(l, 1, GROUP_W),
        ],
        out_specs=pl.BlockSpec((1, tt, GROUP_W), lambda b, c: (b, c, 0)),
        scratch_shapes=[pltpu.VMEM((tt + 2 * HALO, GROUP_W), F32)],
        compiler_params=_cparams(2),
        name="pool_mixer",
    )(P, P, P, pool_w, pool_scale)


def _log_decay(z):
    t = jnp.exp2(jnp.abs(z) * (-math.log2(math.e)))
    return jnp.minimum(z, 0.0) * (1.0 / GLA_GATE_NORM) - jnp.log2(1.0 + t) * (math.log(2.0) / GLA_GATE_NORM)


def _gla_block_consts(rb):
    ri = lax.broadcasted_iota(jnp.int32, (rb, rb), 0)
    ci = lax.broadcasted_iota(jnp.int32, (rb, rb), 1)
    shift = GLA_CHUNK.bit_length() - 1
    same = jnp.right_shift(ri, shift) == jnp.right_shift(ci, shift)
    out = []
    for causal in (ci <= ri, ci >= ri):
        tri = jnp.where(causal, jnp.where(same, 1.0, 0.0), 0.0)
        out.append((tri.astype(BF16), tri > 0.0))
    return out


def _gla_prepare(items, consts, q_ref, k_ref, v_ref, g_ref, qg_ref, u_ref, dec_ref, o_ref):
    C = GLA_CHUNK
    rb = consts[0][1].shape[0]
    nchunk = rb // C
    Gs = []
    for d, rows, _ in items:
        tri = consts[d][0]
        g = g_ref[d, rows, :]
        g_hi = g.astype(BF16)
        g_lo = (g - g_hi.astype(F32)).astype(BF16)
        Gs.append(jnp.dot(tri, g_hi, preferred_element_type=F32) + jnp.dot(tri, g_lo, preferred_element_type=F32))
    staged = []
    for (d, rows, c0), G in zip(items, Gs):
        tot = C - 1 if d == 0 else 0
        tots = [G[ci * C + tot:ci * C + tot + 1, :] for ci in range(nchunk)]
        Gt = jnp.concatenate([jnp.broadcast_to(t, (C, LANE)) for t in tots], axis=0)
        r = 0.5 * Gt
        q = q_ref[0, rows, :].astype(F32) * (HEAD_DIM ** -0.5)
        k = k_ref[0, rows, :].astype(F32)
        qg = (q * jnp.exp(G - r)).astype(BF16)
        kg = (k * jnp.exp(r - G)).astype(BF16)
        qg_ref[d, rows, :] = (q * jnp.exp(G)).astype(BF16)
        kd = (k * jnp.exp(Gt - G)).astype(BF16)
        for ci in range(nchunk):
            dec_ref[d, c0 + ci] = jnp.exp(jnp.broadcast_to(tots[ci], (8, LANE)))
        staged.append((qg, kg, kd))
    atts = [lax.dot_general(qg, kg, NT_DIMS, preferred_element_type=F32) for qg, kg, _ in staged]
    for (d, rows, c0), att, (_, _, kd) in zip(items, atts, staged):
        v = v_ref[0, rows, :]
        att = jnp.where(consts[d][1], att, 0.0).astype(BF16)
        o_ref[d, rows, :] = jnp.dot(att, v, preferred_element_type=F32)
        for ci in range(nchunk):
            cr = slice(ci * C, (ci + 1) * C)
            u_ref[d, c0 + ci] = lax.dot_general(v[cr], kd[cr], TN_DIMS, preferred_element_type=F32)


def _gla_kernel(ql, kl, vl, gtl, lowl, qc, kc, vc, gtc, lowc, upf, bf, upb, bb, ng,
                ol_ref, oc_ref, gl, qgl, ul, decl, sbl, osl, gc, qgc, uc, decc, sbc, osc):
    C = GLA_CHUNK
    T = ql.shape[1]
    Tc = qc.shape[1]

    def gates(low_ref, g_ref):
        low = low_ref[0]
        zf = jnp.dot(low, upf[...], preferred_element_type=F32) + bf[...]
        zb = jnp.dot(low, upb[...], preferred_element_type=F32) + bb[...]
        g_ref[0] = _log_decay(zf)
        g_ref[1] = _log_decay(zb)

    def prepare(q_ref, k_ref, v_ref, g_ref, qg_ref, u_ref, dec_ref, o_ref, Tx):
        rb = min(Tx, 256)
        nb = Tx // rb
        per = 4 if nb % 4 == 0 else (2 if nb % 2 == 0 else 1)
        consts = _gla_block_consts(rb)

        def body(j, carry):
            items = []
            for p in range(per):
                blk = j * per + p
                rows = pl.ds(pl.multiple_of(blk * rb, rb), rb)
                items += [(d, rows, blk * (rb // C)) for d in (0, 1)]
            _gla_prepare(items, consts, q_ref, k_ref, v_ref, g_ref, qg_ref, u_ref, dec_ref, o_ref)
            return carry

        lax.fori_loop(0, nb // per, body, 0)

    def recur(u_ref, dec_ref, sb_ref, n, Sf, Sb):
        def step(d, c, S):
            sb_ref[d, c] = S.astype(BF16)
            return S * dec_ref[d, c, 0:1, :] + u_ref[d, c]

        def body(i, carry):
            Sf, Sb = carry
            return step(0, i, Sf), step(1, n - 1 - i, Sb)

        return lax.fori_loop(0, n, body, (Sf, Sb), unroll=min(n, 4))

    def inter(qg_ref, sb_ref, o_ref, n):
        grp = min(n, 4)
        nb = n // grp
        per = 4 if nb % 4 == 0 else (2 if nb % 2 == 0 else 1)

        def body(j, carry):
            work = []
            for p in range(per):
                blk = j * per + p
                rows = pl.ds(pl.multiple_of(blk * grp * C, grp * C), grp * C)
                for d in (0, 1):
                    qg = qg_ref[d, rows, :]
                    parts = [lax.dot_general(qg[ci * C:(ci + 1) * C], sb_ref[d, blk * grp + ci], NT_DIMS,
                                             preferred_element_type=F32) for ci in range(grp)]
                    work.append((d, rows, parts))
            for d, rows, parts in work:
                o_ref[d, rows, :] += jnp.concatenate(parts, axis=0)
            return carry

        lax.fori_loop(0, n // (grp * per), body, 0)

    def finish(o_s, gt_ref, o_ref):
        o = _rms(o_s[0] + o_s[1], ng[...])
        o_ref[0] = (o * _silu(gt_ref[0].astype(F32))).astype(BF16)

    gates(lowc, gc)
    gates(lowl, gl)
    prepare(qc, kc, vc, gc, qgc, uc, decc, osc, Tc)
    prepare(ql, kl, vl, gl, qgl, ul, decl, osl, T)
    S0 = jnp.zeros((HEAD_DIM, HEAD_DIM), F32)
    Sf, Sb = recur(uc, decc, sbc, Tc // C, S0, S0)
    recur(ul, decl, sbl, T // C, Sf, Sb)
    inter(qgc, sbc, osc, Tc // C)
    inter(qgl, sbl, osl, T // C)
    finish(osl, gtl, ol_ref)
    finish(osc, gtc, oc_ref)


def _gla_call(Pl, Pc, upf, bias_f, upb, bias_b, norm_g, l):
    B, T, _ = Pl.shape
    Tc = Pc.shape[1]

    def colspec(Tx, col):
        return pl.BlockSpec((1, Tx, LANE), lambda b, h: (b, 0, col + h))

    def lowspec(Tx):
        return pl.BlockSpec((1, Tx, LANE), lambda b, h: (b, 0, COL_LOW))

    headw = pl.BlockSpec((None, LANE, LANE), lambda b, h: (l, 0, h))
    headv = pl.BlockSpec((None, 1, LANE), lambda b, h: (l, 0, h))
    outspec = lambda Tx: pl.BlockSpec((1, Tx, LANE), lambda b, h: (b, 0, h))

    def scratch(Tx):
        n = Tx // GLA_CHUNK
        return [pltpu.VMEM((2, Tx, LANE), F32),
                pltpu.VMEM((2, Tx, LANE), BF16),
                pltpu.VMEM((2, n, HEAD_DIM, HEAD_DIM), F32),
                pltpu.VMEM((2, n, 8, LANE), F32),
                pltpu.VMEM((2, n, HEAD_DIM, HEAD_DIM), BF16),
                pltpu.VMEM((2, Tx, LANE), F32)]

    return pl.pallas_call(
        _gla_kernel,
        out_shape=(jax.ShapeDtypeStruct((B, T, GROUP_W), BF16), jax.ShapeDtypeStruct((B, Tc, GROUP_W), BF16)),
        grid=(B, N_HEADS),
        in_specs=[colspec(T, COL_GQ), colspec(T, COL_GK), colspec(T, COL_GV), colspec(T, COL_GGATE), lowspec(T),
                  colspec(Tc, COL_GQ), colspec(Tc, COL_GK), colspec(Tc, COL_GV), colspec(Tc, COL_GGATE), lowspec(Tc),
                  headw, headv, headw, headv, _layer_spec(l, 1, LANE)],
        out_specs=(outspec(T), outspec(Tc)),
        scratch_shapes=scratch(T) + scratch(Tc),
        compiler_params=_cparams(2),
        name="gla_mixer",
    )(Pl, Pl, Pl, Pl, Pl, Pc, Pc, Pc, Pc, Pc, upf, bias_f, upb, bias_b, norm_g)


def _conv_kernel(prev_ref, cur_ref, next_ref, dw_ref, dwb_ref, lng_ref, lnb_ref, pw_ref, pwb_ref,
                 o_ref, pad_ref, sh_ref, acc_ref, *, tt):
    c = pl.program_id(1)
    H = HALO
    W = GROUP_W

    def glu(u):
        u = u.astype(F32)
        return u[:, :W] * jax.nn.sigmoid(u[:, W:])

    pad_ref[0:H, :] = jnp.where(c > 0, glu(prev_ref[0]), 0.0)
    pad_ref[H:H + tt, :] = glu(cur_ref[0])
    pad_ref[H + tt:H + tt + H, :] = jnp.where(c < pl.num_programs(1) - 1, glu(next_ref[0]), 0.0)

    n_sh = tt + 2 * H - SUBLANE
    sh_ref[0, :, :] = pad_ref[...]
    for r in range(1, SUBLANE):
        sh_ref[r, 0:n_sh, :] = pad_ref[r:r + n_sh, :]

    rs = min(tt, 128)
    off = H - CONV_K // 2
    for r0 in range(0, tt, rs):
        for lb in range(W // LANE):
            cols = slice(lb * LANE, (lb + 1) * LANE)
            acc = jnp.zeros((rs, LANE), F32) + dwb_ref[:, cols]
            for k in range(CONV_K):
                r = (off + k) % SUBLANE
                a0 = r0 + off + k - r
                acc = acc + sh_ref[r, a0:a0 + rs, cols] * dw_ref[k:k + 1, cols]
            acc_ref[r0:r0 + rs, cols] = acc

    h = acc_ref[...]
    mu = jnp.mean(h, axis=-1, keepdims=True)
    hc = h - mu
    var = jnp.mean(hc * hc, axis=-1, keepdims=True)
    y = hc * lax.rsqrt(var + NORM_EPS) * lng_ref[...] + lnb_ref[...]
    y = jnp.dot(_silu(y).astype(BF16), pw_ref[...], preferred_element_type=F32) + pwb_ref[...]
    o_ref[0] = y.astype(BF16)


def _conv_call(P, dw, dw_b, ln_g, ln_b, pw, pw_b, l):
    B, T, _ = P.shape
    tt = min(T, 512)
    vec = _layer_spec(l, 1, GROUP_W)
    return pl.pallas_call(
        functools.partial(_conv_kernel, tt=tt),
        out_shape=jax.ShapeDtypeStruct((B, T, GROUP_W), BF16),
        grid=(B, T // tt),
        in_specs=_halo_specs(T, tt, 2 * GROUP_W, 0) + [
            _layer_spec(l, CONV_K, GROUP_W), vec, vec, vec, _layer_spec(l, GROUP_W, GROUP_W), vec],
        out_specs=pl.BlockSpec((1, tt, GROUP_W), lambda b, c: (b, c, 0)),
        scratch_shapes=[pltpu.VMEM((tt + 2 * HALO, GROUP_W), F32),
                        pltpu.VMEM((SUBLANE, tt + 2 * HALO, GROUP_W), F32),
                        pltpu.VMEM((tt, GROUP_W), F32)],
        compiler_params=_cparams(2),
        name="conv_mixer",
    )(P, P, P, dw, dw_b, ln_g, ln_b, pw, pw_b)


def _rope(x, cos, sin):
    hw = ROPE_AXIS_DIM // 2
    lane = lax.broadcasted_iota(jnp.int32, x.shape, 1)
    first_half = (lane % (2 * hw)) < hw
    swapped = jnp.where(first_half, pltpu.roll(x, LANE - hw, 1), pltpu.roll(x, hw, 1))
    return x * cos + swapped * sin


def _diff_kernel(*refs, rope, lam_init, n_kv):
    it = iter(refs)
    q_ref = next(it)
    kv = [(next(it), next(it)) for _ in range(n_kv)]
    if rope:
        cq, sq, ck, sk = next(it), next(it), next(it), next(it)
    lq1, lk1, lq2, lk2, sg = next(it), next(it), next(it), next(it), next(it)
    o_ref, kbuf, vbuf = next(it), next(it), next(it)

    @pl.when(pl.program_id(2) == 0)
    def _():
        r0 = 0
        for i, (k_ref, v_ref) in enumerate(kv):
            n = k_ref.shape[1]
            k = k_ref[0]
            if rope and i == 0:
                k = _rope(k.astype(F32), ck[...], sk[...]).astype(BF16)
            kbuf[r0:r0 + n, :] = k
            vbuf[r0:r0 + n, 0:LANE] = v_ref[0]
            r0 += n
        vbuf[:, LANE:2 * LANE] = jnp.ones((vbuf.shape[0], LANE), BF16)

    tq = q_ref.shape[1]
    rs = min(tq, 256)
    k = kbuf[...]
    v1 = vbuf[...]
    lam = (jnp.exp(jnp.sum(lq1[...] * lk1[...], axis=-1, keepdims=True))
           - jnp.exp(jnp.sum(lq2[...] * lk2[...], axis=-1, keepdims=True)) + lam_init)
    scores = []
    for r0 in range(0, tq, rs):
        q = q_ref[0, r0:r0 + rs, :].astype(F32)
        if rope:
            q = _rope(q, cq[r0:r0 + rs, :], sq[r0:r0 + rs, :])
        q = q * (DIFF_QK_DIM ** -0.5 * math.log2(math.e))
        lane = lax.broadcasted_iota(jnp.int32, q.shape, 1)
        q1 = jnp.where(lane < DIFF_QK_DIM, q, 0.0).astype(BF16)
        q2 = jnp.where(lane >= DIFF_QK_DIM, q, 0.0).astype(BF16)
        scores.append((lax.dot_general(q1, k, NT_DIMS, preferred_element_type=F32),
                       lax.dot_general(q2, k, NT_DIMS, preferred_element_type=F32)))
    for r0, (s1, s2) in zip(range(0, tq, rs), scores):
        e1 = jnp.exp2(s1 - jnp.max(s1, axis=-1, keepdims=True)).astype(BF16)
        e2 = jnp.exp2(s2 - jnp.max(s2, axis=-1, keepdims=True)).astype(BF16)
        r1 = jnp.dot(e1, v1, preferred_element_type=F32)
        r2 = jnp.dot(e2, v1, preferred_element_type=F32)
        o = r1[:, :LANE] * (1.0 / r1[:, LANE:LANE + 1]) - r2[:, :LANE] * (lam / r2[:, LANE:LANE + 1])
        o_ref[0, r0:r0 + rs, :] = (_rms(o, sg[...]) * (1.0 - lam_init)).astype(BF16)


def _diff_call(Pq, kv_sources, rope_tabs, lq1, lk1, lq2, lk2, subln_g, lam_init, l):
    B, T, _ = Pq.shape
    tq = min(T, 1024)
    rope = rope_tabs is not None
    in_specs = [pl.BlockSpec((1, tq, LANE), lambda b, h, i: (b, i, COL_DQ + h))]
    args = [Pq]
    Tk = 0
    for Ps in kv_sources:
        n = Ps.shape[1]
        in_specs.append(pl.BlockSpec((1, n, LANE), lambda b, h, i: (b, 0, COL_DK + h)))
        in_specs.append(pl.BlockSpec((1, n, LANE), lambda b, h, i: (b, 0, COL_DV + h)))
        args += [Ps, Ps]
        Tk += n
    if rope:
        cos, sin = rope_tabs
        in_specs += [pl.BlockSpec((tq, LANE), lambda b, h, i: (i, 0)), pl.BlockSpec((tq, LANE), lambda b, h, i: (i, 0)),
                     pl.BlockSpec((T, LANE), lambda b, h, i: (0, 0)), pl.BlockSpec((T, LANE), lambda b, h, i: (0, 0))]
        args += [cos, sin, cos, sin]
    small = _layer_spec(l, 1, DIFF_QK_DIM)
    in_specs += [small, small, small, small, _layer_spec(l, 1, LANE)]
    args += [lq1, lk1, lq2, lk2, subln_g]
    return pl.pallas_call(
        functools.partial(_diff_kernel, rope=rope, lam_init=lam_init, n_kv=len(kv_sources)),
        out_shape=jax.ShapeDtypeStruct((B, T, GROUP_W), BF16),
        grid=(B, N_HEADS, T // tq),
        in_specs=in_specs,
        out_specs=pl.BlockSpec((1, tq, LANE), lambda b, h, i: (b, i, h)),
        scratch_shapes=[pltpu.VMEM((Tk, LANE), BF16), pltpu.VMEM((Tk, 2 * LANE), BF16)],
        compiler_params=_cparams(3),
        name="diff_attn",
    )(*args)


def _outproj_kernel(a_ref, b_ref, c_ref, d_ref, w_ref, x_ref, gate_ref, sh_ref, sc_ref, ng_ref, wr_ref,
                    x1_ref, h2_ref, aff_ref, mix_ref):
    W = GROUP_W
    tm = x_ref.shape[1]
    rs = min(tm, 256)
    for p, m_ref in enumerate((a_ref, b_ref, c_ref, d_ref)):
        mix_ref[:, p * W:(p + 1) * W] = m_ref[0]
    ys = [jnp.dot(mix_ref[r0:r0 + rs, :], w_ref[...], preferred_element_type=F32) for r0 in range(0, tm, rs)]
    for r0, y in zip(range(0, tm, rs), ys):
        rows = slice(r0, r0 + rs)
        x1 = x_ref[0, rows, :] + gate_ref[...] * y
        x1_ref[0, rows, :] = x1
        h = _norm_mod(x1, ng_ref[...], sc_ref[...], sh_ref[...])
        hh = h.astype(BF16)
        hl = (h - hh.astype(F32)).astype(BF16)
        h2_ref[0, rows, :] = hh
        lg2 = jnp.dot(hh, wr_ref[...], preferred_element_type=F32)
        lg = lg2[:, :LANE] + lg2[:, LANE:] + jnp.dot(hl, wr_ref[:, 0:LANE], preferred_element_type=F32)
        lane = lax.broadcasted_iota(jnp.int32, lg.shape, 1)
        lg = jnp.where(lane < N_EXPERTS, lg, -jnp.inf)
        e = jnp.exp(lg - jnp.max(lg, axis=-1, keepdims=True))
        aff_ref[0, rows, :] = e / jnp.sum(e, axis=-1, keepdims=True)


def _outproj_call(mix, w_out, x, mod, row_of, ng, wr2, l):
    Bx, T, D = x.shape
    tm = min(T, 512)
    mixspec = pl.BlockSpec((1, tm, GROUP_W), lambda b, i: (b, i, 0))
    rowspec = pl.BlockSpec((1, tm, D), lambda b, i: (b, i, 0))
    return pl.pallas_call(
        _outproj_kernel,
        out_shape=(jax.ShapeDtypeStruct((Bx, T, D), F32), jax.ShapeDtypeStruct((Bx, T, D), BF16),
                   jax.ShapeDtypeStruct((Bx, T, LANE), F32)),
        grid=(Bx, T // tm),
        in_specs=[mixspec, mixspec, mixspec, mixspec,
                  _layer_spec(l, D, D),
                  rowspec, _mod_spec(l, row_of, 2), _mod_spec(l, row_of, 3), _mod_spec(l, row_of, 4),
                  _layer_spec(l, 1, D), _layer_spec(l, D, 2 * LANE)],
        out_specs=(rowspec, rowspec, pl.BlockSpec((1, tm, LANE), lambda b, i: (b, i, 0))),
        scratch_shapes=[pltpu.VMEM((tm, D), BF16)],
        compiler_params=_cparams(2),
        name="out_proj",
    )(*mix, w_out, x, mod, mod, mod, ng, wr2)


def _excl_prefix(x):
    rows, T = x.shape
    ri = lax.broadcasted_iota(jnp.int32, (LANE, LANE), 0)
    ci = lax.broadcasted_iota(jnp.int32, (LANE, LANE), 1)
    upper = jnp.where(ri <= ci, 1.0, 0.0).astype(BF16)
    carry = jnp.zeros((rows, 1), F32)
    out = []
    for b in range(T // LANE):
        xb = x[:, b * LANE:(b + 1) * LANE]
        inc = jnp.dot(xb.astype(BF16), upper, preferred_element_type=F32)
        out.append(inc - xb + carry)
        carry = carry + jnp.sum(xb, axis=1, keepdims=True)
    return jnp.concatenate(out, axis=1)


def _route_kernel(aff_ref, slot_row_ref, slot_col_ref, aff_row_ref, st_ref, *, cap):
    bt, T, _ = aff_ref.shape
    E = N_EXPERTS
    arow = jnp.concatenate([aff_ref[bi].T[0:E, :] for bi in range(bt)], axis=0)
    keys = lax.bitcast_convert_type(arow, jnp.int32)
    v = jnp.zeros((bt * E, 1), jnp.int32)
    for bit in range(30, -1, -1):
        cand = v | (1 << bit)
        cnt = jnp.sum(jnp.where(keys >= cand, 1.0, 0.0), axis=1, keepdims=True)
        v = jnp.where(cnt >= cap, cand, v)
    above = keys > v
    tied = jnp.where(keys == v, 1.0, 0.0)
    room = cap - jnp.sum(jnp.where(above, 1.0, 0.0), axis=1, keepdims=True)
    kept = jnp.where(above, 1.0, jnp.where(_excl_prefix(tied) < room, tied, 0.0))
    slot = jnp.where(kept > 0.0, _excl_prefix(kept), float(T))
    st_ref[...] = jnp.full(st_ref.shape, float(T), F32)
    for bi in range(bt):
        for e in range(E):
            r = bi * E + e
            slot_row_ref[bi, e] = slot[r:r + 1, :].astype(jnp.int32)
            aff_row_ref[bi, e] = arow[r:r + 1, :]
        st_ref[0:E, :] = slot[bi * E:(bi + 1) * E, :]
        slot_col_ref[bi] = st_ref[...].T.astype(jnp.int32)


def _route_call(aff, cap):
    B, T, _ = aff.shape
    E = N_EXPERTS
    bt = 4 if B % 4 == 0 else (2 if B % 2 == 0 else 1)
    return pl.pallas_call(
        functools.partial(_route_kernel, cap=cap),
        out_shape=(jax.ShapeDtypeStruct((B, E, 1, T), jnp.int32),
                   jax.ShapeDtypeStruct((B, T, LANE), jnp.int32),
                   jax.ShapeDtypeStruct((B, E, 1, T), F32)),
        grid=(B // bt,),
        in_specs=[pl.BlockSpec((bt, T, LANE), lambda b: (b, 0, 0))],
        out_specs=(pl.BlockSpec((bt, E, 1, T), lambda b: (b, 0, 0, 0)),
                   pl.BlockSpec((bt, T, LANE), lambda b: (b, 0, 0)),
                   pl.BlockSpec((bt, E, 1, T), lambda b: (b, 0, 0, 0))),
        scratch_shapes=[pltpu.VMEM((LANE, T), F32)],
        compiler_params=_cparams(1),
        name="router_route",
    )(aff)


def _dispatch_kernel(slot_ref, affr_ref, h_ref, xe_ref, gs_ref):
    eg, cap, D = xe_ref.shape
    T = h_ref.shape[1]
    slot = lax.broadcasted_iota(jnp.int32, (cap, T), 0)
    hits = [slot_ref[0, e] == slot for e in range(eg)]
    onehot = jnp.concatenate([jnp.where(h, 1.0, 0.0).astype(BF16) for h in hits], axis=0)
    x = jnp.dot(onehot, h_ref[0], preferred_element_type=F32).astype(BF16)
    xe_ref[...] = x.reshape(eg, cap, D)
    for e in range(eg):
        g = jnp.sum(jnp.where(hits[e], affr_ref[0, e], 0.0), axis=-1, keepdims=True)
        gs_ref[e] = jnp.broadcast_to(g, (cap, LANE))


def _dispatch_call(slot_row, aff_row, h2, cap):
    B, T, D = h2.shape
    E = N_EXPERTS
    eg = max(1, min(E, 1024 // cap))
    return pl.pallas_call(
        _dispatch_kernel,
        out_shape=(jax.ShapeDtypeStruct((E, B * cap, D), BF16), jax.ShapeDtypeStruct((E, B * cap, LANE), F32)),
        grid=(B, E // eg),
        in_specs=[pl.BlockSpec((1, eg, 1, T), lambda b, g: (b, g, 0, 0)),
                  pl.BlockSpec((1, eg, 1, T), lambda b, g: (b, g, 0, 0)),
                  pl.BlockSpec((1, T, D), lambda b, g: (b, 0, 0))],
        out_specs=(pl.BlockSpec((eg, cap, D), lambda b, g: (g, b, 0)),
                   pl.BlockSpec((eg, cap, LANE), lambda b, g: (g, b, 0))),
        compiler_params=_cparams(2),
        name="moe_dispatch",
    )(slot_row, aff_row, h2)


def _expert_kernel(*refs, n_groups, nf):
    xs = refs[:n_groups]
    wg_ref, wu_ref, wd_ref = refs[n_groups:n_groups + 3]
    gss = refs[n_groups + 3:2 * n_groups + 3]
    ys = refs[2 * n_groups + 3:3 * n_groups + 3]
    hms = refs[3 * n_groups + 3:]
    s = pl.program_id(1)
    tf = wg_ref.shape[-1]

    @pl.when(s < nf)
    def _():
        wg = wg_ref[...].astype(BF16)
        wu = wu_ref[...].astype(BF16)
        for x_ref, hm_ref in zip(xs, hms):
            x = x_ref[...]
            a = jnp.dot(x, wg, preferred_element_type=F32)
            u = jnp.dot(x, wu, preferred_element_type=F32)
            hm_ref[s] = (_silu(a) * u).astype(BF16)

    @pl.when(s >= nf)
    def _():
        for hm_ref, gs_ref, y_ref in zip(hms, gss, ys):
            acc = jnp.dot(hm_ref[0], wd_ref[0:tf, :].astype(BF16), preferred_element_type=F32)
            for f in range(1, nf):
                acc = acc + jnp.dot(hm_ref[f], wd_ref[f * tf:(f + 1) * tf, :].astype(BF16),
                                    preferred_element_type=F32)
            y_ref[...] = (acc * gs_ref[...][:, 0:1]).astype(BF16)


def _expert_call(groups, w_gate, w_up, w_down, l):
    E, _, D = groups[0][0].shape
    F = w_gate.shape[-1]
    tf = 256
    tn = 512
    nf = F // tf
    nn = D // tn
    n = len(groups)
    Ms = [g[0].shape[1] for g in groups]
    nidx = lambda s: jnp.maximum(s - nf, 0)
    x_e = lambda e, s: jnp.minimum(e + (s >= nf).astype(jnp.int32), E - 1)
    w_e = lambda e, s: jnp.minimum(e + (s > nf).astype(jnp.int32), E - 1)
    w_f = lambda s: jnp.where(s > nf, 0, jnp.minimum(s, nf - 1))
    in_specs = ([pl.BlockSpec((None, M, D), lambda e, s: (x_e(e, s), 0, 0)) for M in Ms]
                + [pl.BlockSpec((None, None, D, tf), lambda e, s: (l, w_e(e, s), 0, w_f(s))),
                   pl.BlockSpec((None, None, D, tf), lambda e, s: (l, w_e(e, s), 0, w_f(s))),
                   pl.BlockSpec((None, None, F, tn), lambda e, s: (l, e, 0, nidx(s)))]
                + [pl.BlockSpec((None, M, LANE), lambda e, s: (e, 0, 0)) for M in Ms])
    return pl.pallas_call(
        functools.partial(_expert_kernel, n_groups=n, nf=nf),
        out_shape=[jax.ShapeDtypeStruct((E, M, D), BF16) for M in Ms],
        grid=(E, nf + nn),
        in_specs=in_specs,
        out_specs=[pl.BlockSpec((None, M, tn), lambda e, s: (e, 0, nidx(s))) for M in Ms],
        scratch_shapes=[pltpu.VMEM((nf, M, tf), BF16) for M in Ms],
        compiler_params=_cparams(2),
        name="moe_experts",
    )(*[g[0] for g in groups], w_gate, w_up, w_down, *[g[1] for g in groups])


def _combine_kernel(slot_ref, y_ref, x_ref, gate_ref, fg_ref, o_ref, *, final_norm):
    cap = y_ref.shape[1]
    tm = x_ref.shape[1]
    lane_slot = lax.broadcasted_iota(jnp.int32, (tm, cap), 1)
    slot = slot_ref[0]
    acc = jnp.zeros(x_ref.shape[1:], F32)
    for e in range(N_EXPERTS):
        hit = jnp.where(slot[:, e:e + 1] == lane_slot, 1.0, 0.0).astype(BF16)
        acc = acc + jnp.dot(hit, y_ref[e], preferred_element_type=F32)
    x2 = x_ref[0] + gate_ref[...] * acc
    if final_norm:
        x2 = _rms(x2, fg_ref[...])
    o_ref[0] = x2


def _combine_call(slot_col, y, x1, mod, row_of, final_g, cap, final_norm, l):
    B, T, D = x1.shape
    tm = min(T, 512)
    return pl.pallas_call(
        functools.partial(_combine_kernel, final_norm=final_norm),
        out_shape=jax.ShapeDtypeStruct((B, T, D), F32),
        grid=(B, T // tm),
        in_specs=[pl.BlockSpec((1, tm, LANE), lambda b, i: (b, i, 0)),
                  pl.BlockSpec((N_EXPERTS, cap, D), lambda b, i: (0, b, 0)),
                  pl.BlockSpec((1, tm, D), lambda b, i: (b, i, 0)),
                  _mod_spec(l, row_of, 5),
                  pl.BlockSpec((1, D), lambda b, i: (0, 0))],
        out_specs=pl.BlockSpec((1, tm, D), lambda b, i: (b, i, 0)),
        compiler_params=_cparams(2),
        name="moe_combine",
    )(slot_col, y, x1, mod, final_g)


def _rope_tables(n):
    rows = n // GRID_W
    row = jnp.repeat(jnp.arange(rows), GRID_W).astype(F32)
    col = jnp.tile(jnp.arange(GRID_W), rows).astype(F32)
    inv_freq = ROPE_BASE ** (-jnp.arange(0, ROPE_AXIS_DIM, 2, dtype=F32) / ROPE_AXIS_DIM)
    ar = row[:, None] * inv_freq
    ac = col[:, None] * inv_freq
    cos = jnp.concatenate([jnp.cos(ar), jnp.cos(ar), jnp.cos(ac), jnp.cos(ac)], axis=-1)
    sin = jnp.concatenate([-jnp.sin(ar), jnp.sin(ar), -jnp.sin(ac), jnp.sin(ac)], axis=-1)
    return jnp.tile(cos, (1, 2)), jnp.tile(sin, (1, 2))


def _reorder_kernel(w_ref, o_ref):
    G = GROUP_W
    low = 2 * GLA_GATE_RANK
    o = 0
    for src, n in ((5 * G + low, 2 * G), (0, 5 * G), (7 * G + low, 3 * G), (5 * G, low)):
        o_ref[:, o:o + n] = w_ref[:, src:src + n]
        o += n
    o_ref[:, o:] = jnp.zeros((o_ref.shape[0], o_ref.shape[1] - o), BF16)


def _reorder_w_in(w_in):
    L, D, W = w_in.shape
    tr = 256
    return pl.pallas_call(
        _reorder_kernel,
        out_shape=jax.ShapeDtypeStruct((L, D, IN_COLS), BF16),
        grid=(L, D // tr),
        in_specs=[pl.BlockSpec((None, tr, W), lambda l, i: (l, i, 0))],
        out_specs=pl.BlockSpec((None, tr, IN_COLS), lambda l, i: (l, i, 0)),
        compiler_params=_cparams(2),
        name="w_in_reorder",
    )(w_in)


def kernel(x, c, ctx, c_ctx, w_ada, b_ada, norm1_g, norm2_g, w_in, pool_w, pool_scale, gla_gk_up_f, gla_gk_bias_f, gla_gk_up_b, gla_gk_bias_b, gla_norm_g, conv_dw, conv_dw_b, conv_ln_g, conv_ln_b, conv_pw, conv_pw_b, diff_lq1, diff_lk1, diff_lq2, diff_lk2, diff_subln_g, w_out, w_router, w_exp_gate, w_exp_up, w_exp_down, final_norm_g):
    B, N, D = x.shape
    Tc = ctx.shape[1]
    L = w_ada.shape[0]
    assert D == D_MODEL and N % 256 == 0 and Tc % 128 == 0 and B < MOD_ROWS

    cc = jnp.concatenate([c, c_ctx[None, :], jnp.zeros((MOD_ROWS - B - 1, D), F32)], axis=0)
    mod = _ada_call(cc, w_ada, b_ada).reshape(L, MOD_ROWS, 6, 1, D)
    lat_row = lambda b, *_: b
    ctx_row = lambda *_: B

    rows3 = lambda a: a.reshape(L, 1, -1)
    w_in_r = _reorder_w_in(w_in.astype(BF16))
    w_out_b = w_out.astype(BF16)
    pool_w_b = pool_w.astype(BF16)
    conv_pw_b16 = conv_pw.astype(BF16)
    R = GLA_GATE_RANK
    zpad = lambda a, lo: jnp.pad(a, ((0, 0), (lo, LANE - R - lo), (0, 0))).astype(BF16)
    upf = zpad(gla_gk_up_f, 0)
    upb = zpad(gla_gk_up_b, R)
    wr = jnp.pad(w_router, ((0, 0), (0, 0), (0, LANE - N_EXPERTS)))
    wr_hi = wr.astype(BF16)
    wr2 = jnp.concatenate([wr_hi, (wr - wr_hi.astype(F32)).astype(BF16)], axis=-1)
    rope_tabs = _rope_tables(N)
    fg = final_norm_g.reshape(1, D)
    n1, n2 = rows3(norm1_g), rows3(norm2_g)
    gla_args = (upf, rows3(gla_gk_bias_f), upb, rows3(gla_gk_bias_b), rows3(gla_norm_g))
    conv_args = (conv_dw, rows3(conv_dw_b), rows3(conv_ln_g), rows3(conv_ln_b), conv_pw_b16, rows3(conv_pw_b))
    diff_vecs = (rows3(diff_lq1), rows3(diff_lk1), rows3(diff_lq2), rows3(diff_lk2), rows3(diff_subln_g))
    pool_args = (pool_w_b, rows3(pool_scale))
    cap_l = EC_CAPACITY * N // N_EXPERTS
    cap_c = EC_CAPACITY * Tc // N_EXPERTS
    flat = lambda a: a.reshape(1, B * Tc, a.shape[-1])
    unflat = lambda a: a.reshape(B, Tc, a.shape[-1])

    for l in range(L):
        last = l == L - 1
        lam_init = 0.8 - 0.6 * math.exp(-0.3 * l)

        Pl = _inproj_call(x, mod, lat_row, n1, w_in_r, l)
        Pc = unflat(_inproj_call(flat(ctx), mod, ctx_row, n1, w_in_r, l))

        gla_l, gla_c = _gla_call(Pl, Pc, *gla_args, l)
        pool_l = _pool_call(Pl, *pool_args, l)
        conv_l = _conv_call(Pl, *conv_args, l)
        diff_l = _diff_call(Pl, [Pl, Pc], rope_tabs, *diff_vecs, lam_init, l)
        x1, h2, aff = _outproj_call((pool_l, gla_l, conv_l, diff_l), w_out_b, x, mod, lat_row, n2, wr2, l)
        slot_row, slot_col, aff_row = _route_call(aff, cap_l)
        groups = [_dispatch_call(slot_row, aff_row, h2, cap_l)]

        if not last:
            pool_c = _pool_call(Pc, *pool_args, l)
            conv_c = _conv_call(Pc, *conv_args, l)
            diff_c = _diff_call(Pc, [Pc], None, *diff_vecs, lam_init, l)
            c1, h2c, affc = _outproj_call((flat(pool_c), flat(gla_c), flat(conv_c), flat(diff_c)), w_out_b,
                                          flat(ctx), mod, ctx_row, n2, wr2, l)
            slot_row_c, slot_col_c, aff_row_c = _route_call(unflat(affc), cap_c)
            groups.append(_dispatch_call(slot_row_c, aff_row_c, unflat(h2c), cap_c))

        ys = _expert_call(groups, w_exp_gate, w_exp_up, w_exp_down, l)
        x = _combine_call(slot_col, ys[0], x1, mod, lat_row, fg, cap_l, last, l)
        if not last:
            ctx = _combine_call(slot_col_c, ys[1], unflat(c1), mod, ctx_row, fg, cap_c, False, l)

    return x
```

```python
import functools
import math

import jax
import jax.numpy as jnp
from jax import lax
from jax.experimental import pallas as pl
from jax.experimental.pallas import tpu as pltpu

F32 = jnp.float32
BF16 = jnp.bfloat16

D_MODEL = 2048
GRID_W = 64
GROUP_W = D_MODEL // 4
POOL_WINDOWS = (2, 4, 8, 16)
HEAD_DIM = 128
N_HEADS = GROUP_W // HEAD_DIM
GLA_GATE_RANK = 16
GLA_GATE_NORM = 16.0
GLA_CHUNK = 64
CONV_K = 31
DIFF_QK_DIM = 64
ROPE_BASE = 10000.0
ROPE_AXIS_DIM = DIFF_QK_DIM // 2
N_EXPERTS = 16
EXPERT_FF = D_MODEL // 2
EC_CAPACITY = 2
NORM_EPS = 1e-6

LANE = 128
SUBLANE = 8
HALO = 16
VMEM_LIMIT = 56 * 1024 * 1024
MOD_ROWS = 16

COL_CONV = 0
COL_POOL = 8
COL_GQ = 12
COL_GK = 16
COL_GV = 20
COL_GGATE = 24
COL_DQ = 28
COL_DK = 32
COL_DV = 36
COL_LOW = 40
MXU_N = 256
IN_COLS = 42 * LANE
IN_TN = 7 * MXU_N

NT_DIMS = (((1,), (1,)), ((), ()))
TN_DIMS = (((0,), (0,)), ((), ()))


def _cparams(n_axes):
    return pltpu.CompilerParams(dimension_semantics=("arbitrary",) * n_axes, vmem_limit_bytes=VMEM_LIMIT)


def _silu(x):
    return x * jax.nn.sigmoid(x)


def _rms(x, g):
    return x * lax.rsqrt(jnp.mean(x * x, axis=-1, keepdims=True) + NORM_EPS) * g


def _norm_mod(x, g, scale, shift):
    inv = lax.rsqrt(jnp.mean(x * x, axis=-1, keepdims=True) + NORM_EPS)
    return x * inv * (g * (1.0 + scale)) + shift


def _layer_spec(l, *tail):
    return pl.BlockSpec((None,) + tail, lambda *_: (l,) + (0,) * len(tail))


def _mod_spec(l, row_of, k):
    return pl.BlockSpec((None, None, None, 1, D_MODEL), lambda *g: (l, row_of(*g), k, 0, 0))


def _ada_kernel(c_ref, w_ref, b_ref, o_ref):
    sc = _silu(c_ref[...]).astype(BF16)
    o_ref[0] = jnp.dot(sc, w_ref[0].astype(BF16), preferred_element_type=F32) + b_ref[0]


def _ada_call(cc, w_ada, b_ada):
    L, D, W = w_ada.shape
    R = cc.shape[0]
    tn = 1536
    return pl.pallas_call(
        _ada_kernel,
        out_shape=jax.ShapeDtypeStruct((L, R, W), F32),
        grid=(L, W // tn),
        in_specs=[
            pl.BlockSpec((R, D), lambda l, j: (0, 0)),
            pl.BlockSpec((1, D, tn), lambda l, j: (l, 0, j)),
            pl.BlockSpec((1, 1, tn), lambda l, j: (l, 0, j)),
        ],
        out_specs=pl.BlockSpec((1, R, tn), lambda l, j: (l, 0, j)),
        compiler_params=_cparams(2),
        name="ada_mod",
    )(cc, w_ada, b_ada.reshape(L, 1, W))


def _inproj_kernel(x_ref, sh_ref, sc_ref, g_ref, w_ref, o_ref, h_ref):
    @pl.when(pl.program_id(2) == 0)
    def _():
        h_ref[...] = _norm_mod(x_ref[0], g_ref[...], sc_ref[...], sh_ref[...]).astype(BF16)

    o_ref[0] = jnp.dot(h_ref[...], w_ref[...], preferred_element_type=F32).astype(BF16)


def _inproj_call(x, mod, row_of, g, w, l):
    Bx, T, D = x.shape
    tm = min(T, 1024)
    return pl.pallas_call(
        _inproj_kernel,
        out_shape=jax.ShapeDtypeStruct((Bx, T, IN_COLS), BF16),
        grid=(Bx, T // tm, IN_COLS // IN_TN),
        in_specs=[
            pl.BlockSpec((1, tm, D), lambda b, i, j: (b, i, 0)),
            _mod_spec(l, row_of, 0),
            _mod_spec(l, row_of, 1),
            _layer_spec(l, 1, D),
            pl.BlockSpec((None, D, IN_TN), lambda b, i, j: (l, 0, j)),
        ],
        out_specs=pl.BlockSpec((1, tm, IN_TN), lambda b, i, j: (b, i, j)),
        scratch_shapes=[pltpu.VMEM((tm, D), BF16)],
        compiler_params=_cparams(3),
        name="in_proj",
    )(x, mod, mod, g, w)


def _halo_specs(T, tt, width, col_block):
    r = tt // HALO
    nh = T // HALO
    return [
        pl.BlockSpec((1, HALO, width), lambda b, c: (b, jnp.maximum(c * r - 1, 0), col_block)),
        pl.BlockSpec((1, tt, width), lambda b, c: (b, c, col_block)),
        pl.BlockSpec((1, HALO, width), lambda b, c: (b, jnp.minimum((c + 1) * r, nh - 1), col_block)),
    ]


def _pool_kernel(prev_ref, cur_ref, next_ref, w_ref, s_ref, o_ref, pad_ref, *, T, tt):
    c = pl.program_id(1)
    H = HALO
    pad_ref[0:H, :] = jnp.where(c > 0, prev_ref[0].astype(F32), 0.0)
    pad_ref[H:H + tt, :] = cur_ref[0].astype(F32)
    pad_ref[H + tt:H + tt + H, :] = jnp.where(c < pl.num_programs(1) - 1, next_ref[0].astype(F32), 0.0)
    rs = min(tt, 256)
    for r0 in range(0, tt, rs):
        t = c * tt + r0 + lax.broadcasted_iota(jnp.int32, (rs, HEAD_DIM), 0)
        for gi, w in enumerate(POOL_WINDOWS):
            cols = slice(gi * HEAD_DIM, (gi + 1) * HEAD_DIM)
            base = r0 + H
            acc = pad_ref[base - w // 2:base - w // 2 + rs, cols]
            for d in range(-w // 2 + 1, w // 2):
                acc = acc + pad_ref[base + d:base + d + rs, cols]
            cnt = jnp.minimum(t + w // 2, T) - jnp.maximum(t - w // 2, 0)
            p = acc / cnt.astype(F32) - pad_ref[base:base + rs, cols]
            y = jnp.dot(p.astype(BF16), w_ref[gi], preferred_element_type=F32)
            o_ref[0, r0:r0 + rs, cols] = (y * s_ref[:, cols]).astype(BF16)


def _pool_call(P, pool_w, pool_scale, l):
    B, T, _ = P.shape
    tt = min(T, 1024)
    return pl.pallas_call(
        functools.partial(_pool_kernel, T=T, tt=tt),
        out_shape=jax.ShapeDtypeStruct((B, T, GROUP_W), BF16),
        grid=(B, T // tt),
        in_specs=_halo_specs(T, tt, GROUP_W, COL_POOL * LANE // GROUP_W) + [
            _layer_spec(l, len(POOL_WINDOWS), HEAD_DIM, HEAD_DIM),
            _layer_spec(l, 1, GROUP_W),
        ],
        out_specs=pl.BlockSpec((1, tt, GROUP_W), lambda b, c: (b, c, 0)),
        scratch_shapes=[pltpu.VMEM((tt + 2 * HALO, GROUP_W), F32)],
        compiler_params=_cparams(2),
        name="pool_mixer",
    )(P, P, P, pool_w, pool_scale)


def _log_decay(z):
    t = jnp.exp2(jnp.abs(z) * (-math.log2(math.e)))
    return jnp.minimum(z, 0.0) * (1.0 / GLA_GATE_NORM) - jnp.log2(1.0 + t) * (math.log(2.0) / GLA_GATE_NORM)


def _gla_block_consts(rb):
    ri = lax.broadcasted_iota(jnp.int32, (rb, rb), 0)
    ci = lax.broadcasted_iota(jnp.int32, (rb, rb), 1)
    shift = GLA_CHUNK.bit_length() - 1
    same = jnp.right_shift(ri, shift) == jnp.right_shift(ci, shift)
    out = []
    for causal in (ci <= ri, ci >= ri):
        tri = jnp.where(causal, jnp.where(same, 1.0, 0.0), 0.0)
        out.append((tri.astype(BF16), tri > 0.0))
    return out


def _gla_prepare(items, consts, q_ref, k_ref, v_ref, g_ref, qg_ref, u_ref, dec_ref, o_ref):
    C = GLA_CHUNK
    rb = consts[0][1].shape[0]
    nchunk = rb // C
    Gs = []
    for d, rows, _ in items:
        tri = consts[d][0]
        g = g_ref[d, rows, :]
        g_hi = g.astype(BF16)
        g_lo = (g - g_hi.astype(F32)).astype(BF16)
        GG = jnp.dot(tri, jnp.concatenate([g_hi, g_lo], axis=1), preferred_element_type=F32)
        Gs.append(GG[:, :LANE] + GG[:, LANE:])
    staged = []
    for (d, rows, c0), G in zip(items, Gs):
        tot = C - 1 if d == 0 else 0
        tots = [G[ci * C + tot:ci * C + tot + 1, :] for ci in range(nchunk)]
        Gt = jnp.concatenate([jnp.broadcast_to(t, (C, LANE)) for t in tots], axis=0)
        r = 0.5 * Gt
        q = q_ref[0, rows, :].astype(F32) * (HEAD_DIM ** -0.5)
        k = k_ref[0, rows, :].astype(F32)
        qg = (q * jnp.exp(G - r)).astype(BF16)
        kg = (k * jnp.exp(r - G)).astype(BF16)
        qg_ref[d, rows, :] = (q * jnp.exp(G)).astype(BF16)
        kd = (k * jnp.exp(Gt - G)).astype(BF16)
        for ci in range(nchunk):
            dec_ref[d, c0 + ci] = jnp.exp(jnp.broadcast_to(tots[ci], (8, LANE)))
        staged.append((qg, kg, kd))
    atts = [lax.dot_general(qg, kg, NT_DIMS, preferred_element_type=F32) for qg, kg, _ in staged]
    for (d, rows, c0), att, (_, _, kd) in zip(items, atts, staged):
        v = v_ref[0, rows, :]
        att = jnp.where(consts[d][1], att, 0.0).astype(BF16)
        o_ref[d, rows, :] = jnp.dot(att, v, preferred_element_type=F32)
        for ci in range(nchunk):
            cr = slice(ci * C, (ci + 1) * C)
            u_ref[d, c0 + ci] = lax.dot_general(v[cr], kd[cr], TN_DIMS, preferred_element_type=F32)


def _gla_kernel(ql, kl, vl, gtl, lowl, qc, kc, vc, gtc, lowc, upf, bf, upb, bb, ng,
                ol_ref, oc_ref, gl, qgl, ul, decl, sbl, osl, gc, qgc, uc, decc, sbc, osc):
    C = GLA_CHUNK
    T = ql.shape[1]
    Tc = qc.shape[1]

    def gates(low_ref, g_ref):
        up = jnp.concatenate([upf[...], upb[...]], axis=1)
        z = jnp.dot(low_ref[0], up, preferred_element_type=F32)
        g_ref[0] = _log_decay(z[:, :LANE] + bf[...])
        g_ref[1] = _log_decay(z[:, LANE:] + bb[...])

    def prepare(q_ref, k_ref, v_ref, g_ref, qg_ref, u_ref, dec_ref, o_ref, Tx):
        rb = min(Tx, 256)
        nb = Tx // rb
        per = 4 if nb % 4 == 0 else (2 if nb % 2 == 0 else 1)
        consts = _gla_block_consts(rb)

        def body(j, carry):
            items = []
            for p in range(per):
                blk = j * per + p
                rows = pl.ds(pl.multiple_of(blk * rb, rb), rb)
                items += [(d, rows, blk * (rb // C)) for d in (0, 1)]
            _gla_prepare(items, consts, q_ref, k_ref, v_ref, g_ref, qg_ref, u_ref, dec_ref, o_ref)
            return carry

        lax.fori_loop(0, nb // per, body, 0)

    def recur(u_ref, dec_ref, sb_ref, n, Sf, Sb):
        def step(d, c, S):
            sb_ref[d, c] = S.astype(BF16)
            return S * dec_ref[d, c, 0:1, :] + u_ref[d, c]

        def body(i, carry):
            Sf, Sb = carry
            return step(0, i, Sf), step(1, n - 1 - i, Sb)

        return lax.fori_loop(0, n, body, (Sf, Sb), unroll=min(n, 4))

    def inter(qg_ref, sb_ref, o_ref, n):
        grp = min(n, 4)
        nb = n // grp
        per = 4 if nb % 4 == 0 else (2 if nb % 2 == 0 else 1)

        def body(j, carry):
            work = []
            for p in range(per):
                blk = j * per + p
                rows = pl.ds(pl.multiple_of(blk * grp * C, grp * C), grp * C)
                for d in (0, 1):
                    qg = qg_ref[d, rows, :]
                    parts = [lax.dot_general(qg[ci * C:(ci + 1) * C], sb_ref[d, blk * grp + ci], NT_DIMS,
                                             preferred_element_type=F32) for ci in range(grp)]
                    work.append((d, rows, parts))
            for d, rows, parts in work:
                o_ref[d, rows, :] += jnp.concatenate(parts, axis=0)
            return carry

        lax.fori_loop(0, n // (grp * per), body, 0)

    def finish(o_s, gt_ref, o_ref):
        o = _rms(o_s[0] + o_s[1], ng[...])
        o_ref[0] = (o * _silu(gt_ref[0].astype(F32))).astype(BF16)

    gates(lowc, gc)
    gates(lowl, gl)
    prepare(qc, kc, vc, gc, qgc, uc, decc, osc, Tc)
    prepare(ql, kl, vl, gl, qgl, ul, decl, osl, T)
    S0 = jnp.zeros((HEAD_DIM, HEAD_DIM), F32)
    Sf, Sb = recur(uc, decc, sbc, Tc // C, S0, S0)
    recur(ul, decl, sbl, T // C, Sf, Sb)
    inter(qgc, sbc, osc, Tc // C)
    inter(qgl, sbl, osl, T // C)
    finish(osl, gtl, ol_ref)
    finish(osc, gtc, oc_ref)


def _gla_call(Pl, Pc, upf, bias_f, upb, bias_b, norm_g, l):
    B, T, _ = Pl.shape
    Tc = Pc.shape[1]

    def colspec(Tx, col):
        return pl.BlockSpec((1, Tx, LANE), lambda b, h: (b, 0, col + h))

    def lowspec(Tx):
        return pl.BlockSpec((1, Tx, LANE), lambda b, h: (b, 0, COL_LOW))

    headw = pl.BlockSpec((None, LANE, LANE), lambda b, h: (l, 0, h))
    headv = pl.BlockSpec((None, 1, LANE), lambda b, h: (l, 0, h))
    outspec = lambda Tx: pl.BlockSpec((1, Tx, LANE), lambda b, h: (b, 0, h))

    def scratch(Tx):
        n = Tx // GLA_CHUNK
        return [pltpu.VMEM((2, Tx, LANE), F32),
                pltpu.VMEM((2, Tx, LANE), BF16),
                pltpu.VMEM((2, n, HEAD_DIM, HEAD_DIM), F32),
                pltpu.VMEM((2, n, 8, LANE), F32),
                pltpu.VMEM((2, n, HEAD_DIM, HEAD_DIM), BF16),
                pltpu.VMEM((2, Tx, LANE), F32)]

    return pl.pallas_call(
        _gla_kernel,
        out_shape=(jax.ShapeDtypeStruct((B, T, GROUP_W), BF16), jax.ShapeDtypeStruct((B, Tc, GROUP_W), BF16)),
        grid=(B, N_HEADS),
        in_specs=[colspec(T, COL_GQ), colspec(T, COL_GK), colspec(T, COL_GV), colspec(T, COL_GGATE), lowspec(T),
                  colspec(Tc, COL_GQ), colspec(Tc, COL_GK), colspec(Tc, COL_GV), colspec(Tc, COL_GGATE), lowspec(Tc),
                  headw, headv, headw, headv, _layer_spec(l, 1, LANE)],
        out_specs=(outspec(T), outspec(Tc)),
        scratch_shapes=scratch(T) + scratch(Tc),
        compiler_params=_cparams(2),
        name="gla_mixer",
    )(Pl, Pl, Pl, Pl, Pl, Pc, Pc, Pc, Pc, Pc, upf, bias_f, upb, bias_b, norm_g)


def _conv_kernel(prev_ref, cur_ref, next_ref, dw_ref, dwb_ref, lng_ref, lnb_ref, pw_ref, pwb_ref,
                 o_ref, pad_ref, sh_ref, acc_ref, *, tt):
    c = pl.program_id(1)
    H = HALO
    W = GROUP_W

    def glu(u):
        u = u.astype(F32)
        return u[:, :W] * jax.nn.sigmoid(u[:, W:])

    pad_ref[0:H, :] = jnp.where(c > 0, glu(prev_ref[0]), 0.0)
    pad_ref[H:H + tt, :] = glu(cur_ref[0])
    pad_ref[H + tt:H + tt + H, :] = jnp.where(c < pl.num_programs(1) - 1, glu(next_ref[0]), 0.0)

    n_sh = tt + 2 * H - SUBLANE
    sh_ref[0, :, :] = pad_ref[...]
    for r in range(1, SUBLANE):
        sh_ref[r, 0:n_sh, :] = pad_ref[r:r + n_sh, :]

    rs = min(tt, 128)
    off = H - CONV_K // 2
    for r0 in range(0, tt, rs):
        for lb in range(W // LANE):
            cols = slice(lb * LANE, (lb + 1) * LANE)
            acc = jnp.zeros((rs, LANE), F32) + dwb_ref[:, cols]
            for k in range(CONV_K):
                r = (off + k) % SUBLANE
                a0 = r0 + off + k - r
                acc = acc + sh_ref[r, a0:a0 + rs, cols] * dw_ref[k:k + 1, cols]
            acc_ref[r0:r0 + rs, cols] = acc

    h = acc_ref[...]
    mu = jnp.mean(h, axis=-1, keepdims=True)
    hc = h - mu
    var = jnp.mean(hc * hc, axis=-1, keepdims=True)
    y = hc * lax.rsqrt(var + NORM_EPS) * lng_ref[...] + lnb_ref[...]
    y = jnp.dot(_silu(y).astype(BF16), pw_ref[...], preferred_element_type=F32) + pwb_ref[...]
    o_ref[0] = y.astype(BF16)


def _conv_call(P, dw, dw_b, ln_g, ln_b, pw, pw_b, l):
    B, T, _ = P.shape
    tt = min(T, 512)
    vec = _layer_spec(l, 1, GROUP_W)
    return pl.pallas_call(
        functools.partial(_conv_kernel, tt=tt),
        out_shape=jax.ShapeDtypeStruct((B, T, GROUP_W), BF16),
        grid=(B, T // tt),
        in_specs=_halo_specs(T, tt, 2 * GROUP_W, 0) + [
            _layer_spec(l, CONV_K, GROUP_W), vec, vec, vec, _layer_spec(l, GROUP_W, GROUP_W), vec],
        out_specs=pl.BlockSpec((1, tt, GROUP_W), lambda b, c: (b, c, 0)),
        scratch_shapes=[pltpu.VMEM((tt + 2 * HALO, GROUP_W), F32),
                        pltpu.VMEM((SUBLANE, tt + 2 * HALO, GROUP_W), F32),
                        pltpu.VMEM((tt, GROUP_W), F32)],
        compiler_params=_cparams(2),
        name="conv_mixer",
    )(P, P, P, dw, dw_b, ln_g, ln_b, pw, pw_b)


def _rope(x, cos, sin):
    hw = ROPE_AXIS_DIM // 2
    lane = lax.broadcasted_iota(jnp.int32, x.shape, 1)
    first_half = (lane % (2 * hw)) < hw
    swapped = jnp.where(first_half, pltpu.roll(x, LANE - hw, 1), pltpu.roll(x, hw, 1))
    return x * cos + swapped * sin


def _diff_kernel(*refs, rope, lam_init, n_kv):
    it = iter(refs)
    q_ref = next(it)
    kv = [(next(it), next(it)) for _ in range(n_kv)]
    if rope:
        cq, sq, ck, sk = next(it), next(it), next(it), next(it)
    lq1, lk1, lq2, lk2, sg = next(it), next(it), next(it), next(it), next(it)
    o_ref, kbuf, vbuf = next(it), next(it), next(it)

    @pl.when(pl.program_id(2) == 0)
    def _():
        r0 = 0
        for i, (k_ref, v_ref) in enumerate(kv):
            n = k_ref.shape[1]
            k = k_ref[0]
            if rope and i == 0:
                k = _rope(k.astype(F32), ck[...], sk[...]).astype(BF16)
            kbuf[r0:r0 + n, :] = k
            vbuf[r0:r0 + n, 0:LANE] = v_ref[0]
            r0 += n
        vbuf[:, LANE:2 * LANE] = jnp.ones((vbuf.shape[0], LANE), BF16)

    tq = q_ref.shape[1]
    rs = min(tq, 256)
    k = kbuf[...]
    v1 = vbuf[...]
    lam = (jnp.exp(jnp.sum(lq1[...] * lk1[...], axis=-1, keepdims=True))
           - jnp.exp(jnp.sum(lq2[...] * lk2[...], axis=-1, keepdims=True)) + lam_init)
    scores = []
    for r0 in range(0, tq, rs):
        q = q_ref[0, r0:r0 + rs, :].astype(F32)
        if rope:
            q = _rope(q, cq[r0:r0 + rs, :], sq[r0:r0 + rs, :])
        q = q * (DIFF_QK_DIM ** -0.5 * math.log2(math.e))
        lane = lax.broadcasted_iota(jnp.int32, q.shape, 1)
        q1 = jnp.where(lane < DIFF_QK_DIM, q, 0.0).astype(BF16)
        q2 = jnp.where(lane >= DIFF_QK_DIM, q, 0.0).astype(BF16)
        scores.append((lax.dot_general(q1, k, NT_DIMS, preferred_element_type=F32),
                       lax.dot_general(q2, k, NT_DIMS, preferred_element_type=F32)))
    for r0, (s1, s2) in zip(range(0, tq, rs), scores):
        e1 = jnp.exp2(s1 - jnp.max(s1, axis=-1, keepdims=True)).astype(BF16)
        e2 = jnp.exp2(s2 - jnp.max(s2, axis=-1, keepdims=True)).astype(BF16)
        r1 = jnp.dot(e1, v1, preferred_element_type=F32)
        r2 = jnp.dot(e2, v1, preferred_element_type=F32)
        o = r1[:, :LANE] * (1.0 / r1[:, LANE:LANE + 1]) - r2[:, :LANE] * (lam / r2[:, LANE:LANE + 1])
        o_ref[0, r0:r0 + rs, :] = (_rms(o, sg[...]) * (1.0 - lam_init)).astype(BF16)


def _diff_call(Pq, kv_sources, rope_tabs, lq1, lk1, lq2, lk2, subln_g, lam_init, l):
    B, T, _ = Pq.shape
    tq = min(T, 1024)
    rope = rope_tabs is not None
    in_specs = [pl.BlockSpec((1, tq, LANE), lambda b, h, i: (b, i, COL_DQ + h))]
    args = [Pq]
    Tk = 0
    for Ps in kv_sources:
        n = Ps.shape[1]
        in_specs.append(pl.BlockSpec((1, n, LANE), lambda b, h, i: (b, 0, COL_DK + h)))
        in_specs.append(pl.BlockSpec((1, n, LANE), lambda b, h, i: (b, 0, COL_DV + h)))
        args += [Ps, Ps]
        Tk += n
    if rope:
        cos, sin = rope_tabs
        in_specs += [pl.BlockSpec((tq, LANE), lambda b, h, i: (i, 0)), pl.BlockSpec((tq, LANE), lambda b, h, i: (i, 0)),
                     pl.BlockSpec((T, LANE), lambda b, h, i: (0, 0)), pl.BlockSpec((T, LANE), lambda b, h, i: (0, 0))]
        args += [cos, sin, cos, sin]
    small = _layer_spec(l, 1, DIFF_QK_DIM)
    in_specs += [small, small, small, small, _layer_spec(l, 1, LANE)]
    args += [lq1, lk1, lq2, lk2, subln_g]
    return pl.pallas_call(
        functools.partial(_diff_kernel, rope=rope, lam_init=lam_init, n_kv=len(kv_sources)),
        out_shape=jax.ShapeDtypeStruct((B, T, GROUP_W), BF16),
        grid=(B, N_HEADS, T // tq),
        in_specs=in_specs,
        out_specs=pl.BlockSpec((1, tq, LANE), lambda b, h, i: (b, i, h)),
        scratch_shapes=[pltpu.VMEM((Tk, LANE), BF16), pltpu.VMEM((Tk, 2 * LANE), BF16)],
        compiler_params=_cparams(3),
        name="diff_attn",
    )(*args)


def _outproj_kernel(a_ref, b_ref, c_ref, d_ref, w_ref, x_ref, gate_ref, sh_ref, sc_ref, ng_ref, wr_ref,
                    x1_ref, h2_ref, aff_ref, mix_ref):
    W = GROUP_W
    tm = x_ref.shape[1]
    rs = min(tm, 256)
    for p, m_ref in enumerate((a_ref, b_ref, c_ref, d_ref)):
        mix_ref[:, p * W:(p + 1) * W] = m_ref[0]
    ys = [jnp.dot(mix_ref[r0:r0 + rs, :], w_ref[...], preferred_element_type=F32) for r0 in range(0, tm, rs)]
    for r0, y in zip(range(0, tm, rs), ys):
        rows = slice(r0, r0 + rs)
        x1 = x_ref[0, rows, :] + gate_ref[...] * y
        x1_ref[0, rows, :] = x1
        h = _norm_mod(x1, ng_ref[...], sc_ref[...], sh_ref[...])
        hh = h.astype(BF16)
        hl = (h - hh.astype(F32)).astype(BF16)
        h2_ref[0, rows, :] = hh
        lg2 = jnp.dot(hh, wr_ref[...], preferred_element_type=F32)
        lg = lg2[:, :LANE] + lg2[:, LANE:] + jnp.dot(hl, wr_ref[:, 0:LANE], preferred_element_type=F32)
        lane = lax.broadcasted_iota(jnp.int32, lg.shape, 1)
        lg = jnp.where(lane < N_EXPERTS, lg, -jnp.inf)
        e = jnp.exp(lg - jnp.max(lg, axis=-1, keepdims=True))
        aff_ref[0, rows, :] = e / jnp.sum(e, axis=-1, keepdims=True)


def _outproj_call(mix, w_out, x, mod, row_of, ng, wr2, l):
    Bx, T, D = x.shape
    tm = min(T, 512)
    mixspec = pl.BlockSpec((1, tm, GROUP_W), lambda b, i: (b, i, 0))
    rowspec = pl.BlockSpec((1, tm, D), lambda b, i: (b, i, 0))
    return pl.pallas_call(
        _outproj_kernel,
        out_shape=(jax.ShapeDtypeStruct((Bx, T, D), F32), jax.ShapeDtypeStruct((Bx, T, D), BF16),
                   jax.ShapeDtypeStruct((Bx, T, LANE), F32)),
        grid=(Bx, T // tm),
        in_specs=[mixspec, mixspec, mixspec, mixspec,
                  _layer_spec(l, D, D),
                  rowspec, _mod_spec(l, row_of, 2), _mod_spec(l, row_of, 3), _mod_spec(l, row_of, 4),
                  _layer_spec(l, 1, D), _layer_spec(l, D, 2 * LANE)],
        out_specs=(rowspec, rowspec, pl.BlockSpec((1, tm, LANE), lambda b, i: (b, i, 0))),
        scratch_shapes=[pltpu.VMEM((tm, D), BF16)],
        compiler_params=_cparams(2),
        name="out_proj",
    )(*mix, w_out, x, mod, mod, mod, ng, wr2)


def _excl_prefix(x):
    rows, T = x.shape
    ri = lax.broadcasted_iota(jnp.int32, (LANE, LANE), 0)
    ci = lax.broadcasted_iota(jnp.int32, (LANE, LANE), 1)
    upper = jnp.where(ri <= ci, 1.0, 0.0).astype(BF16)
    carry = jnp.zeros((rows, 1), F32)
    out = []
    for b in range(T // LANE):
        xb = x[:, b * LANE:(b + 1) * LANE]
        inc = jnp.dot(xb.astype(BF16), upper, preferred_element_type=F32)
        out.append(inc - xb + carry)
        carry = carry + jnp.sum(xb, axis=1, keepdims=True)
    return jnp.concatenate(out, axis=1)


def _route_kernel(aff_ref, slot_row_ref, slot_col_ref, aff_row_ref, st_ref, *, cap):
    bt, T, _ = aff_ref.shape
    E = N_EXPERTS
    arow = jnp.concatenate([aff_ref[bi].T[0:E, :] for bi in range(bt)], axis=0)
    keys = lax.bitcast_convert_type(arow, jnp.int32)
    v = jnp.zeros((bt * E, 1), jnp.int32)
    for bit in range(30, -1, -1):
        cand = v | (1 << bit)
        cnt = jnp.sum(jnp.where(keys >= cand, 1.0, 0.0), axis=1, keepdims=True)
        v = jnp.where(cnt >= cap, cand, v)
    above = keys > v
    tied = jnp.where(keys == v, 1.0, 0.0)
    room = cap - jnp.sum(jnp.where(above, 1.0, 0.0), axis=1, keepdims=True)
    kept = jnp.where(above, 1.0, jnp.where(_excl_prefix(tied) < room, tied, 0.0))
    slot = jnp.where(kept > 0.0, _excl_prefix(kept), float(T))
    st_ref[...] = jnp.full(st_ref.shape, float(T), F32)
    for bi in range(bt):
        for e in range(E):
            r = bi * E + e
            slot_row_ref[bi, e] = slot[r:r + 1, :].astype(jnp.int32)
            aff_row_ref[bi, e] = arow[r:r + 1, :]
        st_ref[0:E, :] = slot[bi * E:(bi + 1) * E, :]
        slot_col_ref[bi] = st_ref[...].T.astype(jnp.int32)


def _route_call(aff, cap):
    B, T, _ = aff.shape
    E = N_EXPERTS
    bt = 4 if B % 4 == 0 else (2 if B % 2 == 0 else 1)
    return pl.pallas_call(
        functools.partial(_route_kernel, cap=cap),
        out_shape=(jax.ShapeDtypeStruct((B, E, 1, T), jnp.int32),
                   jax.ShapeDtypeStruct((B, T, LANE), jnp.int32),
                   jax.ShapeDtypeStruct((B, E, 1, T), F32)),
        grid=(B // bt,),
        in_specs=[pl.BlockSpec((bt, T, LANE), lambda b: (b, 0, 0))],
        out_specs=(pl.BlockSpec((bt, E, 1, T), lambda b: (b, 0, 0, 0)),
                   pl.BlockSpec((bt, T, LANE), lambda b: (b, 0, 0)),
                   pl.BlockSpec((bt, E, 1, T), lambda b: (b, 0, 0, 0))),
        scratch_shapes=[pltpu.VMEM((LANE, T), F32)],
        compiler_params=_cparams(1),
        name="router_route",
    )(aff)


def _dispatch_kernel(slot_ref, affr_ref, h_ref, xe_ref, gs_ref):
    eg, cap, D = xe_ref.shape
    T = h_ref.shape[1]
    slot = lax.broadcasted_iota(jnp.int32, (cap, T), 0)
    hits = [slot_ref[0, e] == slot for e in range(eg)]
    onehot = jnp.concatenate([jnp.where(h, 1.0, 0.0).astype(BF16) for h in hits], axis=0)
    x = jnp.dot(onehot, h_ref[0], preferred_element_type=F32).astype(BF16)
    xe_ref[...] = x.reshape(eg, cap, D)
    for e in range(eg):
        g = jnp.sum(jnp.where(hits[e], affr_ref[0, e], 0.0), axis=-1, keepdims=True)
        gs_ref[e] = jnp.broadcast_to(g, (cap, LANE))


def _dispatch_call(slot_row, aff_row, h2, cap):
    B, T, D = h2.shape
    E = N_EXPERTS
    eg = max(1, min(E, 1024 // cap))
    return pl.pallas_call(
        _dispatch_kernel,
        out_shape=(jax.ShapeDtypeStruct((E, B * cap, D), BF16), jax.ShapeDtypeStruct((E, B * cap, LANE), F32)),
        grid=(B, E // eg),
        in_specs=[pl.BlockSpec((1, eg, 1, T), lambda b, g: (b, g, 0, 0)),
                  pl.BlockSpec((1, eg, 1, T), lambda b, g: (b, g, 0, 0)),
                  pl.BlockSpec((1, T, D), lambda b, g: (b, 0, 0))],
        out_specs=(pl.BlockSpec((eg, cap, D), lambda b, g: (g, b, 0)),
                   pl.BlockSpec((eg, cap, LANE), lambda b, g: (g, b, 0))),
        compiler_params=_cparams(2),
        name="moe_dispatch",
    )(slot_row, aff_row, h2)


def _expert_kernel(*refs, n_groups, nf):
    xs = refs[:n_groups]
    wg_ref, wu_ref, wd_ref = refs[n_groups:n_groups + 3]
    gss = refs[n_groups + 3:2 * n_groups + 3]
    ys = refs[2 * n_groups + 3:3 * n_groups + 3]
    hms = refs[3 * n_groups + 3:]
    s = pl.program_id(1)
    tf = wg_ref.shape[-1]

    @pl.when(s < nf)
    def _():
        wg = wg_ref[...].astype(BF16)
        wu = wu_ref[...].astype(BF16)
        for x_ref, hm_ref in zip(xs, hms):
            x = x_ref[...]
            a = jnp.dot(x, wg, preferred_element_type=F32)
            u = jnp.dot(x, wu, preferred_element_type=F32)
            hm_ref[s] = (_silu(a) * u).astype(BF16)

    @pl.when(s >= nf)
    def _():
        for hm_ref, gs_ref, y_ref in zip(hms, gss, ys):
            acc = jnp.dot(hm_ref[0], wd_ref[0:tf, :].astype(BF16), preferred_element_type=F32)
            for f in range(1, nf):
                acc = acc + jnp.dot(hm_ref[f], wd_ref[f * tf:(f + 1) * tf, :].astype(BF16),
                                    preferred_element_type=F32)
            y_ref[...] = (acc * gs_ref[...][:, 0:1]).astype(BF16)


def _expert_call(groups, w_gate, w_up, w_down, l):
    E, _, D = groups[0][0].shape
    F = w_gate.shape[-1]
    tf = 256
    tn = 512
    nf = F // tf
    nn = D // tn
    n = len(groups)
    Ms = [g[0].shape[1] for g in groups]
    nidx = lambda s: jnp.maximum(s - nf, 0)
    x_e = lambda e, s: jnp.minimum(e + (s >= nf).astype(jnp.int32), E - 1)
    w_e = lambda e, s: jnp.minimum(e + (s > nf).astype(jnp.int32), E - 1)
    w_f = lambda s: jnp.where(s > nf, 0, jnp.minimum(s, nf - 1))
    in_specs = ([pl.BlockSpec((None, M, D), lambda e, s: (x_e(e, s), 0, 0)) for M in Ms]
                + [pl.BlockSpec((None, None, D, tf), lambda e, s: (l, w_e(e, s), 0, w_f(s))),
                   pl.BlockSpec((None, None, D, tf), lambda e, s: (l, w_e(e, s), 0, w_f(s))),
                   pl.BlockSpec((None, None, F, tn), lambda e, s: (l, e, 0, nidx(s)))]
                + [pl.BlockSpec((None, M, LANE), lambda e, s: (e, 0, 0)) for M in Ms])
    return pl.pallas_call(
        functools.partial(_expert_kernel, n_groups=n, nf=nf),
        out_shape=[jax.ShapeDtypeStruct((E, M, D), BF16) for M in Ms],
        grid=(E, nf + nn),
        in_specs=in_specs,
        out_specs=[pl.BlockSpec((None, M, tn), lambda e, s: (e, 0, nidx(s))) for M in Ms],
        scratch_shapes=[pltpu.VMEM((nf, M, tf), BF16) for M in Ms],
        compiler_params=_cparams(2),
        name="moe_experts",
    )(*[g[0] for g in groups], w_gate, w_up, w_down, *[g[1] for g in groups])


def _combine_kernel(slot_ref, y_ref, x_ref, gate_ref, fg_ref, o_ref, *, final_norm):
    E, cap, D = y_ref.shape
    tm = x_ref.shape[1]
    slot = slot_ref[0]
    if E * cap <= 4 * MXU_N and cap & (cap - 1) == 0 and cap % 16 == 0:
        K = E * cap
        shift = cap.bit_length() - 1
        col = lax.broadcasted_iota(jnp.int32, (LANE, K), 1)
        row = lax.broadcasted_iota(jnp.int32, (LANE, K), 0)
        expand = jnp.where(jnp.right_shift(col, shift) == row, 1.0, 0.0).astype(BF16)
        spread = jnp.dot(slot.astype(F32).astype(BF16), expand, preferred_element_type=F32)
        want = (lax.broadcasted_iota(jnp.int32, (tm, K), 1) & (cap - 1)).astype(F32)
        hit = jnp.where(spread == want, 1.0, 0.0).astype(BF16)
        acc = jnp.dot(hit, y_ref[...].reshape(K, D), preferred_element_type=F32)
    else:
        lane_slot = lax.broadcasted_iota(jnp.int32, (tm, cap), 1)
        acc = jnp.zeros(x_ref.shape[1:], F32)
        for e in range(E):
            hit = jnp.where(slot[:, e:e + 1] == lane_slot, 1.0, 0.0).astype(BF16)
            acc = acc + jnp.dot(hit, y_ref[e], preferred_element_type=F32)
    x2 = x_ref[0] + gate_ref[...] * acc
    if final_norm:
        x2 = _rms(x2, fg_ref[...])
    o_ref[0] = x2


def _combine_call(slot_col, y, x1, mod, row_of, final_g, cap, final_norm, l):
    B, T, D = x1.shape
    tm = min(T, 512)
    return pl.pallas_call(
        functools.partial(_combine_kernel, final_norm=final_norm),
        out_shape=jax.ShapeDtypeStruct((B, T, D), F32),
        grid=(B, T // tm),
        in_specs=[pl.BlockSpec((1, tm, LANE), lambda b, i: (b, i, 0)),
                  pl.BlockSpec((N_EXPERTS, cap, D), lambda b, i: (0, b, 0)),
                  pl.BlockSpec((1, tm, D), lambda b, i: (b, i, 0)),
                  _mod_spec(l, row_of, 5),
                  pl.BlockSpec((1, D), lambda b, i: (0, 0))],
        out_specs=pl.BlockSpec((1, tm, D), lambda b, i: (b, i, 0)),
        compiler_params=_cparams(2),
        name="moe_combine",
    )(slot_col, y, x1, mod, final_g)


def _rope_tables(n):
    rows = n // GRID_W
    row = jnp.repeat(jnp.arange(rows), GRID_W).astype(F32)
    col = jnp.tile(jnp.arange(GRID_W), rows).astype(F32)
    inv_freq = ROPE_BASE ** (-jnp.arange(0, ROPE_AXIS_DIM, 2, dtype=F32) / ROPE_AXIS_DIM)
    ar = row[:, None] * inv_freq
    ac = col[:, None] * inv_freq
    cos = jnp.concatenate([jnp.cos(ar), jnp.cos(ar), jnp.cos(ac), jnp.cos(ac)], axis=-1)
    sin = jnp.concatenate([-jnp.sin(ar), jnp.sin(ar), -jnp.sin(ac), jnp.sin(ac)], axis=-1)
    return jnp.tile(cos, (1, 2)), jnp.tile(sin, (1, 2))


def _reorder_kernel(w_ref, o_ref):
    G = GROUP_W
    low = 2 * GLA_GATE_RANK
    o = 0
    for src, n in ((5 * G + low, 2 * G), (0, 5 * G), (7 * G + low, 3 * G), (5 * G, low)):
        o_ref[:, o:o + n] = w_ref[:, src:src + n]
        o += n
    o_ref[:, o:] = jnp.zeros((o_ref.shape[0], o_ref.shape[1] - o), BF16)


def _reorder_w_in(w_in):
    L, D, W = w_in.shape
    tr = 256
    return pl.pallas_call(
        _reorder_kernel,
        out_shape=jax.ShapeDtypeStruct((L, D, IN_COLS), BF16),
        grid=(L, D // tr),
        in_specs=[pl.BlockSpec((None, tr, W), lambda l, i: (l, i, 0))],
        out_specs=pl.BlockSpec((None, tr, IN_COLS), lambda l, i: (l, i, 0)),
        compiler_params=_cparams(2),
        name="w_in_reorder",
    )(w_in)


def kernel(x, c, ctx, c_ctx, w_ada, b_ada, norm1_g, norm2_g, w_in, pool_w, pool_scale, gla_gk_up_f, gla_gk_bias_f, gla_gk_up_b, gla_gk_bias_b, gla_norm_g, conv_dw, conv_dw_b, conv_ln_g, conv_ln_b, conv_pw, conv_pw_b, diff_lq1, diff_lk1, diff_lq2, diff_lk2, diff_subln_g, w_out, w_router, w_exp_gate, w_exp_up, w_exp_down, final_norm_g):
    B, N, D = x.shape
    Tc = ctx.shape[1]
    L = w_ada.shape[0]
    assert D == D_MODEL and N % 256 == 0 and Tc % 128 == 0 and B < MOD_ROWS

    cc = jnp.concatenate([c, c_ctx[None, :], jnp.zeros((MOD_ROWS - B - 1, D), F32)], axis=0)
    mod = _ada_call(cc, w_ada, b_ada).reshape(L, MOD_ROWS, 6, 1, D)
    lat_row = lambda b, *_: b
    ctx_row = lambda *_: B

    rows3 = lambda a: a.reshape(L, 1, -1)
    w_in_r = _reorder_w_in(w_in.astype(BF16))
    w_out_b = w_out.astype(BF16)
    pool_w_b = pool_w.astype(BF16)
    conv_pw_b16 = conv_pw.astype(BF16)
    R = GLA_GATE_RANK
    zpad = lambda a, lo: jnp.pad(a, ((0, 0), (lo, LANE - R - lo), (0, 0))).astype(BF16)
    upf = zpad(gla_gk_up_f, 0)
    upb = zpad(gla_gk_up_b, R)
    wr = jnp.pad(w_router, ((0, 0), (0, 0), (0, LANE - N_EXPERTS)))
    wr_hi = wr.astype(BF16)
    wr2 = jnp.concatenate([wr_hi, (wr - wr_hi.astype(F32)).astype(BF16)], axis=-1)
    rope_tabs = _rope_tables(N)
    fg = final_norm_g.reshape(1, D)
    n1, n2 = rows3(norm1_g), rows3(norm2_g)
    gla_args = (upf, rows3(gla_gk_bias_f), upb, rows3(gla_gk_bias_b), rows3(gla_norm_g))
    conv_args = (conv_dw, rows3(conv_dw_b), rows3(conv_ln_g), rows3(conv_ln_b), conv_pw_b16, rows3(conv_pw_b))
    diff_vecs = (rows3(diff_lq1), rows3(diff_lk1), rows3(diff_lq2), rows3(diff_lk2), rows3(diff_subln_g))
    pool_args = (pool_w_b, rows3(pool_scale))
    cap_l = EC_CAPACITY * N // N_EXPERTS
    cap_c = EC_CAPACITY * Tc // N_EXPERTS
    flat = lambda a: a.reshape(1, B * Tc, a.shape[-1])
    unflat = lambda a: a.reshape(B, Tc, a.shape[-1])

    for l in range(L):
        last = l == L - 1
        lam_init = 0.8 - 0.6 * math.exp(-0.3 * l)

        Pl = _inproj_call(x, mod, lat_row, n1, w_in_r, l)
        Pc = unflat(_inproj_call(flat(ctx), mod, ctx_row, n1, w_in_r, l))

        gla_l, gla_c = _gla_call(Pl, Pc, *gla_args, l)
        pool_l = _pool_call(Pl, *pool_args, l)
        conv_l = _conv_call(Pl, *conv_args, l)
        diff_l = _diff_call(Pl, [Pl, Pc], rope_tabs, *diff_vecs, lam_init, l)
        x1, h2, aff = _outproj_call((pool_l, gla_l, conv_l, diff_l), w_out_b, x, mod, lat_row, n2, wr2, l)
        slot_row, slot_col, aff_row = _route_call(aff, cap_l)
        groups = [_dispatch_call(slot_row, aff_row, h2, cap_l)]

        if not last:
            pool_c = _pool_call(Pc, *pool_args, l)
            conv_c = _conv_call(Pc, *conv_args, l)
            diff_c = _diff_call(Pc, [Pc], None, *diff_vecs, lam_init, l)
            c1, h2c, affc = _outproj_call((flat(pool_c), flat(gla_c), flat(conv_c), flat(diff_c)), w_out_b,
                                          flat(ctx), mod, ctx_row, n2, wr2, l)
            slot_row_c, slot_col_c, aff_row_c = _route_call(unflat(affc), cap_c)
            groups.append(_dispatch_call(slot_row_c, aff_row_c, unflat(h2c), cap_c))

        ys = _expert_call(groups, w_exp_gate, w_exp_up, w_exp_down, l)
        x = _combine_call(slot_col, ys[0], x1, mod, lat_row, fg, cap_l, last, l)
        if not last:
            ctx = _combine_call(slot_col_c, ys[1], unflat(c1), mod, ctx_row, fg, cap_c, False, l)

    return x
```

```python
import functools
import math

import jax
import jax.numpy as jnp
from jax import lax
from jax.experimental import pallas as pl
from jax.experimental.pallas import tpu as pltpu

F32 = jnp.float32
BF16 = jnp.bfloat16

D_MODEL = 2048
GRID_W = 64
GROUP_W = D_MODEL // 4
POOL_WINDOWS = (2, 4, 8, 16)
HEAD_DIM = 128
N_HEADS = GROUP_W // HEAD_DIM
GLA_GATE_RANK = 16
GLA_GATE_NORM = 16.0
GLA_CHUNK = 64
CONV_K = 31
DIFF_QK_DIM = 64
ROPE_BASE = 10000.0
ROPE_AXIS_DIM = DIFF_QK_DIM // 2
N_EXPERTS = 16
EXPERT_FF = D_MODEL // 2
EC_CAPACITY = 2
NORM_EPS = 1e-6

LANE = 128
SUBLANE = 8
HALO = 16
VMEM_LIMIT = 56 * 1024 * 1024
MOD_ROWS = 16

COL_CONV = 0
COL_POOL = 8
COL_GQ = 12
COL_GK = 16
COL_GV = 20
COL_GGATE = 24
COL_DQ = 28
COL_DK = 32
COL_DV = 36
COL_LOW = 40
MXU_N = 256
IN_COLS = 42 * LANE
IN_TN = 7 * MXU_N

NT_DIMS = (((1,), (1,)), ((), ()))
TN_DIMS = (((0,), (0,)), ((), ()))


def _cparams(n_axes):
    return pltpu.CompilerParams(dimension_semantics=("arbitrary",) * n_axes, vmem_limit_bytes=VMEM_LIMIT)


def _silu(x):
    return x * jax.nn.sigmoid(x)


def _rms(x, g):
    return x * lax.rsqrt(jnp.mean(x * x, axis=-1, keepdims=True) + NORM_EPS) * g


def _norm_mod(x, g, scale, shift):
    inv = lax.rsqrt(jnp.mean(x * x, axis=-1, keepdims=True) + NORM_EPS)
    return x * inv * (g * (1.0 + scale)) + shift


def _layer_spec(l, *tail):
    return pl.BlockSpec((None,) + tail, lambda *_: (l,) + (0,) * len(tail))


def _mod_spec(l, row_of, k):
    return pl.BlockSpec((None, None, None, 1, D_MODEL), lambda *g: (l, row_of(*g), k, 0, 0))


def _ada_kernel(c_ref, w_ref, b_ref, o_ref):
    sc = _silu(c_ref[...]).astype(BF16)
    o_ref[0] = jnp.dot(sc, w_ref[0].astype(BF16), preferred_element_type=F32) + b_ref[0]


def _ada_call(cc, w_ada, b_ada):
    L, D, W = w_ada.shape
    R = cc.shape[0]
    tn = 1536
    return pl.pallas_call(
        _ada_kernel,
        out_shape=jax.ShapeDtypeStruct((L, R, W), F32),
        grid=(L, W // tn),
        in_specs=[
            pl.BlockSpec((R, D), lambda l, j: (0, 0)),
            pl.BlockSpec((1, D, tn), lambda l, j: (l, 0, j)),
            pl.BlockSpec((1, 1, tn), lambda l, j: (l, 0, j)),
        ],
        out_specs=pl.BlockSpec((1, R, tn), lambda l, j: (l, 0, j)),
        compiler_params=_cparams(2),
        name="ada_mod",
    )(cc, w_ada, b_ada.reshape(L, 1, W))


def _inproj_kernel(x_ref, sh_ref, sc_ref, g_ref, w_ref, o_ref, h_ref):
    @pl.when(pl.program_id(2) == 0)
    def _():
        h_ref[...] = _norm_mod(x_ref[0], g_ref[...], sc_ref[...], sh_ref[...]).astype(BF16)

    o_ref[0] = jnp.dot(h_ref[...], w_ref[...], preferred_element_type=F32).astype(BF16)


def _inproj_call(x, mod, row_of, g, w, l):
    Bx, T, D = x.shape
    tm = min(T, 1024)
    return pl.pallas_call(
        _inproj_kernel,
        out_shape=jax.ShapeDtypeStruct((Bx, T, IN_COLS), BF16),
        grid=(Bx, T // tm, IN_COLS // IN_TN),
        in_specs=[
            pl.BlockSpec((1, tm, D), lambda b, i, j: (b, i, 0)),
            _mod_spec(l, row_of, 0),
            _mod_spec(l, row_of, 1),
            _layer_spec(l, 1, D),
            pl.BlockSpec((None, D, IN_TN), lambda b, i, j: (l, 0, j)),
        ],
        out_specs=pl.BlockSpec((1, tm, IN_TN), lambda b, i, j: (b, i, j)),
        scratch_shapes=[pltpu.VMEM((tm, D), BF16)],
        compiler_params=_cparams(3),
        name="in_proj",
    )(x, mod, mod, g, w)


def _halo_specs(T, tt, width, col_block):
    r = tt // HALO
    nh = T // HALO
    return [
        pl.BlockSpec((1, HALO, width), lambda b, c: (b, jnp.maximum(c * r - 1, 0), col_block)),
        pl.BlockSpec((1, tt, width), lambda b, c: (b, c, col_block)),
        pl.BlockSpec((1, HALO, width), lambda b, c: (b, jnp.minimum((c + 1) * r, nh - 1), col_block)),
    ]


def _pool_kernel(prev_ref, cur_ref, next_ref, w_ref, s_ref, o_ref, pad_ref, *, T, tt):
    c = pl.program_id(1)
    H = HALO
    pad_ref[0:H, :] = jnp.where(c > 0, prev_ref[0].astype(F32), 0.0)
    pad_ref[H:H + tt, :] = cur_ref[0].astype(F32)
    pad_ref[H + tt:H + tt + H, :] = jnp.where(c < pl.num_programs(1) - 1, next_ref[0].astype(F32), 0.0)
    rs = min(tt, 256)
    hw = max(POOL_WINDOWS) // 2
    ri = lax.broadcasted_iota(jnp.int32, (rs, rs + 2 * hw), 0)
    ci = lax.broadcasted_iota(jnp.int32, (rs, rs + 2 * hw), 1) - hw
    bands = [jnp.where((ci >= ri - w // 2) & (ci < ri + w // 2), 1.0, 0.0).astype(BF16) for w in POOL_WINDOWS]
    work = [(r0, gi) for r0 in range(0, tt, rs) for gi in range(len(POOL_WINDOWS))]
    sums = []
    for r0, gi in work:
        cols = slice(gi * HEAD_DIM, (gi + 1) * HEAD_DIM)
        win = pad_ref[r0 + H - hw:r0 + H + rs + hw, cols].astype(BF16)
        sums.append(jnp.dot(bands[gi], win, preferred_element_type=F32))
    for (r0, gi), acc in zip(work, sums):
        w = POOL_WINDOWS[gi]
        cols = slice(gi * HEAD_DIM, (gi + 1) * HEAD_DIM)
        t = c * tt + r0 + lax.broadcasted_iota(jnp.int32, (rs, HEAD_DIM), 0)
        cnt = jnp.minimum(t + w // 2, T) - jnp.maximum(t - w // 2, 0)
        p = acc / cnt.astype(F32) - pad_ref[r0 + H:r0 + H + rs, cols]
        y = jnp.dot(p.astype(BF16), w_ref[gi], preferred_element_type=F32)
        o_ref[0, r0:r0 + rs, cols] = (y * s_ref[:, cols]).astype(BF16)


def _pool_call(P, pool_w, pool_scale, l):
    B, T, _ = P.shape
    tt = min(T, 1024)
    return pl.pallas_call(
        functools.partial(_pool_kernel, T=T, tt=tt),
        out_shape=jax.ShapeDtypeStruct((B, T, GROUP_W), BF16),
        grid=(B, T // tt),
        in_specs=_halo_specs(T, tt, GROUP_W, COL_POOL * LANE // GROUP_W) + [
            _layer_spec(l, len(POOL_WINDOWS), HEAD_DIM, HEAD_DIM),
            _layer_spec(l, 1, GROUP_W),
        ],
        out_specs=pl.BlockSpec((1, tt, GROUP_W), lambda b, c: (b, c, 0)),
        scratch_shapes=[pltpu.VMEM((tt + 2 * HALO, GROUP_W), F32)],
        compiler_params=_cparams(2),
        name="pool_mixer",
    )(P, P, P, pool_w, pool_scale)


def _log_decay(z):
    t = jnp.exp2(jnp.abs(z) * (-math.log2(math.e)))
    return jnp.minimum(z, 0.0) * (1.0 / GLA_GATE_NORM) - jnp.log2(1.0 + t) * (math.log(2.0) / GLA_GATE_NORM)


def _gla_block_consts(rb):
    ri = lax.broadcasted_iota(jnp.int32, (rb, rb), 0)
    ci = lax.broadcasted_iota(jnp.int32, (rb, rb), 1)
    shift = GLA_CHUNK.bit_length() - 1
    same = jnp.right_shift(ri, shift) == jnp.right_shift(ci, shift)
    out = []
    for causal in (ci <= ri, ci >= ri):
        tri = jnp.where(causal, jnp.where(same, 1.0, 0.0), 0.0)
        out.append((tri.astype(BF16), tri > 0.0))
    return out


def _gla_prepare(items, consts, q_ref, k_ref, v_ref, g_ref, qg_ref, u_ref, dec_ref, o_ref):
    C = GLA_CHUNK
    rb = consts[0][1].shape[0]
    nchunk = rb // C
    Gs = []
    for d, rows, _ in items:
        tri = consts[d][0]
        g = g_ref[d, rows, :]
        g_hi = g.astype(BF16)
        g_lo = (g - g_hi.astype(F32)).astype(BF16)
        GG = jnp.dot(tri, jnp.concatenate([g_hi, g_lo], axis=1), preferred_element_type=F32)
        Gs.append(GG[:, :LANE] + GG[:, LANE:])
    staged = []
    for (d, rows, c0), G in zip(items, Gs):
        tot = C - 1 if d == 0 else 0
        tots = [G[ci * C + tot:ci * C + tot + 1, :] for ci in range(nchunk)]
        Gt = jnp.concatenate([jnp.broadcast_to(t, (C, LANE)) for t in tots], axis=0)
        r = 0.5 * Gt
        q = q_ref[0, rows, :].astype(F32) * (HEAD_DIM ** -0.5)
        k = k_ref[0, rows, :].astype(F32)
        qg = (q * jnp.exp(G - r)).astype(BF16)
        kg = (k * jnp.exp(r - G)).astype(BF16)
        qg_ref[d, rows, :] = (q * jnp.exp(G)).astype(BF16)
        kd = (k * jnp.exp(Gt - G)).astype(BF16)
        for ci in range(nchunk):
            dec_ref[d, c0 + ci] = jnp.exp(jnp.broadcast_to(tots[ci], (8, LANE)))
        staged.append((qg, kg, kd))
    atts = [lax.dot_general(qg, kg, NT_DIMS, preferred_element_type=F32) for qg, kg, _ in staged]
    for (d, rows, c0), att, (_, _, kd) in zip(items, atts, staged):
        v = v_ref[0, rows, :]
        att = jnp.where(consts[d][1], att, 0.0).astype(BF16)
        o_ref[d, rows, :] = jnp.dot(att, v, preferred_element_type=F32)
        for ci in range(nchunk):
            cr = slice(ci * C, (ci + 1) * C)
            u_ref[d, c0 + ci] = lax.dot_general(v[cr], kd[cr], TN_DIMS, preferred_element_type=F32)


def _gla_kernel(ql, kl, vl, gtl, lowl, qc, kc, vc, gtc, lowc, upf, bf, upb, bb, ng,
                ol_ref, oc_ref, gl, qgl, ul, decl, sbl, osl, gc, qgc, uc, decc, sbc, osc):
    C = GLA_CHUNK
    T = ql.shape[1]
    Tc = qc.shape[1]

    def gates(low_ref, g_ref):
        up = jnp.concatenate([upf[...], upb[...]], axis=1)
        z = jnp.dot(low_ref[0], up, preferred_element_type=F32)
        g_ref[0] = _log_decay(z[:, :LANE] + bf[...])
        g_ref[1] = _log_decay(z[:, LANE:] + bb[...])

    def prepare(q_ref, k_ref, v_ref, g_ref, qg_ref, u_ref, dec_ref, o_ref, Tx):
        rb = min(Tx, 256)
        nb = Tx // rb
        per = 4 if nb % 4 == 0 else (2 if nb % 2 == 0 else 1)
        consts = _gla_block_consts(rb)

        def body(j, carry):
            items = []
            for p in range(per):
                blk = j * per + p
                rows = pl.ds(pl.multiple_of(blk * rb, rb), rb)
                items += [(d, rows, blk * (rb // C)) for d in (0, 1)]
            _gla_prepare(items, consts, q_ref, k_ref, v_ref, g_ref, qg_ref, u_ref, dec_ref, o_ref)
            return carry

        lax.fori_loop(0, nb // per, body, 0)

    def recur(u_ref, dec_ref, sb_ref, n, Sf, Sb):
        def step(d, c, S):
            sb_ref[d, c] = S.astype(BF16)
            return S * dec_ref[d, c, 0:1, :] + u_ref[d, c]

        def body(i, carry):
            Sf, Sb = carry
            return step(0, i, Sf), step(1, n - 1 - i, Sb)

        return lax.fori_loop(0, n, body, (Sf, Sb), unroll=min(n, 4))

    def inter(qg_ref, sb_ref, o_ref, n):
        grp = min(n, 4)
        nb = n // grp
        per = 4 if nb % 4 == 0 else (2 if nb % 2 == 0 else 1)

        def body(j, carry):
            work = []
            for p in range(per):
                blk = j * per + p
                rows = pl.ds(pl.multiple_of(blk * grp * C, grp * C), grp * C)
                for d in (0, 1):
                    qg = qg_ref[d, rows, :]
                    parts = [lax.dot_general(qg[ci * C:(ci + 1) * C], sb_ref[d, blk * grp + ci], NT_DIMS,
                                             preferred_element_type=F32) for ci in range(grp)]
                    work.append((d, rows, parts))
            for d, rows, parts in work:
                o_ref[d, rows, :] += jnp.concatenate(parts, axis=0)
            return carry

        lax.fori_loop(0, n // (grp * per), body, 0)

    def finish(o_s, gt_ref, o_ref):
        o = _rms(o_s[0] + o_s[1], ng[...])
        o_ref[0] = (o * _silu(gt_ref[0].astype(F32))).astype(BF16)

    gates(lowc, gc)
    gates(lowl, gl)
    prepare(qc, kc, vc, gc, qgc, uc, decc, osc, Tc)
    prepare(ql, kl, vl, gl, qgl, ul, decl, osl, T)
    S0 = jnp.zeros((HEAD_DIM, HEAD_DIM), F32)
    Sf, Sb = recur(uc, decc, sbc, Tc // C, S0, S0)
    recur(ul, decl, sbl, T // C, Sf, Sb)
    inter(qgc, sbc, osc, Tc // C)
    inter(qgl, sbl, osl, T // C)
    finish(osl, gtl, ol_ref)
    finish(osc, gtc, oc_ref)


def _gla_call(Pl, Pc, upf, bias_f, upb, bias_b, norm_g, l):
    B, T, _ = Pl.shape
    Tc = Pc.shape[1]

    def colspec(Tx, col):
        return pl.BlockSpec((1, Tx, LANE), lambda b, h: (b, 0, col + h))

    def lowspec(Tx):
        return pl.BlockSpec((1, Tx, LANE), lambda b, h: (b, 0, COL_LOW))

    headw = pl.BlockSpec((None, LANE, LANE), lambda b, h: (l, 0, h))
    headv = pl.BlockSpec((None, 1, LANE), lambda b, h: (l, 0, h))
    outspec = lambda Tx: pl.BlockSpec((1, Tx, LANE), lambda b, h: (b, 0, h))

    def scratch(Tx):
        n = Tx // GLA_CHUNK
        return [pltpu.VMEM((2, Tx, LANE), F32),
                pltpu.VMEM((2, Tx, LANE), BF16),
                pltpu.VMEM((2, n, HEAD_DIM, HEAD_DIM), F32),
                pltpu.VMEM((2, n, 8, LANE), F32),
                pltpu.VMEM((2, n, HEAD_DIM, HEAD_DIM), BF16),
                pltpu.VMEM((2, Tx, LANE), F32)]

    return pl.pallas_call(
        _gla_kernel,
        out_shape=(jax.ShapeDtypeStruct((B, T, GROUP_W), BF16), jax.ShapeDtypeStruct((B, Tc, GROUP_W), BF16)),
        grid=(B, N_HEADS),
        in_specs=[colspec(T, COL_GQ), colspec(T, COL_GK), colspec(T, COL_GV), colspec(T, COL_GGATE), lowspec(T),
                  colspec(Tc, COL_GQ), colspec(Tc, COL_GK), colspec(Tc, COL_GV), colspec(Tc, COL_GGATE), lowspec(Tc),
                  headw, headv, headw, headv, _layer_spec(l, 1, LANE)],
        out_specs=(outspec(T), outspec(Tc)),
        scratch_shapes=scratch(T) + scratch(Tc),
        compiler_params=_cparams(2),
        name="gla_mixer",
    )(Pl, Pl, Pl, Pl, Pl, Pc, Pc, Pc, Pc, Pc, upf, bias_f, upb, bias_b, norm_g)


def _conv_kernel(prev_ref, cur_ref, next_ref, dw_ref, dwb_ref, lng_ref, lnb_ref, pw_ref, pwb_ref,
                 o_ref, pad_ref, sh_ref, acc_ref, *, tt):
    c = pl.program_id(1)
    H = HALO
    W = GROUP_W

    def glu(u):
        u = u.astype(F32)
        return u[:, :W] * jax.nn.sigmoid(u[:, W:])

    pad_ref[0:H, :] = jnp.where(c > 0, glu(prev_ref[0]), 0.0)
    pad_ref[H:H + tt, :] = glu(cur_ref[0])
    pad_ref[H + tt:H + tt + H, :] = jnp.where(c < pl.num_programs(1) - 1, glu(next_ref[0]), 0.0)

    n_sh = tt + 2 * H - SUBLANE
    sh_ref[0, :, :] = pad_ref[...]
    for r in range(1, SUBLANE):
        sh_ref[r, 0:n_sh, :] = pad_ref[r:r + n_sh, :]

    rs = min(tt, 128)
    off = H - CONV_K // 2
    for r0 in range(0, tt, rs):
        for lb in range(W // LANE):
            cols = slice(lb * LANE, (lb + 1) * LANE)
            acc = jnp.zeros((rs, LANE), F32) + dwb_ref[:, cols]
            for k in range(CONV_K):
                r = (off + k) % SUBLANE
                a0 = r0 + off + k - r
                acc = acc + sh_ref[r, a0:a0 + rs, cols] * dw_ref[k:k + 1, cols]
            acc_ref[r0:r0 + rs, cols] = acc

    h = acc_ref[...]
    mu = jnp.mean(h, axis=-1, keepdims=True)
    hc = h - mu
    var = jnp.mean(hc * hc, axis=-1, keepdims=True)
    y = hc * lax.rsqrt(var + NORM_EPS) * lng_ref[...] + lnb_ref[...]
    y = jnp.dot(_silu(y).astype(BF16), pw_ref[...], preferred_element_type=F32) + pwb_ref[...]
    o_ref[0] = y.astype(BF16)


def _conv_call(P, dw, dw_b, ln_g, ln_b, pw, pw_b, l):
    B, T, _ = P.shape
    tt = min(T, 512)
    vec = _layer_spec(l, 1, GROUP_W)
    return pl.pallas_call(
        functools.partial(_conv_kernel, tt=tt),
        out_shape=jax.ShapeDtypeStruct((B, T, GROUP_W), BF16),
        grid=(B, T // tt),
        in_specs=_halo_specs(T, tt, 2 * GROUP_W, 0) + [
            _layer_spec(l, CONV_K, GROUP_W), vec, vec, vec, _layer_spec(l, GROUP_W, GROUP_W), vec],
        out_specs=pl.BlockSpec((1, tt, GROUP_W), lambda b, c: (b, c, 0)),
        scratch_shapes=[pltpu.VMEM((tt + 2 * HALO, GROUP_W), F32),
                        pltpu.VMEM((SUBLANE, tt + 2 * HALO, GROUP_W), F32),
                        pltpu.VMEM((tt, GROUP_W), F32)],
        compiler_params=_cparams(2),
        name="conv_mixer",
    )(P, P, P, dw, dw_b, ln_g, ln_b, pw, pw_b)


def _rope(x, cos, sin):
    hw = ROPE_AXIS_DIM // 2
    lane = lax.broadcasted_iota(jnp.int32, x.shape, 1)
    first_half = (lane % (2 * hw)) < hw
    swapped = jnp.where(first_half, pltpu.roll(x, LANE - hw, 1), pltpu.roll(x, hw, 1))
    return x * cos + swapped * sin


def _diff_kernel(*refs, rope, lam_init, n_kv):
    it = iter(refs)
    q_ref = next(it)
    kv = [(next(it), next(it)) for _ in range(n_kv)]
    if rope:
        cq, sq, ck, sk = next(it), next(it), next(it), next(it)
    lq1, lk1, lq2, lk2, sg = next(it), next(it), next(it), next(it), next(it)
    o_ref, kbuf, vbuf = next(it), next(it), next(it)

    @pl.when(pl.program_id(2) == 0)
    def _():
        r0 = 0
        for i, (k_ref, v_ref) in enumerate(kv):
            n = k_ref.shape[1]
            k = k_ref[0]
            if rope and i == 0:
                k = _rope(k.astype(F32), ck[...], sk[...]).astype(BF16)
            kbuf[r0:r0 + n, :] = k
            vbuf[r0:r0 + n, 0:LANE] = v_ref[0]
            r0 += n
        vbuf[:, LANE:2 * LANE] = jnp.ones((vbuf.shape[0], LANE), BF16)

    tq = q_ref.shape[1]
    rs = min(tq, 256)
    k = kbuf[...]
    v1 = vbuf[...]
    lam = (jnp.exp(jnp.sum(lq1[...] * lk1[...], axis=-1, keepdims=True))
           - jnp.exp(jnp.sum(lq2[...] * lk2[...], axis=-1, keepdims=True)) + lam_init)
    scores = []
    for r0 in range(0, tq, rs):
        q = q_ref[0, r0:r0 + rs, :].astype(F32)
        if rope:
            q = _rope(q, cq[r0:r0 + rs, :], sq[r0:r0 + rs, :])
        q = q * (DIFF_QK_DIM ** -0.5 * math.log2(math.e))
        lane = lax.broadcasted_iota(jnp.int32, q.shape, 1)
        q1 = jnp.where(lane < DIFF_QK_DIM, q, 0.0).astype(BF16)
        q2 = jnp.where(lane >= DIFF_QK_DIM, q, 0.0).astype(BF16)
        scores.append((lax.dot_general(q1, k, NT_DIMS, preferred_element_type=F32),
                       lax.dot_general(q2, k, NT_DIMS, preferred_element_type=F32)))
    for r0, (s1, s2) in zip(range(0, tq, rs), scores):
        e1 = jnp.exp2(s1 - jnp.max(s1, axis=-1, keepdims=True)).astype(BF16)
        e2 = jnp.exp2(s2 - jnp.max(s2, axis=-1, keepdims=True)).astype(BF16)
        r1 = jnp.dot(e1, v1, preferred_element_type=F32)
        r2 = jnp.dot(e2, v1, preferred_element_type=F32)
        o = r1[:, :LANE] * (1.0 / r1[:, LANE:LANE + 1]) - r2[:, :LANE] * (lam / r2[:, LANE:LANE + 1])
        o_ref[0, r0:r0 + rs, :] = (_rms(o, sg[...]) * (1.0 - lam_init)).astype(BF16)


def _diff_call(Pq, kv_sources, rope_tabs, lq1, lk1, lq2, lk2, subln_g, lam_init, l):
    B, T, _ = Pq.shape
    tq = min(T, 1024)
    rope = rope_tabs is not None
    in_specs = [pl.BlockSpec((1, tq, LANE), lambda b, h, i: (b, i, COL_DQ + h))]
    args = [Pq]
    Tk = 0
    for Ps in kv_sources:
        n = Ps.shape[1]
        in_specs.append(pl.BlockSpec((1, n, LANE), lambda b, h, i: (b, 0, COL_DK + h)))
        in_specs.append(pl.BlockSpec((1, n, LANE), lambda b, h, i: (b, 0, COL_DV + h)))
        args += [Ps, Ps]
        Tk += n
    if rope:
        cos, sin = rope_tabs
        in_specs += [pl.BlockSpec((tq, LANE), lambda b, h, i: (i, 0)), pl.BlockSpec((tq, LANE), lambda b, h, i: (i, 0)),
                     pl.BlockSpec((T, LANE), lambda b, h, i: (0, 0)), pl.BlockSpec((T, LANE), lambda b, h, i: (0, 0))]
        args += [cos, sin, cos, sin]
    small = _layer_spec(l, 1, DIFF_QK_DIM)
    in_specs += [small, small, small, small, _layer_spec(l, 1, LANE)]
    args += [lq1, lk1, lq2, lk2, subln_g]
    return pl.pallas_call(
        functools.partial(_diff_kernel, rope=rope, lam_init=lam_init, n_kv=len(kv_sources)),
        out_shape=jax.ShapeDtypeStruct((B, T, GROUP_W), BF16),
        grid=(B, N_HEADS, T // tq),
        in_specs=in_specs,
        out_specs=pl.BlockSpec((1, tq, LANE), lambda b, h, i: (b, i, h)),
        scratch_shapes=[pltpu.VMEM((Tk, LANE), BF16), pltpu.VMEM((Tk, 2 * LANE), BF16)],
        compiler_params=_cparams(3),
        name="diff_attn",
    )(*args)


def _outproj_kernel(a_ref, b_ref, c_ref, d_ref, w_ref, x_ref, gate_ref, sh_ref, sc_ref, ng_ref, wr_ref,
                    x1_ref, h2_ref, aff_ref, mix_ref):
    W = GROUP_W
    tm = x_ref.shape[1]
    rs = min(tm, 256)
    for p, m_ref in enumerate((a_ref, b_ref, c_ref, d_ref)):
        mix_ref[:, p * W:(p + 1) * W] = m_ref[0]
    ys = [jnp.dot(mix_ref[r0:r0 + rs, :], w_ref[...], preferred_element_type=F32) for r0 in range(0, tm, rs)]
    for r0, y in zip(range(0, tm, rs), ys):
        rows = slice(r0, r0 + rs)
        x1 = x_ref[0, rows, :] + gate_ref[...] * y
        x1_ref[0, rows, :] = x1
        h = _norm_mod(x1, ng_ref[...], sc_ref[...], sh_ref[...])
        hh = h.astype(BF16)
        hl = (h - hh.astype(F32)).astype(BF16)
        h2_ref[0, rows, :] = hh
        lg2 = jnp.dot(hh, wr_ref[...], preferred_element_type=F32)
        lg = lg2[:, :LANE] + lg2[:, LANE:] + jnp.dot(hl, wr_ref[:, 0:LANE], preferred_element_type=F32)
        lane = lax.broadcasted_iota(jnp.int32, lg.shape, 1)
        lg = jnp.where(lane < N_EXPERTS, lg, -jnp.inf)
        e = jnp.exp(lg - jnp.max(lg, axis=-1, keepdims=True))
        aff_ref[0, rows, :] = e / jnp.sum(e, axis=-1, keepdims=True)


def _outproj_call(mix, w_out, x, mod, row_of, ng, wr2, l):
    Bx, T, D = x.shape
    tm = min(T, 512)
    mixspec = pl.BlockSpec((1, tm, GROUP_W), lambda b, i: (b, i, 0))
    rowspec = pl.BlockSpec((1, tm, D), lambda b, i: (b, i, 0))
    return pl.pallas_call(
        _outproj_kernel,
        out_shape=(jax.ShapeDtypeStruct((Bx, T, D), F32), jax.ShapeDtypeStruct((Bx, T, D), BF16),
                   jax.ShapeDtypeStruct((Bx, T, LANE), F32)),
        grid=(Bx, T // tm),
        in_specs=[mixspec, mixspec, mixspec, mixspec,
                  _layer_spec(l, D, D),
                  rowspec, _mod_spec(l, row_of, 2), _mod_spec(l, row_of, 3), _mod_spec(l, row_of, 4),
                  _layer_spec(l, 1, D), _layer_spec(l, D, 2 * LANE)],
        out_specs=(rowspec, rowspec, pl.BlockSpec((1, tm, LANE), lambda b, i: (b, i, 0))),
        scratch_shapes=[pltpu.VMEM((tm, D), BF16)],
        compiler_params=_cparams(2),
        name="out_proj",
    )(*mix, w_out, x, mod, mod, mod, ng, wr2)


def _excl_prefix(x):
    rows, T = x.shape
    ri = lax.broadcasted_iota(jnp.int32, (LANE, LANE), 0)
    ci = lax.broadcasted_iota(jnp.int32, (LANE, LANE), 1)
    upper = jnp.where(ri <= ci, 1.0, 0.0).astype(BF16)
    carry = jnp.zeros((rows, 1), F32)
    out = []
    for b in range(T // LANE):
        xb = x[:, b * LANE:(b + 1) * LANE]
        inc = jnp.dot(xb.astype(BF16), upper, preferred_element_type=F32)
        out.append(inc - xb + carry)
        carry = carry + jnp.sum(xb, axis=1, keepdims=True)
    return jnp.concatenate(out, axis=1)


def _route_kernel(aff_ref, slot_row_ref, slot_col_ref, aff_row_ref, st_ref, *, cap):
    bt, T, _ = aff_ref.shape
    E = N_EXPERTS
    arow = jnp.concatenate([aff_ref[bi].T[0:E, :] for bi in range(bt)], axis=0)
    keys = lax.bitcast_convert_type(arow, jnp.int32)
    v = jnp.zeros((bt * E, 1), jnp.int32)
    for bit in range(30, -1, -1):
        cand = v | (1 << bit)
        cnt = jnp.sum(jnp.where(keys >= cand, 1.0, 0.0), axis=1, keepdims=True)
        v = jnp.where(cnt >= cap, cand, v)
    above = keys > v
    tied = jnp.where(keys == v, 1.0, 0.0)
    room = cap - jnp.sum(jnp.where(above, 1.0, 0.0), axis=1, keepdims=True)
    kept = jnp.where(above, 1.0, jnp.where(_excl_prefix(tied) < room, tied, 0.0))
    slot = jnp.where(kept > 0.0, _excl_prefix(kept), float(T))
    st_ref[...] = jnp.full(st_ref.shape, float(T), F32)
    for bi in range(bt):
        for e in range(E):
            r = bi * E + e
            slot_row_ref[bi, e] = slot[r:r + 1, :].astype(jnp.int32)
            aff_row_ref[bi, e] = arow[r:r + 1, :]
        st_ref[0:E, :] = slot[bi * E:(bi + 1) * E, :]
        slot_col_ref[bi] = st_ref[...].T.astype(jnp.int32)


def _route_call(aff, cap):
    B, T, _ = aff.shape
    E = N_EXPERTS
    bt = 4 if B % 4 == 0 else (2 if B % 2 == 0 else 1)
    return pl.pallas_call(
        functools.partial(_route_kernel, cap=cap),
        out_shape=(jax.ShapeDtypeStruct((B, E, 1, T), jnp.int32),
                   jax.ShapeDtypeStruct((B, T, LANE), jnp.int32),
                   jax.ShapeDtypeStruct((B, E, 1, T), F32)),
        grid=(B // bt,),
        in_specs=[pl.BlockSpec((bt, T, LANE), lambda b: (b, 0, 0))],
        out_specs=(pl.BlockSpec((bt, E, 1, T), lambda b: (b, 0, 0, 0)),
                   pl.BlockSpec((bt, T, LANE), lambda b: (b, 0, 0)),
                   pl.BlockSpec((bt, E, 1, T), lambda b: (b, 0, 0, 0))),
        scratch_shapes=[pltpu.VMEM((LANE, T), F32)],
        compiler_params=_cparams(1),
        name="router_route",
    )(aff)


def _dispatch_kernel(slot_ref, affr_ref, h_ref, xe_ref, gs_ref):
    eg, cap, D = xe_ref.shape
    T = h_ref.shape[1]
    slot = lax.broadcasted_iota(jnp.int32, (cap, T), 0)
    hits = [slot_ref[0, e] == slot for e in range(eg)]
    onehot = jnp.concatenate([jnp.where(h, 1.0, 0.0).astype(BF16) for h in hits], axis=0)
    x = jnp.dot(onehot, h_ref[0], preferred_element_type=F32).astype(BF16)
    xe_ref[...] = x.reshape(eg, cap, D)
    for e in range(eg):
        g = jnp.sum(jnp.where(hits[e], affr_ref[0, e], 0.0), axis=-1, keepdims=True)
        gs_ref[e] = jnp.broadcast_to(g, (cap, LANE))


def _dispatch_call(slot_row, aff_row, h2, cap):
    B, T, D = h2.shape
    E = N_EXPERTS
    eg = max(1, min(E, 1024 // cap))
    return pl.pallas_call(
        _dispatch_kernel,
        out_shape=(jax.ShapeDtypeStruct((E, B * cap, D), BF16), jax.ShapeDtypeStruct((E, B * cap, LANE), F32)),
        grid=(B, E // eg),
        in_specs=[pl.BlockSpec((1, eg, 1, T), lambda b, g: (b, g, 0, 0)),
                  pl.BlockSpec((1, eg, 1, T), lambda b, g: (b, g, 0, 0)),
                  pl.BlockSpec((1, T, D), lambda b, g: (b, 0, 0))],
        out_specs=(pl.BlockSpec((eg, cap, D), lambda b, g: (g, b, 0)),
                   pl.BlockSpec((eg, cap, LANE), lambda b, g: (g, b, 0))),
        compiler_params=_cparams(2),
        name="moe_dispatch",
    )(slot_row, aff_row, h2)


def _expert_kernel(*refs, n_groups, nf):
    xs = refs[:n_groups]
    wg_ref, wu_ref, wd_ref = refs[n_groups:n_groups + 3]
    gss = refs[n_groups + 3:2 * n_groups + 3]
    ys = refs[2 * n_groups + 3:3 * n_groups + 3]
    hms = refs[3 * n_groups + 3:]
    s = pl.program_id(1)
    tf = wg_ref.shape[-1]

    @pl.when(s < nf)
    def _():
        wg = wg_ref[...].astype(BF16)
        wu = wu_ref[...].astype(BF16)
        for x_ref, hm_ref in zip(xs, hms):
            x = x_ref[...]
            a = jnp.dot(x, wg, preferred_element_type=F32)
            u = jnp.dot(x, wu, preferred_element_type=F32)
            hm_ref[s] = (_silu(a) * u).astype(BF16)

    @pl.when(s >= nf)
    def _():
        for hm_ref, gs_ref, y_ref in zip(hms, gss, ys):
            acc = jnp.dot(hm_ref[0], wd_ref[0:tf, :].astype(BF16), preferred_element_type=F32)
            for f in range(1, nf):
                acc = acc + jnp.dot(hm_ref[f], wd_ref[f * tf:(f + 1) * tf, :].astype(BF16),
                                    preferred_element_type=F32)
            y_ref[...] = (acc * gs_ref[...][:, 0:1]).astype(BF16)


def _expert_call(groups, w_gate, w_up, w_down, l):
    E, _, D = groups[0][0].shape
    F = w_gate.shape[-1]
    tf = 256
    tn = 512
    nf = F // tf
    nn = D // tn
    n = len(groups)
    Ms = [g[0].shape[1] for g in groups]
    nidx = lambda s: jnp.maximum(s - nf, 0)
    x_e = lambda e, s: jnp.minimum(e + (s >= nf).astype(jnp.int32), E - 1)
    w_e = lambda e, s: jnp.minimum(e + (s > nf).astype(jnp.int32), E - 1)
    w_f = lambda s: jnp.where(s > nf, 0, jnp.minimum(s, nf - 1))
    in_specs = ([pl.BlockSpec((None, M, D), lambda e, s: (x_e(e, s), 0, 0)) for M in Ms]
                + [pl.BlockSpec((None, None, D, tf), lambda e, s: (l, w_e(e, s), 0, w_f(s))),
                   pl.BlockSpec((None, None, D, tf), lambda e, s: (l, w_e(e, s), 0, w_f(s))),
                   pl.BlockSpec((None, None, F, tn), lambda e, s: (l, e, 0, nidx(s)))]
                + [pl.BlockSpec((None, M, LANE), lambda e, s: (e, 0, 0)) for M in Ms])
    return pl.pallas_call(
        functools.partial(_expert_kernel, n_groups=n, nf=nf),
        out_shape=[jax.ShapeDtypeStruct((E, M, D), BF16) for M in Ms],
        grid=(E, nf + nn),
        in_specs=in_specs,
        out_specs=[pl.BlockSpec((None, M, tn), lambda e, s: (e, 0, nidx(s))) for M in Ms],
        scratch_shapes=[pltpu.VMEM((nf, M, tf), BF16) for M in Ms],
        compiler_params=_cparams(2),
        name="moe_experts",
    )(*[g[0] for g in groups], w_gate, w_up, w_down, *[g[1] for g in groups])


def _combine_kernel(slot_ref, y_ref, x_ref, gate_ref, fg_ref, o_ref, *, final_norm):
    E, cap, D = y_ref.shape
    tm = x_ref.shape[1]
    slot = slot_ref[0]
    if E * cap <= 4 * MXU_N and cap & (cap - 1) == 0 and cap % 16 == 0:
        K = E * cap
        shift = cap.bit_length() - 1
        col = lax.broadcasted_iota(jnp.int32, (LANE, K), 1)
        row = lax.broadcasted_iota(jnp.int32, (LANE, K), 0)
        expand = jnp.where(jnp.right_shift(col, shift) == row, 1.0, 0.0).astype(BF16)
        spread = jnp.dot(slot.astype(F32).astype(BF16), expand, preferred_element_type=F32)
        want = (lax.broadcasted_iota(jnp.int32, (tm, K), 1) & (cap - 1)).astype(F32)
        hit = jnp.where(spread == want, 1.0, 0.0).astype(BF16)
        acc = jnp.dot(hit, y_ref[...].reshape(K, D), preferred_element_type=F32)
    else:
        lane_slot = lax.broadcasted_iota(jnp.int32, (tm, cap), 1)
        acc = jnp.zeros(x_ref.shape[1:], F32)
        for e in range(E):
            hit = jnp.where(slot[:, e:e + 1] == lane_slot, 1.0, 0.0).astype(BF16)
            acc = acc + jnp.dot(hit, y_ref[e], preferred_element_type=F32)
    x2 = x_ref[0] + gate_ref[...] * acc
    if final_norm:
        x2 = _rms(x2, fg_ref[...])
    o_ref[0] = x2


def _combine_call(slot_col, y, x1, mod, row_of, final_g, cap, final_norm, l):
    B, T, D = x1.shape
    tm = min(T, 512)
    return pl.pallas_call(
        functools.partial(_combine_kernel, final_norm=final_norm),
        out_shape=jax.ShapeDtypeStruct((B, T, D), F32),
        grid=(B, T // tm),
        in_specs=[pl.BlockSpec((1, tm, LANE), lambda b, i: (b, i, 0)),
                  pl.BlockSpec((N_EXPERTS, cap, D), lambda b, i: (0, b, 0)),
                  pl.BlockSpec((1, tm, D), lambda b, i: (b, i, 0)),
                  _mod_spec(l, row_of, 5),
                  pl.BlockSpec((1, D), lambda b, i: (0, 0))],
        out_specs=pl.BlockSpec((1, tm, D), lambda b, i: (b, i, 0)),
        compiler_params=_cparams(2),
        name="moe_combine",
    )(slot_col, y, x1, mod, final_g)


def _rope_tables(n):
    rows = n // GRID_W
    row = jnp.repeat(jnp.arange(rows), GRID_W).astype(F32)
    col = jnp.tile(jnp.arange(GRID_W), rows).astype(F32)
    inv_freq = ROPE_BASE ** (-jnp.arange(0, ROPE_AXIS_DIM, 2, dtype=F32) / ROPE_AXIS_DIM)
    ar = row[:, None] * inv_freq
    ac = col[:, None] * inv_freq
    cos = jnp.concatenate([jnp.cos(ar), jnp.cos(ar), jnp.cos(ac), jnp.cos(ac)], axis=-1)
    sin = jnp.concatenate([-jnp.sin(ar), jnp.sin(ar), -jnp.sin(ac), jnp.sin(ac)], axis=-1)
    return jnp.tile(cos, (1, 2)), jnp.tile(sin, (1, 2))


def _reorder_kernel(w_ref, o_ref):
    G = GROUP_W
    low = 2 * GLA_GATE_RANK
    o = 0
    for src, n in ((5 * G + low, 2 * G), (0, 5 * G), (7 * G + low, 3 * G), (5 * G, low)):
        o_ref[:, o:o + n] = w_ref[:, src:src + n]
        o += n
    o_ref[:, o:] = jnp.zeros((o_ref.shape[0], o_ref.shape[1] - o), BF16)


def _reorder_w_in(w_in):
    L, D, W = w_in.shape
    tr = 256
    return pl.pallas_call(
        _reorder_kernel,
        out_shape=jax.ShapeDtypeStruct((L, D, IN_COLS), BF16),
        grid=(L, D // tr),
        in_specs=[pl.BlockSpec((None, tr, W), lambda l, i: (l, i, 0))],
        out_specs=pl.BlockSpec((None, tr, IN_COLS), lambda l, i: (l, i, 0)),
        compiler_params=_cparams(2),
        name="w_in_reorder",
    )(w_in)


def kernel(x, c, ctx, c_ctx, w_ada, b_ada, norm1_g, norm2_g, w_in, pool_w, pool_scale, gla_gk_up_f, gla_gk_bias_f, gla_gk_up_b, gla_gk_bias_b, gla_norm_g, conv_dw, conv_dw_b, conv_ln_g, conv_ln_b, conv_pw, conv_pw_b, diff_lq1, diff_lk1, diff_lq2, diff_lk2, diff_subln_g, w_out, w_router, w_exp_gate, w_exp_up, w_exp_down, final_norm_g):
    B, N, D = x.shape
    Tc = ctx.shape[1]
    L = w_ada.shape[0]
    assert D == D_MODEL and N % 256 == 0 and Tc % 128 == 0 and B < MOD_ROWS

    cc = jnp.concatenate([c, c_ctx[None, :], jnp.zeros((MOD_ROWS - B - 1, D), F32)], axis=0)
    mod = _ada_call(cc, w_ada, b_ada).reshape(L, MOD_ROWS, 6, 1, D)
    lat_row = lambda b, *_: b
    ctx_row = lambda *_: B

    rows3 = lambda a: a.reshape(L, 1, -1)
    w_in_r = _reorder_w_in(w_in.astype(BF16))
    w_out_b = w_out.astype(BF16)
    pool_w_b = pool_w.astype(BF16)
    conv_pw_b16 = conv_pw.astype(BF16)
    R = GLA_GATE_RANK
    zpad = lambda a, lo: jnp.pad(a, ((0, 0), (lo, LANE - R - lo), (0, 0))).astype(BF16)
    upf = zpad(gla_gk_up_f, 0)
    upb = zpad(gla_gk_up_b, R)
    wr = jnp.pad(w_router, ((0, 0), (0, 0), (0, LANE - N_EXPERTS)))
    wr_hi = wr.astype(BF16)
    wr2 = jnp.concatenate([wr_hi, (wr - wr_hi.astype(F32)).astype(BF16)], axis=-1)
    rope_tabs = _rope_tables(N)
    fg = final_norm_g.reshape(1, D)
    n1, n2 = rows3(norm1_g), rows3(norm2_g)
    gla_args = (upf, rows3(gla_gk_bias_f), upb, rows3(gla_gk_bias_b), rows3(gla_norm_g))
    conv_args = (conv_dw, rows3(conv_dw_b), rows3(conv_ln_g), rows3(conv_ln_b), conv_pw_b16, rows3(conv_pw_b))
    diff_vecs = (rows3(diff_lq1), rows3(diff_lk1), rows3(diff_lq2), rows3(diff_lk2), rows3(diff_subln_g))
    pool_args = (pool_w_b, rows3(pool_scale))
    cap_l = EC_CAPACITY * N // N_EXPERTS
    cap_c = EC_CAPACITY * Tc // N_EXPERTS
    flat = lambda a: a.reshape(1, B * Tc, a.shape[-1])
    unflat = lambda a: a.reshape(B, Tc, a.shape[-1])

    for l in range(L):
        last = l == L - 1
        lam_init = 0.8 - 0.6 * math.exp(-0.3 * l)

        Pl = _inproj_call(x, mod, lat_row, n1, w_in_r, l)
        Pc = unflat(_inproj_call(flat(ctx), mod, ctx_row, n1, w_in_r, l))

        gla_l, gla_c = _gla_call(Pl, Pc, *gla_args, l)
        pool_l = _pool_call(Pl, *pool_args, l)
        conv_l = _conv_call(Pl, *conv_args, l)
        diff_l = _diff_call(Pl, [Pl, Pc], rope_tabs, *diff_vecs, lam_init, l)
        x1, h2, aff = _outproj_call((pool_l, gla_l, conv_l, diff_l), w_out_b, x, mod, lat_row, n2, wr2, l)
        slot_row, slot_col, aff_row = _route_call(aff, cap_l)
        groups = [_dispatch_call(slot_row, aff_row, h2, cap_l)]

        if not last:
            pool_c = _pool_call(Pc, *pool_args, l)
            conv_c = _conv_call(Pc, *conv_args, l)
            diff_c = _diff_call(Pc, [Pc], None, *diff_vecs, lam_init, l)
            c1, h2c, affc = _outproj_call((flat(pool_c), flat(gla_c), flat(conv_c), flat(diff_c)), w_out_b,
                                          flat(ctx), mod, ctx_row, n2, wr2, l)
            slot_row_c, slot_col_c, aff_row_c = _route_call(unflat(affc), cap_c)
            groups.append(_dispatch_call(slot_row_c, aff_row_c, unflat(h2c), cap_c))

        ys = _expert_call(groups, w_exp_gate, w_exp_up, w_exp_down, l)
        x = _combine_call(slot_col, ys[0], x1, mod, lat_row, fg, cap_l, last, l)
        if not last:
            ctx = _combine_call(slot_col_c, ys[1], unflat(c1), mod, ctx_row, fg, cap_c, False, l)

    return x
```

```python
import functools
import math

import jax
import jax.numpy as jnp
from jax import lax
from jax.experimental import pallas as pl
from jax.experimental.pallas import tpu as pltpu

F32 = jnp.float32
BF16 = jnp.bfloat16

D_MODEL = 2048
GRID_W = 64
GROUP_W = D_MODEL // 4
POOL_WINDOWS = (2, 4, 8, 16)
HEAD_DIM = 128
N_HEADS = GROUP_W // HEAD_DIM
GLA_GATE_RANK = 16
GLA_GATE_NORM = 16.0
GLA_CHUNK = 64
CONV_K = 31
DIFF_QK_DIM = 64
ROPE_BASE = 10000.0
ROPE_AXIS_DIM = DIFF_QK_DIM // 2
N_EXPERTS = 16
EXPERT_FF = D_MODEL // 2
EC_CAPACITY = 2
NORM_EPS = 1e-6

LANE = 128
SUBLANE = 8
HALO = 16
VMEM_LIMIT = 56 * 1024 * 1024
MOD_ROWS = 16

COL_CONV = 0
COL_POOL = 8
COL_GQ = 12
COL_GK = 16
COL_GV = 20
COL_GGATE = 24
COL_DQ = 28
COL_DK = 32
COL_DV = 36
COL_LOW = 40
MXU_N = 256
IN_COLS = 42 * LANE
IN_TN = 7 * MXU_N

NT_DIMS = (((1,), (1,)), ((), ()))
TN_DIMS = (((0,), (0,)), ((), ()))


def _cparams(n_axes):
    return pltpu.CompilerParams(dimension_semantics=("arbitrary",) * n_axes, vmem_limit_bytes=VMEM_LIMIT)


def _silu(x):
    return x * jax.nn.sigmoid(x)


def _rms(x, g):
    return x * lax.rsqrt(jnp.mean(x * x, axis=-1, keepdims=True) + NORM_EPS) * g


def _norm_mod(x, g, scale, shift):
    inv = lax.rsqrt(jnp.mean(x * x, axis=-1, keepdims=True) + NORM_EPS)
    return x * inv * (g * (1.0 + scale)) + shift


def _layer_spec(l, *tail):
    return pl.BlockSpec((None,) + tail, lambda *_: (l,) + (0,) * len(tail))


def _mod_spec(l, row_of, k):
    return pl.BlockSpec((None, None, None, 1, D_MODEL), lambda *g: (l, row_of(*g), k, 0, 0))


def _ada_kernel(c_ref, w_ref, b_ref, o_ref):
    sc = _silu(c_ref[...]).astype(BF16)
    o_ref[0] = jnp.dot(sc, w_ref[0].astype(BF16), preferred_element_type=F32) + b_ref[0]


def _ada_call(cc, w_ada, b_ada):
    L, D, W = w_ada.shape
    R = cc.shape[0]
    tn = 1536
    return pl.pallas_call(
        _ada_kernel,
        out_shape=jax.ShapeDtypeStruct((L, R, W), F32),
        grid=(L, W // tn),
        in_specs=[
            pl.BlockSpec((R, D), lambda l, j: (0, 0)),
            pl.BlockSpec((1, D, tn), lambda l, j: (l, 0, j)),
            pl.BlockSpec((1, 1, tn), lambda l, j: (l, 0, j)),
        ],
        out_specs=pl.BlockSpec((1, R, tn), lambda l, j: (l, 0, j)),
        compiler_params=_cparams(2),
        name="ada_mod",
    )(cc, w_ada, b_ada.reshape(L, 1, W))


def _inproj_kernel(x_ref, sh_ref, sc_ref, g_ref, w_ref, o_ref, h_ref):
    @pl.when(pl.program_id(2) == 0)
    def _():
        h_ref[...] = _norm_mod(x_ref[0], g_ref[...], sc_ref[...], sh_ref[...]).astype(BF16)

    o_ref[0] = jnp.dot(h_ref[...], w_ref[...], preferred_element_type=F32).astype(BF16)


def _inproj_call(x, mod, row_of, g, w, l):
    Bx, T, D = x.shape
    tm = min(T, 1024)
    return pl.pallas_call(
        _inproj_kernel,
        out_shape=jax.ShapeDtypeStruct((Bx, T, IN_COLS), BF16),
        grid=(Bx, T // tm, IN_COLS // IN_TN),
        in_specs=[
            pl.BlockSpec((1, tm, D), lambda b, i, j: (b, i, 0)),
            _mod_spec(l, row_of, 0),
            _mod_spec(l, row_of, 1),
            _layer_spec(l, 1, D),
            pl.BlockSpec((None, D, IN_TN), lambda b, i, j: (l, 0, j)),
        ],
        out_specs=pl.BlockSpec((1, tm, IN_TN), lambda b, i, j: (b, i, j)),
        scratch_shapes=[pltpu.VMEM((tm, D), BF16)],
        compiler_params=_cparams(3),
        name="in_proj",
    )(x, mod, mod, g, w)


def _halo_specs(T, tt, width, col_block):
    r = tt // HALO
    nh = T // HALO
    return [
        pl.BlockSpec((1, HALO, width), lambda b, c: (b, jnp.maximum(c * r - 1, 0), col_block)),
        pl.BlockSpec((1, tt, width), lambda b, c: (b, c, col_block)),
        pl.BlockSpec((1, HALO, width), lambda b, c: (b, jnp.minimum((c + 1) * r, nh - 1), col_block)),
    ]


def _pool_kernel(prev_ref, cur_ref, next_ref, w_ref, s_ref, o_ref, pad_ref, *, T, tt):
    c = pl.program_id(1)
    H = HALO
    pad_ref[0:H, :] = jnp.where(c > 0, prev_ref[0].astype(F32), 0.0)
    pad_ref[H:H + tt, :] = cur_ref[0].astype(F32)
    pad_ref[H + tt:H + tt + H, :] = jnp.where(c < pl.num_programs(1) - 1, next_ref[0].astype(F32), 0.0)
    rs = min(tt, 256)
    hw = max(POOL_WINDOWS) // 2
    ri = lax.broadcasted_iota(jnp.int32, (rs, rs + 2 * hw), 0)
    ci = lax.broadcasted_iota(jnp.int32, (rs, rs + 2 * hw), 1) - hw
    bands = [jnp.where((ci >= ri - w // 2) & (ci < ri + w // 2), 1.0, 0.0).astype(BF16) for w in POOL_WINDOWS]
    work = [(r0, gi) for r0 in range(0, tt, rs) for gi in range(len(POOL_WINDOWS))]
    sums = []
    for r0, gi in work:
        cols = slice(gi * HEAD_DIM, (gi + 1) * HEAD_DIM)
        win = pad_ref[r0 + H - hw:r0 + H + rs + hw, cols].astype(BF16)
        sums.append(jnp.dot(bands[gi], win, preferred_element_type=F32))
    for (r0, gi), acc in zip(work, sums):
        w = POOL_WINDOWS[gi]
        cols = slice(gi * HEAD_DIM, (gi + 1) * HEAD_DIM)
        t = c * tt + r0 + lax.broadcasted_iota(jnp.int32, (rs, HEAD_DIM), 0)
        cnt = jnp.minimum(t + w // 2, T) - jnp.maximum(t - w // 2, 0)
        p = acc / cnt.astype(F32) - pad_ref[r0 + H:r0 + H + rs, cols]
        y = jnp.dot(p.astype(BF16), w_ref[gi], preferred_element_type=F32)
        o_ref[0, r0:r0 + rs, cols] = (y * s_ref[:, cols]).astype(BF16)


def _pool_call(P, pool_w, pool_scale, l):
    B, T, _ = P.shape
    tt = min(T, 1024)
    return pl.pallas_call(
        functools.partial(_pool_kernel, T=T, tt=tt),
        out_shape=jax.ShapeDtypeStruct((B, T, GROUP_W), BF16),
        grid=(B, T // tt),
        in_specs=_halo_specs(T, tt, GROUP_W, COL_POOL * LANE // GROUP_W) + [
            _layer_spec(l, len(POOL_WINDOWS), HEAD_DIM, HEAD_DIM),
            _layer_spec(l, 1, GROUP_W),
        ],
        out_specs=pl.BlockSpec((1, tt, GROUP_W), lambda b, c: (b, c, 0)),
        scratch_shapes=[pltpu.VMEM((tt + 2 * HALO, GROUP_W), F32)],
        compiler_params=_cparams(2),
        name="pool_mixer",
    )(P, P, P, pool_w, pool_scale)


def _log_decay(z):
    t = jnp.exp2(jnp.abs(z) * (-math.log2(math.e)))
    return jnp.minimum(z, 0.0) * (1.0 / GLA_GATE_NORM) - jnp.log2(1.0 + t) * (math.log(2.0) / GLA_GATE_NORM)


def _gla_block_consts(rb):
    ri = lax.broadcasted_iota(jnp.int32, (rb, rb), 0)
    ci = lax.broadcasted_iota(jnp.int32, (rb, rb), 1)
    shift = GLA_CHUNK.bit_length() - 1
    same = jnp.right_shift(ri, shift) == jnp.right_shift(ci, shift)
    out = []
    for causal in (ci <= ri, ci >= ri):
        tri = jnp.where(causal, jnp.where(same, 1.0, 0.0), 0.0)
        out.append((tri.astype(BF16), tri > 0.0))
    return out


def _gla_prepare(items, consts, q_ref, k_ref, v_ref, g_ref, qg_ref, u_ref, dec_ref, o_ref):
    C = GLA_CHUNK
    rb = consts[0][1].shape[0]
    nchunk = rb // C
    Gs = []
    for d, rows, _ in items:
        tri = consts[d][0]
        g = g_ref[d, rows, :]
        g_hi = g.astype(BF16)
        g_lo = (g - g_hi.astype(F32)).astype(BF16)
        GG = jnp.dot(tri, jnp.concatenate([g_hi, g_lo], axis=1), preferred_element_type=F32)
        Gs.append(GG[:, :LANE] + GG[:, LANE:])
    staged = []
    for (d, rows, c0), G in zip(items, Gs):
        tot = C - 1 if d == 0 else 0
        tots = [G[ci * C + tot:ci * C + tot + 1, :] for ci in range(nchunk)]
        Gt = jnp.concatenate([jnp.broadcast_to(t, (C, LANE)) for t in tots], axis=0)
        r = 0.5 * Gt
        q = q_ref[0, rows, :].astype(F32) * (HEAD_DIM ** -0.5)
        k = k_ref[0, rows, :].astype(F32)
        qg = (q * jnp.exp(G - r)).astype(BF16)
        kg = (k * jnp.exp(r - G)).astype(BF16)
        qg_ref[d, rows, :] = (q * jnp.exp(G)).astype(BF16)
        kd = (k * jnp.exp(Gt - G)).astype(BF16)
        for ci in range(nchunk):
            dec_ref[d, c0 + ci] = jnp.exp(jnp.broadcast_to(tots[ci], (8, LANE)))
        staged.append((qg, kg, kd))
    atts = [lax.dot_general(qg, kg, NT_DIMS, preferred_element_type=F32) for qg, kg, _ in staged]
    for (d, rows, c0), att, (_, _, kd) in zip(items, atts, staged):
        v = v_ref[0, rows, :]
        att = jnp.where(consts[d][1], att, 0.0).astype(BF16)
        o_ref[d, rows, :] = jnp.dot(att, v, preferred_element_type=F32)
        for ci in range(nchunk):
            cr = slice(ci * C, (ci + 1) * C)
            u_ref[d, c0 + ci] = lax.dot_general(v[cr], kd[cr], TN_DIMS, preferred_element_type=F32)


def _gla_kernel(ql, kl, vl, gtl, lowl, qc, kc, vc, gtc, lowc, upf, bf, upb, bb, ng,
                ol_ref, oc_ref, gl, qgl, ul, decl, sbl, osl, gc, qgc, uc, decc, sbc, osc):
    C = GLA_CHUNK
    T = ql.shape[1]
    Tc = qc.shape[1]

    def gates(low_ref, g_ref):
        up = jnp.concatenate([upf[...], upb[...]], axis=1)
        z = jnp.dot(low_ref[0], up, preferred_element_type=F32)
        g_ref[0] = _log_decay(z[:, :LANE] + bf[...])
        g_ref[1] = _log_decay(z[:, LANE:] + bb[...])

    def prepare(q_ref, k_ref, v_ref, g_ref, qg_ref, u_ref, dec_ref, o_ref, Tx):
        rb = min(Tx, 256)
        nb = Tx // rb
        per = 4 if nb % 4 == 0 else (2 if nb % 2 == 0 else 1)
        consts = _gla_block_consts(rb)

        def body(j, carry):
            items = []
            for p in range(per):
                blk = j * per + p
                rows = pl.ds(pl.multiple_of(blk * rb, rb), rb)
                items += [(d, rows, blk * (rb // C)) for d in (0, 1)]
            _gla_prepare(items, consts, q_ref, k_ref, v_ref, g_ref, qg_ref, u_ref, dec_ref, o_ref)
            return carry

        lax.fori_loop(0, nb // per, body, 0)

    def recur(u_ref, dec_ref, sb_ref, n, Sf, Sb):
        def step(d, c, S):
            sb_ref[d, c] = S.astype(BF16)
            return S * dec_ref[d, c, 0:1, :] + u_ref[d, c]

        def body(i, carry):
            Sf, Sb = carry
            return step(0, i, Sf), step(1, n - 1 - i, Sb)

        return lax.fori_loop(0, n, body, (Sf, Sb), unroll=min(n, 4))

    def inter(qg_ref, sb_ref, o_ref, n):
        grp = min(n, 4)
        nb = n // grp
        per = 4 if nb % 4 == 0 else (2 if nb % 2 == 0 else 1)

        def body(j, carry):
            work = []
            for p in range(per):
                blk = j * per + p
                rows = pl.ds(pl.multiple_of(blk * grp * C, grp * C), grp * C)
                for d in (0, 1):
                    qg = qg_ref[d, rows, :]
                    parts = [lax.dot_general(qg[ci * C:(ci + 1) * C], sb_ref[d, blk * grp + ci], NT_DIMS,
                                             preferred_element_type=F32) for ci in range(grp)]
                    work.append((d, rows, parts))
            for d, rows, parts in work:
                o_ref[d, rows, :] += jnp.concatenate(parts, axis=0)
            return carry

        lax.fori_loop(0, n // (grp * per), body, 0)

    def finish(o_s, gt_ref, o_ref):
        o = _rms(o_s[0] + o_s[1], ng[...])
        o_ref[0] = (o * _silu(gt_ref[0].astype(F32))).astype(BF16)

    gates(lowc, gc)
    gates(lowl, gl)
    prepare(qc, kc, vc, gc, qgc, uc, decc, osc, Tc)
    prepare(ql, kl, vl, gl, qgl, ul, decl, osl, T)
    S0 = jnp.zeros((HEAD_DIM, HEAD_DIM), F32)
    Sf, Sb = recur(uc, decc, sbc, Tc // C, S0, S0)
    recur(ul, decl, sbl, T // C, Sf, Sb)
    inter(qgc, sbc, osc, Tc // C)
    inter(qgl, sbl, osl, T // C)
    finish(osl, gtl, ol_ref)
    finish(osc, gtc, oc_ref)


def _gla_call(Pl, Pc, upf, bias_f, upb, bias_b, norm_g, l):
    B, T, _ = Pl.shape
    Tc = Pc.shape[1]

    def colspec(Tx, col):
        return pl.BlockSpec((1, Tx, LANE), lambda b, h: (b, 0, col + h))

    def lowspec(Tx):
        return pl.BlockSpec((1, Tx, LANE), lambda b, h: (b, 0, COL_LOW))

    headw = pl.BlockSpec((None, LANE, LANE), lambda b, h: (l, 0, h))
    headv = pl.BlockSpec((None, 1, LANE), lambda b, h: (l, 0, h))
    outspec = lambda Tx: pl.BlockSpec((1, Tx, LANE), lambda b, h: (b, 0, h))

    def scratch(Tx):
        n = Tx // GLA_CHUNK
        return [pltpu.VMEM((2, Tx, LANE), F32),
                pltpu.VMEM((2, Tx, LANE), BF16),
                pltpu.VMEM((2, n, HEAD_DIM, HEAD_DIM), F32),
                pltpu.VMEM((2, n, 8, LANE), F32),
                pltpu.VMEM((2, n, HEAD_DIM, HEAD_DIM), BF16),
                pltpu.VMEM((2, Tx, LANE), F32)]

    return pl.pallas_call(
        _gla_kernel,
        out_shape=(jax.ShapeDtypeStruct((B, T, GROUP_W), BF16), jax.ShapeDtypeStruct((B, Tc, GROUP_W), BF16)),
        grid=(B, N_HEADS),
        in_specs=[colspec(T, COL_GQ), colspec(T, COL_GK), colspec(T, COL_GV), colspec(T, COL_GGATE), lowspec(T),
                  colspec(Tc, COL_GQ), colspec(Tc, COL_GK), colspec(Tc, COL_GV), colspec(Tc, COL_GGATE), lowspec(Tc),
                  headw, headv, headw, headv, _layer_spec(l, 1, LANE)],
        out_specs=(outspec(T), outspec(Tc)),
        scratch_shapes=scratch(T) + scratch(Tc),
        compiler_params=_cparams(2),
        name="gla_mixer",
    )(Pl, Pl, Pl, Pl, Pl, Pc, Pc, Pc, Pc, Pc, upf, bias_f, upb, bias_b, norm_g)


def _conv_kernel(prev_ref, cur_ref, next_ref, dw_ref, dwb_ref, lng_ref, lnb_ref, pw_ref, pwb_ref,
                 o_ref, pad_ref, sh_ref, acc_ref, *, tt):
    c = pl.program_id(1)
    H = HALO
    W = GROUP_W

    def glu(u):
        u = u.astype(F32)
        return u[:, :W] * jax.nn.sigmoid(u[:, W:])

    pad_ref[0:H, :] = jnp.where(c > 0, glu(prev_ref[0]), 0.0)
    pad_ref[H:H + tt, :] = glu(cur_ref[0])
    pad_ref[H + tt:H + tt + H, :] = jnp.where(c < pl.num_programs(1) - 1, glu(next_ref[0]), 0.0)

    n_sh = tt + 2 * H - SUBLANE
    sh_ref[0, :, :] = pad_ref[...]
    for r in range(1, SUBLANE):
        sh_ref[r, 0:n_sh, :] = pad_ref[r:r + n_sh, :]

    rs = min(tt, 128)
    off = H - CONV_K // 2
    for r0 in range(0, tt, rs):
        for lb in range(W // LANE):
            cols = slice(lb * LANE, (lb + 1) * LANE)
            acc = jnp.zeros((rs, LANE), F32) + dwb_ref[:, cols]
            for k in range(CONV_K):
                r = (off + k) % SUBLANE
                a0 = r0 + off + k - r
                acc = acc + sh_ref[r, a0:a0 + rs, cols] * dw_ref[k:k + 1, cols]
            acc_ref[r0:r0 + rs, cols] = acc

    h = acc_ref[...]
    mu = jnp.mean(h, axis=-1, keepdims=True)
    hc = h - mu
    var = jnp.mean(hc * hc, axis=-1, keepdims=True)
    y = hc * lax.rsqrt(var + NORM_EPS) * lng_ref[...] + lnb_ref[...]
    y = jnp.dot(_silu(y).astype(BF16), pw_ref[...], preferred_element_type=F32) + pwb_ref[...]
    o_ref[0] = y.astype(BF16)


def _conv_call(P, dw, dw_b, ln_g, ln_b, pw, pw_b, l):
    B, T, _ = P.shape
    tt = min(T, 512)
    vec = _layer_spec(l, 1, GROUP_W)
    return pl.pallas_call(
        functools.partial(_conv_kernel, tt=tt),
        out_shape=jax.ShapeDtypeStruct((B, T, GROUP_W), BF16),
        grid=(B, T // tt),
        in_specs=_halo_specs(T, tt, 2 * GROUP_W, 0) + [
            _layer_spec(l, CONV_K, GROUP_W), vec, vec, vec, _layer_spec(l, GROUP_W, GROUP_W), vec],
        out_specs=pl.BlockSpec((1, tt, GROUP_W), lambda b, c: (b, c, 0)),
        scratch_shapes=[pltpu.VMEM((tt + 2 * HALO, GROUP_W), F32),
                        pltpu.VMEM((SUBLANE, tt + 2 * HALO, GROUP_W), F32),
                        pltpu.VMEM((tt, GROUP_W), F32)],
        compiler_params=_cparams(2),
        name="conv_mixer",
    )(P, P, P, dw, dw_b, ln_g, ln_b, pw, pw_b)


def _rope(x, cos, sin):
    hw = ROPE_AXIS_DIM // 2
    lane = lax.broadcasted_iota(jnp.int32, x.shape, 1)
    first_half = (lane % (2 * hw)) < hw
    swapped = jnp.where(first_half, pltpu.roll(x, LANE - hw, 1), pltpu.roll(x, hw, 1))
    return x * cos + swapped * sin


def _diff_kernel(*refs, rope, lam_init, n_kv):
    it = iter(refs)
    q_ref = next(it)
    kv = [(next(it), next(it)) for _ in range(n_kv)]
    if rope:
        cq, sq, ck, sk = next(it), next(it), next(it), next(it)
    lq1, lk1, lq2, lk2, sg = next(it), next(it), next(it), next(it), next(it)
    o_ref, kbuf, vbuf = next(it), next(it), next(it)

    @pl.when(pl.program_id(2) == 0)
    def _():
        r0 = 0
        for i, (k_ref, v_ref) in enumerate(kv):
            n = k_ref.shape[1]
            k = k_ref[0]
            if rope and i == 0:
                k = _rope(k.astype(F32), ck[...], sk[...]).astype(BF16)
            kbuf[r0:r0 + n, :] = k
            vbuf[r0:r0 + n, 0:LANE] = v_ref[0]
            r0 += n
        vbuf[:, LANE:2 * LANE] = jnp.ones((vbuf.shape[0], LANE), BF16)

    tq = q_ref.shape[1]
    rs = min(tq, 256)
    k = kbuf[...]
    v1 = vbuf[...]
    lam = (jnp.exp(jnp.sum(lq1[...] * lk1[...], axis=-1, keepdims=True))
           - jnp.exp(jnp.sum(lq2[...] * lk2[...], axis=-1, keepdims=True)) + lam_init)
    def scores(r0):
        q = q_ref[0, r0:r0 + rs, :].astype(F32)
        if rope:
            q = _rope(q, cq[r0:r0 + rs, :], sq[r0:r0 + rs, :])
        q = q * (DIFF_QK_DIM ** -0.5 * math.log2(math.e))
        lane = lax.broadcasted_iota(jnp.int32, q.shape, 1)
        q1 = jnp.where(lane < DIFF_QK_DIM, q, 0.0).astype(BF16)
        q2 = jnp.where(lane >= DIFF_QK_DIM, q, 0.0).astype(BF16)
        return (lax.dot_general(q1, k, NT_DIMS, preferred_element_type=F32),
                lax.dot_general(q2, k, NT_DIMS, preferred_element_type=F32))

    starts = list(range(0, tq, rs))
    ahead = min(len(starts), 3)
    pending = [scores(r0) for r0 in starts[:ahead]]
    for i, r0 in enumerate(starts):
        if i + ahead < len(starts):
            pending.append(scores(starts[i + ahead]))
        s1, s2 = pending.pop(0)
        e1 = jnp.exp2(s1 - jnp.max(s1, axis=-1, keepdims=True)).astype(BF16)
        e2 = jnp.exp2(s2 - jnp.max(s2, axis=-1, keepdims=True)).astype(BF16)
        r1 = jnp.dot(e1, v1, preferred_element_type=F32)
        r2 = jnp.dot(e2, v1, preferred_element_type=F32)
        o = r1[:, :LANE] * (1.0 / r1[:, LANE:LANE + 1]) - r2[:, :LANE] * (lam / r2[:, LANE:LANE + 1])
        o_ref[0, r0:r0 + rs, :] = (_rms(o, sg[...]) * (1.0 - lam_init)).astype(BF16)


def _diff_call(Pq, kv_sources, rope_tabs, lq1, lk1, lq2, lk2, subln_g, lam_init, l):
    B, T, _ = Pq.shape
    tq = min(T, 2048)
    rope = rope_tabs is not None
    in_specs = [pl.BlockSpec((1, tq, LANE), lambda b, h, i: (b, i, COL_DQ + h))]
    args = [Pq]
    Tk = 0
    for Ps in kv_sources:
        n = Ps.shape[1]
        in_specs.append(pl.BlockSpec((1, n, LANE), lambda b, h, i: (b, 0, COL_DK + h)))
        in_specs.append(pl.BlockSpec((1, n, LANE), lambda b, h, i: (b, 0, COL_DV + h)))
        args += [Ps, Ps]
        Tk += n
    if rope:
        cos, sin = rope_tabs
        in_specs += [pl.BlockSpec((tq, LANE), lambda b, h, i: (i, 0)), pl.BlockSpec((tq, LANE), lambda b, h, i: (i, 0)),
                     pl.BlockSpec((T, LANE), lambda b, h, i: (0, 0)), pl.BlockSpec((T, LANE), lambda b, h, i: (0, 0))]
        args += [cos, sin, cos, sin]
    small = _layer_spec(l, 1, DIFF_QK_DIM)
    in_specs += [small, small, small, small, _layer_spec(l, 1, LANE)]
    args += [lq1, lk1, lq2, lk2, subln_g]
    return pl.pallas_call(
        functools.partial(_diff_kernel, rope=rope, lam_init=lam_init, n_kv=len(kv_sources)),
        out_shape=jax.ShapeDtypeStruct((B, T, GROUP_W), BF16),
        grid=(B, N_HEADS, T // tq),
        in_specs=in_specs,
        out_specs=pl.BlockSpec((1, tq, LANE), lambda b, h, i: (b, i, h)),
        scratch_shapes=[pltpu.VMEM((Tk, LANE), BF16), pltpu.VMEM((Tk, 2 * LANE), BF16)],
        compiler_params=_cparams(3),
        name="diff_attn",
    )(*args)


def _outproj_kernel(a_ref, b_ref, c_ref, d_ref, w_ref, x_ref, gate_ref, sh_ref, sc_ref, ng_ref, wr_ref,
                    x1_ref, h2_ref, aff_ref, mix_ref):
    W = GROUP_W
    tm = x_ref.shape[1]
    rs = min(tm, 256)
    for p, m_ref in enumerate((a_ref, b_ref, c_ref, d_ref)):
        mix_ref[:, p * W:(p + 1) * W] = m_ref[0]
    ys = [jnp.dot(mix_ref[r0:r0 + rs, :], w_ref[...], preferred_element_type=F32) for r0 in range(0, tm, rs)]
    for r0, y in zip(range(0, tm, rs), ys):
        rows = slice(r0, r0 + rs)
        x1 = x_ref[0, rows, :] + gate_ref[...] * y
        x1_ref[0, rows, :] = x1
        h = _norm_mod(x1, ng_ref[...], sc_ref[...], sh_ref[...])
        hh = h.astype(BF16)
        hl = (h - hh.astype(F32)).astype(BF16)
        h2_ref[0, rows, :] = hh
        lg2 = jnp.dot(hh, wr_ref[...], preferred_element_type=F32)
        lg = lg2[:, :LANE] + lg2[:, LANE:] + jnp.dot(hl, wr_ref[:, 0:LANE], preferred_element_type=F32)
        lane = lax.broadcasted_iota(jnp.int32, lg.shape, 1)
        lg = jnp.where(lane < N_EXPERTS, lg, -jnp.inf)
        e = jnp.exp(lg - jnp.max(lg, axis=-1, keepdims=True))
        aff_ref[0, rows, :] = e / jnp.sum(e, axis=-1, keepdims=True)


def _outproj_call(mix, w_out, x, mod, row_of, ng, wr2, l):
    Bx, T, D = x.shape
    tm = min(T, 512)
    mixspec = pl.BlockSpec((1, tm, GROUP_W), lambda b, i: (b, i, 0))
    rowspec = pl.BlockSpec((1, tm, D), lambda b, i: (b, i, 0))
    return pl.pallas_call(
        _outproj_kernel,
        out_shape=(jax.ShapeDtypeStruct((Bx, T, D), F32), jax.ShapeDtypeStruct((Bx, T, D), BF16),
                   jax.ShapeDtypeStruct((Bx, T, LANE), F32)),
        grid=(Bx, T // tm),
        in_specs=[mixspec, mixspec, mixspec, mixspec,
                  _layer_spec(l, D, D),
                  rowspec, _mod_spec(l, row_of, 2), _mod_spec(l, row_of, 3), _mod_spec(l, row_of, 4),
                  _layer_spec(l, 1, D), _layer_spec(l, D, 2 * LANE)],
        out_specs=(rowspec, rowspec, pl.BlockSpec((1, tm, LANE), lambda b, i: (b, i, 0))),
        scratch_shapes=[pltpu.VMEM((tm, D), BF16)],
        compiler_params=_cparams(2),
        name="out_proj",
    )(*mix, w_out, x, mod, mod, mod, ng, wr2)


def _excl_prefix(x):
    rows, T = x.shape
    ri = lax.broadcasted_iota(jnp.int32, (LANE, LANE), 0)
    ci = lax.broadcasted_iota(jnp.int32, (LANE, LANE), 1)
    upper = jnp.where(ri <= ci, 1.0, 0.0).astype(BF16)
    carry = jnp.zeros((rows, 1), F32)
    out = []
    for b in range(T // LANE):
        xb = x[:, b * LANE:(b + 1) * LANE]
        inc = jnp.dot(xb.astype(BF16), upper, preferred_element_type=F32)
        out.append(inc - xb + carry)
        carry = carry + jnp.sum(xb, axis=1, keepdims=True)
    return jnp.concatenate(out, axis=1)


def _route_kernel(aff_ref, slot_row_ref, slot_col_ref, aff_row_ref, st_ref, *, cap):
    bt, T, _ = aff_ref.shape
    E = N_EXPERTS
    arow = jnp.concatenate([aff_ref[bi].T[0:E, :] for bi in range(bt)], axis=0)
    keys = lax.bitcast_convert_type(arow, jnp.int32)
    v = jnp.zeros((bt * E, 1), jnp.int32)
    for bit in range(30, -1, -1):
        cand = v | (1 << bit)
        cnt = jnp.sum(jnp.where(keys >= cand, 1.0, 0.0), axis=1, keepdims=True)
        v = jnp.where(cnt >= cap, cand, v)
    above = keys > v
    tied = jnp.where(keys == v, 1.0, 0.0)
    room = cap - jnp.sum(jnp.where(above, 1.0, 0.0), axis=1, keepdims=True)
    kept = jnp.where(above, 1.0, jnp.where(_excl_prefix(tied) < room, tied, 0.0))
    slot = jnp.where(kept > 0.0, _excl_prefix(kept), float(T))
    st_ref[...] = jnp.full(st_ref.shape, float(T), F32)
    for bi in range(bt):
        for e in range(E):
            r = bi * E + e
            slot_row_ref[bi, e] = slot[r:r + 1, :].astype(jnp.int32)
            aff_row_ref[bi, e] = arow[r:r + 1, :]
        st_ref[0:E, :] = slot[bi * E:(bi + 1) * E, :]
        slot_col_ref[bi] = st_ref[...].T.astype(jnp.int32)


def _route_call(aff, cap):
    B, T, _ = aff.shape
    E = N_EXPERTS
    bt = 4 if B % 4 == 0 else (2 if B % 2 == 0 else 1)
    return pl.pallas_call(
        functools.partial(_route_kernel, cap=cap),
        out_shape=(jax.ShapeDtypeStruct((B, E, 1, T), jnp.int32),
                   jax.ShapeDtypeStruct((B, T, LANE), jnp.int32),
                   jax.ShapeDtypeStruct((B, E, 1, T), F32)),
        grid=(B // bt,),
        in_specs=[pl.BlockSpec((bt, T, LANE), lambda b: (b, 0, 0))],
        out_specs=(pl.BlockSpec((bt, E, 1, T), lambda b: (b, 0, 0, 0)),
                   pl.BlockSpec((bt, T, LANE), lambda b: (b, 0, 0)),
                   pl.BlockSpec((bt, E, 1, T), lambda b: (b, 0, 0, 0))),
        scratch_shapes=[pltpu.VMEM((LANE, T), F32)],
        compiler_params=_cparams(1),
        name="router_route",
    )(aff)


def _dispatch_kernel(slot_ref, affr_ref, h_ref, xe_ref, gs_ref):
    eg, cap, D = xe_ref.shape
    T = h_ref.shape[1]
    slot = lax.broadcasted_iota(jnp.int32, (cap, T), 0)
    hits = [slot_ref[0, e] == slot for e in range(eg)]
    onehot = jnp.concatenate([jnp.where(h, 1.0, 0.0).astype(BF16) for h in hits], axis=0)
    x = jnp.dot(onehot, h_ref[0], preferred_element_type=F32).astype(BF16)
    xe_ref[...] = x.reshape(eg, cap, D)
    for e in range(eg):
        g = jnp.sum(jnp.where(hits[e], affr_ref[0, e], 0.0), axis=-1, keepdims=True)
        gs_ref[e] = jnp.broadcast_to(g, (cap, LANE))


def _dispatch_call(slot_row, aff_row, h2, cap):
    B, T, D = h2.shape
    E = N_EXPERTS
    eg = max(1, min(E, 1024 // cap))
    return pl.pallas_call(
        _dispatch_kernel,
        out_shape=(jax.ShapeDtypeStruct((E, B * cap, D), BF16), jax.ShapeDtypeStruct((E, B * cap, LANE), F32)),
        grid=(B, E // eg),
        in_specs=[pl.BlockSpec((1, eg, 1, T), lambda b, g: (b, g, 0, 0)),
                  pl.BlockSpec((1, eg, 1, T), lambda b, g: (b, g, 0, 0)),
                  pl.BlockSpec((1, T, D), lambda b, g: (b, 0, 0))],
        out_specs=(pl.BlockSpec((eg, cap, D), lambda b, g: (g, b, 0)),
                   pl.BlockSpec((eg, cap, LANE), lambda b, g: (g, b, 0))),
        compiler_params=_cparams(2),
        name="moe_dispatch",
    )(slot_row, aff_row, h2)


def _expert_kernel(*refs, n_groups, nf):
    xs = refs[:n_groups]
    wg_ref, wu_ref, wd_ref = refs[n_groups:n_groups + 3]
    gss = refs[n_groups + 3:2 * n_groups + 3]
    ys = refs[2 * n_groups + 3:3 * n_groups + 3]
    hms = refs[3 * n_groups + 3:]
    s = pl.program_id(1)
    tf = wg_ref.shape[-1]

    @pl.when(s < nf)
    def _():
        wg = wg_ref[...].astype(BF16)
        wu = wu_ref[...].astype(BF16)
        for x_ref, hm_ref in zip(xs, hms):
            x = x_ref[...]
            a = jnp.dot(x, wg, preferred_element_type=F32)
            u = jnp.dot(x, wu, preferred_element_type=F32)
            hm_ref[s] = (_silu(a) * u).astype(BF16)

    @pl.when(s >= nf)
    def _():
        for hm_ref, gs_ref, y_ref in zip(hms, gss, ys):
            acc = jnp.dot(hm_ref[0], wd_ref[0:tf, :].astype(BF16), preferred_element_type=F32)
            for f in range(1, nf):
                acc = acc + jnp.dot(hm_ref[f], wd_ref[f * tf:(f + 1) * tf, :].astype(BF16),
                                    preferred_element_type=F32)
            y_ref[...] = (acc * gs_ref[...][:, 0:1]).astype(BF16)


def _expert_call(groups, w_gate, w_up, w_down, l):
    E, _, D = groups[0][0].shape
    F = w_gate.shape[-1]
    tf = 256
    tn = 512
    nf = F // tf
    nn = D // tn
    n = len(groups)
    Ms = [g[0].shape[1] for g in groups]
    nidx = lambda s: jnp.maximum(s - nf, 0)
    x_e = lambda e, s: jnp.minimum(e + (s >= nf).astype(jnp.int32), E - 1)
    w_e = lambda e, s: jnp.minimum(e + (s > nf).astype(jnp.int32), E - 1)
    w_f = lambda s: jnp.where(s > nf, 0, jnp.minimum(s, nf - 1))
    in_specs = ([pl.BlockSpec((None, M, D), lambda e, s: (x_e(e, s), 0, 0)) for M in Ms]
                + [pl.BlockSpec((None, None, D, tf), lambda e, s: (l, w_e(e, s), 0, w_f(s))),
                   pl.BlockSpec((None, None, D, tf), lambda e, s: (l, w_e(e, s), 0, w_f(s))),
                   pl.BlockSpec((None, None, F, tn), lambda e, s: (l, e, 0, nidx(s)))]
                + [pl.BlockSpec((None, M, LANE), lambda e, s: (e, 0, 0)) for M in Ms])
    return pl.pallas_call(
        functools.partial(_expert_kernel, n_groups=n, nf=nf),
        out_shape=[jax.ShapeDtypeStruct((E, M, D), BF16) for M in Ms],
        grid=(E, nf + nn),
        in_specs=in_specs,
        out_specs=[pl.BlockSpec((None, M, tn), lambda e, s: (e, 0, nidx(s))) for M in Ms],
        scratch_shapes=[pltpu.VMEM((nf, M, tf), BF16) for M in Ms],
        compiler_params=_cparams(2),
        name="moe_experts",
    )(*[g[0] for g in groups], w_gate, w_up, w_down, *[g[1] for g in groups])


def _combine_kernel(slot_ref, y_ref, x_ref, gate_ref, fg_ref, o_ref, *, final_norm):
    E, cap, D = y_ref.shape
    tm = x_ref.shape[1]
    slot = slot_ref[0]
    if E * cap <= 4 * MXU_N and cap & (cap - 1) == 0 and cap % 16 == 0:
        K = E * cap
        shift = cap.bit_length() - 1
        col = lax.broadcasted_iota(jnp.int32, (LANE, K), 1)
        row = lax.broadcasted_iota(jnp.int32, (LANE, K), 0)
        expand = jnp.where(jnp.right_shift(col, shift) == row, 1.0, 0.0).astype(BF16)
        spread = jnp.dot(slot.astype(F32).astype(BF16), expand, preferred_element_type=F32)
        want = (lax.broadcasted_iota(jnp.int32, (tm, K), 1) & (cap - 1)).astype(F32)
        hit = jnp.where(spread == want, 1.0, 0.0).astype(BF16)
        acc = jnp.dot(hit, y_ref[...].reshape(K, D), preferred_element_type=F32)
    else:
        lane_slot = lax.broadcasted_iota(jnp.int32, (tm, cap), 1)
        acc = jnp.zeros(x_ref.shape[1:], F32)
        for e in range(E):
            hit = jnp.where(slot[:, e:e + 1] == lane_slot, 1.0, 0.0).astype(BF16)
            acc = acc + jnp.dot(hit, y_ref[e], preferred_element_type=F32)
    x2 = x_ref[0] + gate_ref[...] * acc
    if final_norm:
        x2 = _rms(x2, fg_ref[...])
    o_ref[0] = x2


def _combine_call(slot_col, y, x1, mod, row_of, final_g, cap, final_norm, l):
    B, T, D = x1.shape
    tm = min(T, 512)
    return pl.pallas_call(
        functools.partial(_combine_kernel, final_norm=final_norm),
        out_shape=jax.ShapeDtypeStruct((B, T, D), F32),
        grid=(B, T // tm),
        in_specs=[pl.BlockSpec((1, tm, LANE), lambda b, i: (b, i, 0)),
                  pl.BlockSpec((N_EXPERTS, cap, D), lambda b, i: (0, b, 0)),
                  pl.BlockSpec((1, tm, D), lambda b, i: (b, i, 0)),
                  _mod_spec(l, row_of, 5),
                  pl.BlockSpec((1, D), lambda b, i: (0, 0))],
        out_specs=pl.BlockSpec((1, tm, D), lambda b, i: (b, i, 0)),
        compiler_params=_cparams(2),
        name="moe_combine",
    )(slot_col, y, x1, mod, final_g)


def _rope_tables(n):
    rows = n // GRID_W
    row = jnp.repeat(jnp.arange(rows), GRID_W).astype(F32)
    col = jnp.tile(jnp.arange(GRID_W), rows).astype(F32)
    inv_freq = ROPE_BASE ** (-jnp.arange(0, ROPE_AXIS_DIM, 2, dtype=F32) / ROPE_AXIS_DIM)
    ar = row[:, None] * inv_freq
    ac = col[:, None] * inv_freq
    cos = jnp.concatenate([jnp.cos(ar), jnp.cos(ar), jnp.cos(ac), jnp.cos(ac)], axis=-1)
    sin = jnp.concatenate([-jnp.sin(ar), jnp.sin(ar), -jnp.sin(ac), jnp.sin(ac)], axis=-1)
    return jnp.tile(cos, (1, 2)), jnp.tile(sin, (1, 2))


def _reorder_kernel(w_ref, o_ref):
    G = GROUP_W
    low = 2 * GLA_GATE_RANK
    o = 0
    for src, n in ((5 * G + low, 2 * G), (0, 5 * G), (7 * G + low, 3 * G), (5 * G, low)):
        o_ref[:, o:o + n] = w_ref[:, src:src + n]
        o += n
    o_ref[:, o:] = jnp.zeros((o_ref.shape[0], o_ref.shape[1] - o), BF16)


def _reorder_w_in(w_in):
    L, D, W = w_in.shape
    tr = 256
    return pl.pallas_call(
        _reorder_kernel,
        out_shape=jax.ShapeDtypeStruct((L, D, IN_COLS), BF16),
        grid=(L, D // tr),
        in_specs=[pl.BlockSpec((None, tr, W), lambda l, i: (l, i, 0))],
        out_specs=pl.BlockSpec((None, tr, IN_COLS), lambda l, i: (l, i, 0)),
        compiler_params=_cparams(2),
        name="w_in_reorder",
    )(w_in)


def kernel(x, c, ctx, c_ctx, w_ada, b_ada, norm1_g, norm2_g, w_in, pool_w, pool_scale, gla_gk_up_f, gla_gk_bias_f, gla_gk_up_b, gla_gk_bias_b, gla_norm_g, conv_dw, conv_dw_b, conv_ln_g, conv_ln_b, conv_pw, conv_pw_b, diff_lq1, diff_lk1, diff_lq2, diff_lk2, diff_subln_g, w_out, w_router, w_exp_gate, w_exp_up, w_exp_down, final_norm_g):
    B, N, D = x.shape
    Tc = ctx.shape[1]
    L = w_ada.shape[0]
    assert D == D_MODEL and N % 256 == 0 and Tc % 128 == 0 and B < MOD_ROWS

    cc = jnp.concatenate([c, c_ctx[None, :], jnp.zeros((MOD_ROWS - B - 1, D), F32)], axis=0)
    mod = _ada_call(cc, w_ada, b_ada).reshape(L, MOD_ROWS, 6, 1, D)
    lat_row = lambda b, *_: b
    ctx_row = lambda *_: B

    rows3 = lambda a: a.reshape(L, 1, -1)
    w_in_r = _reorder_w_in(w_in.astype(BF16))
    w_out_b = w_out.astype(BF16)
    pool_w_b = pool_w.astype(BF16)
    conv_pw_b16 = conv_pw.astype(BF16)
    R = GLA_GATE_RANK
    zpad = lambda a, lo: jnp.pad(a, ((0, 0), (lo, LANE - R - lo), (0, 0))).astype(BF16)
    upf = zpad(gla_gk_up_f, 0)
    upb = zpad(gla_gk_up_b, R)
    wr = jnp.pad(w_router, ((0, 0), (0, 0), (0, LANE - N_EXPERTS)))
    wr_hi = wr.astype(BF16)
    wr2 = jnp.concatenate([wr_hi, (wr - wr_hi.astype(F32)).astype(BF16)], axis=-1)
    rope_tabs = _rope_tables(N)
    fg = final_norm_g.reshape(1, D)
    n1, n2 = rows3(norm1_g), rows3(norm2_g)
    gla_args = (upf, rows3(gla_gk_bias_f), upb, rows3(gla_gk_bias_b), rows3(gla_norm_g))
    conv_args = (conv_dw, rows3(conv_dw_b), rows3(conv_ln_g), rows3(conv_ln_b), conv_pw_b16, rows3(conv_pw_b))
    diff_vecs = (rows3(diff_lq1), rows3(diff_lk1), rows3(diff_lq2), rows3(diff_lk2), rows3(diff_subln_g))
    pool_args = (pool_w_b, rows3(pool_scale))
    cap_l = EC_CAPACITY * N // N_EXPERTS
    cap_c = EC_CAPACITY * Tc // N_EXPERTS
    flat = lambda a: a.reshape(1, B * Tc, a.shape[-1])
    unflat = lambda a: a.reshape(B, Tc, a.shape[-1])

    for l in range(L):
        last = l == L - 1
        lam_init = 0.8 - 0.6 * math.exp(-0.3 * l)

        Pl = _inproj_call(x, mod, lat_row, n1, w_in_r, l)
        Pc = unflat(_inproj_call(flat(ctx), mod, ctx_row, n1, w_in_r, l))

        gla_l, gla_c = _gla_call(Pl, Pc, *gla_args, l)
        pool_l = _pool_call(Pl, *pool_args, l)
        conv_l = _conv_call(Pl, *conv_args, l)
        diff_l = _diff_call(Pl, [Pl, Pc], rope_tabs, *diff_vecs, lam_init, l)
        x1, h2, aff = _outproj_call((pool_l, gla_l, conv_l, diff_l), w_out_b, x, mod, lat_row, n2, wr2, l)
        slot_row, slot_col, aff_row = _route_call(aff, cap_l)
        groups = [_dispatch_call(slot_row, aff_row, h2, cap_l)]

        if not last:
            pool_c = _pool_call(Pc, *pool_args, l)
            conv_c = _conv_call(Pc, *conv_args, l)
            diff_c = _diff_call(Pc, [Pc], None, *diff_vecs, lam_init, l)
            c1, h2c, affc = _outproj_call((flat(pool_c), flat(gla_c), flat(conv_c), flat(diff_c)), w_out_b,
                                          flat(ctx), mod, ctx_row, n2, wr2, l)
            slot_row_c, slot_col_c, aff_row_c = _route_call(unflat(affc), cap_c)
            groups.append(_dispatch_call(slot_row_c, aff_row_c, unflat(h2c), cap_c))

        ys = _expert_call(groups, w_exp_gate, w_exp_up, w_exp_down, l)
        x = _combine_call(slot_col, ys[0], x1, mod, lat_row, fg, cap_l, last, l)
        if not last:
            ctx = _combine_call(slot_col_c, ys[1], unflat(c1), mod, ctx_row, fg, cap_c, False, l)

    return x
```
